```python
import math
import jax, jax.numpy as jnp
from jax import lax
import numpy as np

D_MODEL = 1024
BATCH = 4
SEQ = 4096
DEPTH = 1

MIX_WIDTH = D_MODEL
DIFF_WIDTH = MIX_WIDTH // 2
GLA_WIDTH = MIX_WIDTH - DIFF_WIDTH
DIFF_DH = 64
DIFF_HEADS = DIFF_WIDTH // (2 * DIFF_DH)
Q_BLOCK = 128
GLA_HEADS = 4
GLA_DV = GLA_WIDTH // GLA_HEADS
GLA_DK = GLA_DV // 2
GLA_RANK = 16
GLA_TAU = 16.0
GLA_CHUNK = 64
D_FF = 2816
CONV_W = 3
ROPE_THETA = 10000.0
LN_EPS = 1e-5
DN_ALPHA = (2.0 * DEPTH) ** 0.25
DN_BETA = (8.0 * DEPTH) ** -0.25

N_DQ = DIFF_HEADS * 2 * DIFF_DH
N_DK = DIFF_HEADS * 2 * DIFF_DH
N_DV = DIFF_WIDTH
N_GQ = GLA_HEADS * GLA_DK
N_GK = GLA_HEADS * GLA_DK
N_GV = GLA_WIDTH
N_GR = GLA_WIDTH
N_GG = GLA_RANK
IN_SPLITS = tuple(np.cumsum([N_DQ, N_DK, N_DV, N_GQ, N_GK, N_GV, N_GR])[:].tolist())
N_IN = N_DQ + N_DK + N_DV + N_GQ + N_GK + N_GV + N_GR + N_GG

kernel_name = "hymba_diffattn_gla_convffn_deepnorm_adaln"


def layer_norm(x, g, b):
    xf = x.astype(jnp.float32)
    mu = jnp.mean(xf, axis=-1, keepdims=True)
    var = jnp.mean(jnp.square(xf - mu), axis=-1, keepdims=True)
    return ((xf - mu) * lax.rsqrt(var + LN_EPS) * g + b).astype(x.dtype)


def rms_norm(x, w):
    xf = x.astype(jnp.float32)
    return (xf * lax.rsqrt(jnp.mean(jnp.square(xf), axis=-1, keepdims=True) + LN_EPS) * w).astype(x.dtype)


def rope_tables(positions, dim):
    inv = ROPE_THETA ** (-jnp.arange(0, dim, 2, dtype=jnp.float32) / dim)
    ang = positions.astype(jnp.float32)[..., None] * inv
    return jnp.cos(ang), jnp.sin(ang)


def apply_rope(x, cos, sin):
    half = x.shape[-1] // 2
    x1, x2 = x[..., :half].astype(jnp.float32), x[..., half:].astype(jnp.float32)
    return jnp.concatenate([x1 * cos - x2 * sin, x2 * cos + x1 * sin], axis=-1).astype(x.dtype)


def diff_attention(q, k, v, lam):
    B, S, H, _, dh = q.shape
    nb = S // Q_BLOCK
    qb = q.reshape(B, nb, Q_BLOCK, H, 2, dh).transpose(1, 0, 3, 4, 2, 5)
    kt = k.transpose(0, 2, 3, 1, 4)
    vt = v.transpose(0, 2, 1, 3)
    key_idx = jnp.arange(S)
    scale = 1.0 / math.sqrt(dh)

    def block(args):
        qi, i = args
        s = jnp.einsum('bhmqd,bhmkd->bhmqk', qi, kt).astype(jnp.float32) * scale
        q_idx = i * Q_BLOCK + jnp.arange(Q_BLOCK)
        mask = key_idx[None, :] <= q_idx[:, None]
        p = jax.nn.softmax(jnp.where(mask, s, -jnp.inf), axis=-1)
        a = p[:, :, 0] - lam * p[:, :, 1]
        return jnp.einsum('bhqk,bhkv->bhqv', a.astype(vt.dtype), vt)

    o = lax.map(block, (qb, jnp.arange(nb)))
    return o.transpose(1, 0, 3, 2, 4).reshape(B, S, H, 2 * dh)


def gla_chunked(q, k, v, g):
    B, S, H, dk = q.shape
    dv = v.shape[-1]
    n = S // GLA_CHUNK

    def chunks(t):
        return t.astype(jnp.float32).reshape(B, n, GLA_CHUNK, H, t.shape[-1]).transpose(1, 0, 3, 2, 4)

    qc, kc, vc, gc = chunks(q), chunks(k), chunks(v), chunks(g)
    bc = jnp.cumsum(gc, axis=-2)
    causal = jnp.tril(jnp.ones((GLA_CHUNK, GLA_CHUNK), dtype=bool))[:, :, None]

    def step(state, inp):
        qi, ki, vi, bi = inp
        diff = bi[:, :, :, None, :] - bi[:, :, None, :, :]
        decay = jnp.exp(jnp.where(causal, diff, -jnp.inf))
        attn = jnp.einsum('bhid,bhjd,bhijd->bhij', qi, ki, decay)
        o = attn @ vi + jnp.einsum('bhid,bhdv->bhiv', qi * jnp.exp(bi), state)
        b_last = bi[:, :, -1:, :]
        state = jnp.exp(b_last[:, :, 0, :])[..., None] * state + \
            jnp.einsum('bhjd,bhjv->bhdv', ki * jnp.exp(b_last - bi), vi)
        return state, o

    s0 = jnp.zeros((B, H, dk, dv), jnp.float32)
    _, o = lax.scan(step, s0, (qc, kc, vc, bc))
    return o.transpose(1, 0, 3, 2, 4).reshape(B, S, H, dv).astype(v.dtype)


def causal_depthwise_conv(h, w, b):
    S = h.shape[1]
    hp = jnp.pad(h, ((0, 0), (CONV_W - 1, 0), (0, 0)))
    out = b
    for j in range(CONV_W):
        out = out + w[j] * hp[:, j:j + S]
    return out


def setup_inputs(seed: int = 0) -> dict:
    key = jax.random.key(seed)
    ks = jax.random.split(key, 28)
    L, D, F = DEPTH, D_MODEL, D_FF

    def nrm(k, shape, scale):
        return jax.random.normal(k, shape, jnp.float32) * scale

    col_scale = jnp.concatenate([
        jnp.full((N_DQ + N_DK,), 1.0), jnp.full((N_DV,), DN_BETA),
        jnp.full((N_GQ + N_GK,), 1.0), jnp.full((N_GV,), DN_BETA),
        jnp.full((N_GR + N_GG,), 1.0)]).astype(jnp.float32) * D ** -0.5
    offsets = jax.random.randint(ks[2], (BATCH, 1), 0, 1024)
    positions = (offsets + jnp.arange(SEQ, dtype=jnp.int32)[None, :]).astype(jnp.int32)
    return {
        'x': nrm(ks[0], (BATCH, SEQ, D), 1.0),
        'c': nrm(ks[1], (BATCH, D), 1.0),
        'positions': positions,
        'ln_in_g': 1.0 + nrm(ks[3], (D,), 0.02),
        'ln_in_b': nrm(ks[4], (D,), 0.02),
        'w_ada': nrm(ks[5], (L, D, 6 * D), 0.2 * D ** -0.5),
        'b_ada': nrm(ks[6], (L, 6 * D), 0.02),
        'w_in': nrm(ks[7], (L, D, N_IN), 1.0) * col_scale,
        'lambda_q1': nrm(ks[8], (L, DIFF_DH), 0.1),
        'lambda_k1': nrm(ks[9], (L, DIFF_DH), 0.1),
        'lambda_q2': nrm(ks[10], (L, DIFF_DH), 0.1),
        'lambda_k2': nrm(ks[11], (L, DIFF_DH), 0.1),
        'diff_norm_w': 1.0 + nrm(ks[12], (L, 2 * DIFF_DH), 0.02),
        'gla_w_gate_up': nrm(ks[13], (L, GLA_RANK, GLA_HEADS * GLA_DK), GLA_RANK ** -0.5),
        'gla_b_gate': nrm(ks[14], (L, GLA_HEADS * GLA_DK), 0.1),
        'gla_norm_w': 1.0 + nrm(ks[15], (L, GLA_DV), 0.02),
        'w_out': nrm(ks[16], (L, MIX_WIDTH, D), DN_BETA * MIX_WIDTH ** -0.5),
        'ln_attn_g': 1.0 + nrm(ks[17], (L, D), 0.02),
        'ln_attn_b': nrm(ks[18], (L, D), 0.02),
        'w_up': nrm(ks[19], (L, D, 2 * F), D ** -0.5),
        'conv_w': nrm(ks[20], (L, CONV_W, 2 * F), CONV_W ** -0.5),
        'conv_b': nrm(ks[21], (L, 2 * F), 0.02),
        'w_down': nrm(ks[22], (L, F, D), DN_BETA * F ** -0.5),
        'ln_ffn_g': 1.0 + nrm(ks[23], (L, D), 0.02),
        'ln_ffn_b': nrm(ks[24], (L, D), 0.02),
    }


def reference(x, c, positions, ln_in_g, ln_in_b, w_ada, b_ada, w_in, lambda_q1, lambda_k1,
              lambda_q2, lambda_k2, diff_norm_w, gla_w_gate_up, gla_b_gate, gla_norm_w, w_out,
              ln_attn_g, ln_attn_b, w_up, conv_w, conv_b, w_down, ln_ffn_g, ln_ffn_b):
    B, S, D = x.shape
    cos, sin = rope_tables(positions, DIFF_DH)
    cos_d, sin_d = cos[:, :, None, None, :], sin[:, :, None, None, :]
    c_act = jax.nn.silu(c)
    h = layer_norm(x, ln_in_g, ln_in_b)

    for l in range(DEPTH):
        lambda_init = 0.8 - 0.6 * math.exp(-0.3 * l)
        ada = c_act @ w_ada[l] + b_ada[l]
        sh_a, sc_a, gt_a, sh_f, sc_f, gt_f = [t[:, None, :] for t in jnp.split(ada, 6, axis=-1)]

        u = h * (1.0 + sc_a) + sh_a
        proj = u @ w_in[l]
        dq, dk_, dv, gq, gk, gv, gr, gg = jnp.split(proj, IN_SPLITS, axis=-1)

        dq = apply_rope(dq.reshape(B, S, DIFF_HEADS, 2, DIFF_DH), cos_d, sin_d)
        dk_ = apply_rope(dk_.reshape(B, S, DIFF_HEADS, 2, DIFF_DH), cos_d, sin_d)
        dv = dv.reshape(B, S, DIFF_HEADS, 2 * DIFF_DH)
        lam = (jnp.exp(jnp.sum(lambda_q1[l].astype(jnp.float32) * lambda_k1[l]))
               - jnp.exp(jnp.sum(lambda_q2[l].astype(jnp.float32) * lambda_k2[l])) + lambda_init)
        d_out = diff_attention(dq, dk_, dv, lam)
        d_out = (rms_norm(d_out, diff_norm_w[l]) * (1.0 - lambda_init)).reshape(B, S, DIFF_WIDTH)

        gq = gq.reshape(B, S, GLA_HEADS, GLA_DK) * (GLA_DK ** -0.5)
        gk = gk.reshape(B, S, GLA_HEADS, GLA_DK)
        gv = gv.reshape(B, S, GLA_HEADS, GLA_DV)
        glog = jax.nn.log_sigmoid((gg @ gla_w_gate_up[l] + gla_b_gate[l]).astype(jnp.float32)) / GLA_TAU
        g_out = gla_chunked(gq, gk, gv, glog.reshape(B, S, GLA_HEADS, GLA_DK))
        g_out = rms_norm(g_out, gla_norm_w[l]).reshape(B, S, GLA_WIDTH) * jax.nn.silu(gr)

        mix = jnp.concatenate([d_out, g_out], axis=-1) @ w_out[l]
        h = layer_norm(DN_ALPHA * h + (1.0 + gt_a) * mix, ln_attn_g[l], ln_attn_b[l])

        u = h * (1.0 + sc_f) + sh_f
        up = causal_depthwise_conv(u @ w_up[l], conv_w[l], conv_b[l])
        a, bgate = jnp.split(up, 2, axis=-1)
        ff = (jax.nn.silu(a) * bgate) @ w_down[l]
        h = layer_norm(DN_ALPHA * h + (1.0 + gt_f) * ff, ln_ffn_g[l], ln_ffn_b[l])

    return h
```

```python
import functools
import math

import numpy as np
import jax
import jax.numpy as jnp
from jax import lax
from jax.experimental import pallas as pl
from jax.experimental.pallas import tpu as pltpu

F32 = jnp.float32
BF16 = jnp.bfloat16

D_MODEL = 1024
DIFF_DH = 64
DIFF_HEADS = 4
HEAD_W = 2 * DIFF_DH
GLA_HEADS = 4
GLA_DK = 64
GLA_DV = 128
GLA_KW = GLA_HEADS * GLA_DK
GLA_VW = GLA_HEADS * GLA_DV
GLA_RANK = 16
GLA_TAU = 16.0
GLA_CHUNK = 64
GLA_LEVELS = 6
D_FF = 2816
CONV_W = 3
ROPE_THETA = 10000.0
LN_EPS = 1e-5
DEPTH = 1
DN_ALPHA = (2.0 * DEPTH) ** 0.25
LAMBDA_INIT = 0.8 - 0.6 * math.exp(-0.3 * 0)

N_MAIN = 3072
N_PROJ = N_MAIN + 128
LANES = 128
NEG_BIG = -1e30

VMEM_LIMIT = 56 * 1024 * 1024


def _layer_norm(x, g, b):
    mu = jnp.mean(x, axis=-1, keepdims=True)
    xc = x - mu
    var = jnp.mean(xc * xc, axis=-1, keepdims=True)
    return xc * lax.rsqrt(var + LN_EPS) * g + b


def _silu(x):
    return x * jax.nn.sigmoid(x)


def _ada_kernel(c_ref, w_ref, b_ref, o_ref):
    ca = _silu(c_ref[...])
    o_ref[...] = jnp.dot(ca.astype(BF16), w_ref[...].astype(BF16),
                         preferred_element_type=F32) + b_ref[...]


def _ada(c_pad, w_ada, b_ada):
    rows, d = c_pad.shape
    n = w_ada.shape[1]
    tn = 1024
    return pl.pallas_call(
        _ada_kernel,
        grid=(n // tn,),
        in_specs=[pl.BlockSpec((rows, d), lambda j: (0, 0)),
                  pl.BlockSpec((d, tn), lambda j: (0, j)),
                  pl.BlockSpec((1, tn), lambda j: (0, j))],
        out_specs=pl.BlockSpec((rows, tn), lambda j: (0, j)),
        out_shape=jax.ShapeDtypeStruct((rows, n), F32),
        name="ada",
    )(c_pad, w_ada, b_ada)


def _proj_kernel(x_ref, pos_ref, inv_ref, lg_ref, lb_ref, sc_ref, sh_ref, w_ref, wg_ref, bg_ref,
                 qk_ref, dv_ref, gq_ref, gk_ref, gv_ref, gr_ref, gl_ref):
    x = x_ref[0]
    h = _layer_norm(x, lg_ref[...], lb_ref[...])
    u = (h * (1.0 + sc_ref[0]) + sh_ref[0]).astype(BF16)
    proj = jnp.dot(u, w_ref[...], preferred_element_type=F32)

    pos = pos_ref[0].astype(F32)
    ang = pos * inv_ref[...]
    cos = jnp.cos(ang)
    sin = jnp.sin(ang)
    lane = lax.broadcasted_iota(jnp.int32, (1, LANES), 1)
    first_half = (lane & 32) == 0
    sin_signed = jnp.where(first_half, -sin, sin)
    scale = 1.0 / math.sqrt(DIFF_DH)
    for j in range(8):
        xg = proj[:, j * LANES:(j + 1) * LANES]
        partner = jnp.where(first_half, pltpu.roll(xg, LANES - 32, axis=1), pltpu.roll(xg, 32, axis=1))
        r = xg * cos + partner * sin_signed
        if j < 4:
            r = r * scale
        qk_ref[0, :, j * LANES:(j + 1) * LANES] = r.astype(BF16)

    dv_ref[0] = proj[:, 1024:1536].astype(BF16)
    gq_ref[0] = (proj[:, 1536:1792] * (GLA_DK ** -0.5)).astype(BF16)
    gk_ref[0] = proj[:, 1792:2048].astype(BF16)
    gv_ref[0] = proj[:, 2048:2560].astype(BF16)
    gr_ref[0] = proj[:, 2560:3072].astype(BF16)
    gg = proj[:, N_MAIN:N_PROJ].astype(BF16)
    z = jnp.dot(gg, wg_ref[...], preferred_element_type=F32) + bg_ref[...]
    log_sig = jnp.minimum(z, 0.0) - jnp.log1p(jnp.exp(-jnp.abs(z)))
    gl_ref[0] = log_sig * (1.0 / GLA_TAU)


def _proj(x, pos3, inv, ln_g, ln_b, sc, sh, w, wg, bg, tm):
    B, S, D = x.shape
    grid = (B, S // tm)
    row = lambda b, s: (b, s, 0)
    const2 = lambda b, s: (0, 0)
    per_b = lambda b, s: (b, 0, 0)
    widths = (1024, 512, GLA_KW, GLA_KW, GLA_VW, GLA_VW)
    out_shape = [jax.ShapeDtypeStruct((B, S, n), BF16) for n in widths]
    out_shape.append(jax.ShapeDtypeStruct((B, S, GLA_KW), F32))
    out_specs = [pl.BlockSpec((1, tm, n), row) for n in widths]
    out_specs.append(pl.BlockSpec((1, tm, GLA_KW), row))
    return pl.pallas_call(
        _proj_kernel,
        grid=grid,
        in_specs=[pl.BlockSpec((1, tm, D), row),
                  pl.BlockSpec((1, tm, 1), row),
                  pl.BlockSpec((1, LANES), const2),
                  pl.BlockSpec((1, D), const2),
                  pl.BlockSpec((1, D), const2),
                  pl.BlockSpec((1, 1, D), per_b),
                  pl.BlockSpec((1, 1, D), per_b),
                  pl.BlockSpec((D, N_PROJ), const2),
                  pl.BlockSpec((LANES, GLA_KW), const2),
                  pl.BlockSpec((1, GLA_KW), const2)],
        out_specs=out_specs,
        out_shape=out_shape,
        compiler_params=pltpu.CompilerParams(
            dimension_semantics=("arbitrary", "arbitrary"), vmem_limit_bytes=VMEM_LIMIT),
        name="proj",
    )(x, pos3, inv, ln_g, ln_b, sc, sh, w, wg, bg)


def _diffattn_kernel(q_ref, k_ref, v_ref, lq1_ref, lk1_ref, lq2_ref, lk2_ref, nw_ref, o_ref,
                     m_ref, l_ref, acc_ref, *, tq):
    i = pl.program_id(2)
    lane = lax.broadcasted_iota(jnp.int32, (1, HEAD_W), 1)
    q = q_ref[0]
    zero = jnp.zeros_like(q)
    qz = jnp.concatenate([jnp.where(lane < DIFF_DH, q, zero), jnp.where(lane >= DIFF_DH, q, zero)], axis=0)

    m_ref[...] = jnp.full(m_ref.shape, NEG_BIG, F32)
    l_ref[...] = jnp.zeros(l_ref.shape, F32)
    acc_ref[...] = jnp.zeros(acc_ref.shape, F32)

    def step(j, masked):
        kb = k_ref[0, pl.ds(pl.multiple_of(j * tq, tq), tq), :]
        vb = v_ref[0, pl.ds(pl.multiple_of(j * tq, tq), tq), :]
        s = lax.dot_general(qz, kb, (((1,), (1,)), ((), ())), preferred_element_type=F32)
        if masked:
            r = lax.broadcasted_iota(jnp.int32, (2 * tq, tq), 0)
            c = lax.broadcasted_iota(jnp.int32, (2 * tq, tq), 1)
            r = jnp.where(r >= tq, r - tq, r)
            s = jnp.where(c <= r, s, NEG_BIG)
        s_fold = s[:, 0:LANES]
        for t in range(1, tq // LANES):
            s_fold = jnp.maximum(s_fold, s[:, t * LANES:(t + 1) * LANES])
        m_prev = m_ref[...]
        m_new = jnp.maximum(m_prev, jnp.max(s_fold, axis=1, keepdims=True))
        alpha = jnp.exp(m_prev - m_new)
        p = jnp.exp(s - pltpu.repeat(m_new, tq // LANES, axis=1))
        p_fold = p[:, 0:LANES]
        for t in range(1, tq // LANES):
            p_fold = p_fold + p[:, t * LANES:(t + 1) * LANES]
        l_ref[...] = alpha * l_ref[...] + p_fold
        acc_ref[...] = alpha * acc_ref[...] + jnp.dot(p.astype(BF16), vb, preferred_element_type=F32)
        m_ref[...] = m_new

    def body(j, carry):
        step(j, False)
        return carry

    lax.fori_loop(0, i, body, 0)
    step(i, True)

    l = jnp.sum(l_ref[...], axis=1, keepdims=True)
    o = acc_ref[...] / l
    lam = (jnp.exp(jnp.sum(lq1_ref[...] * lk1_ref[...], axis=1, keepdims=True))
           - jnp.exp(jnp.sum(lq2_ref[...] * lk2_ref[...], axis=1, keepdims=True)) + LAMBDA_INIT)
    d = o[0:tq] - lam * o[tq:2 * tq]
    ms = jnp.mean(d * d, axis=1, keepdims=True)
    d = d * lax.rsqrt(ms + LN_EPS) * nw_ref[...] * (1.0 - LAMBDA_INIT)
    o_ref[0] = d.astype(BF16)


def _diffattn(qk, dv, lq1, lk1, lq2, lk2, nw, tq):
    B, S, _ = qk.shape
    grid = (B, DIFF_HEADS, S // tq)
    const2 = lambda b, h, i: (0, 0)
    return pl.pallas_call(
        functools.partial(_diffattn_kernel, tq=tq),
        grid=grid,
        in_specs=[pl.BlockSpec((1, tq, HEAD_W), lambda b, h, i: (b, i, h)),
                  pl.BlockSpec((1, S, HEAD_W), lambda b, h, i: (b, 0, DIFF_HEADS + h)),
                  pl.BlockSpec((1, S, HEAD_W), lambda b, h, i: (b, 0, h)),
                  pl.BlockSpec((1, DIFF_DH), const2),
                  pl.BlockSpec((1, DIFF_DH), const2),
                  pl.BlockSpec((1, DIFF_DH), const2),
                  pl.BlockSpec((1, DIFF_DH), const2),
                  pl.BlockSpec((1, HEAD_W), const2)],
        out_specs=pl.BlockSpec((1, tq, HEAD_W), lambda b, h, i: (b, i, h)),
        out_shape=jax.ShapeDtypeStruct((B, S, DIFF_HEADS * HEAD_W), BF16),
        scratch_shapes=[pltpu.VMEM((2 * tq, LANES), F32),
                        pltpu.VMEM((2 * tq, LANES), F32),
                        pltpu.VMEM((2 * tq, HEAD_W), F32)],
        compiler_params=pltpu.CompilerParams(
            dimension_semantics=("arbitrary", "arbitrary", "arbitrary"), vmem_limit_bytes=VMEM_LIMIT),
        name="diffattn",
    )(qk, qk, dv, lq1, lk1, lq2, lk2, nw)


def _gla_tables():
    C = GLA_CHUNK
    t = np.arange(C)
    rng = np.zeros((2 + GLA_LEVELS, C, C), np.float32)
    rng[0] = (t[None, :] <= t[:, None])
    rng[1] = (t[None, :] > t[:, None])
    lvl_mask = np.zeros((GLA_LEVELS + 1, C, C), np.float32)
    for l in range(GLA_LEVELS):
        s = C >> (l + 1)
        blk = t // (2 * s)
        mid = blk * 2 * s + s
        upper = (t % (2 * s)) >= s
        for i in range(C):
            if upper[i]:
                rng[2 + l, i, mid[i]:i + 1] = 1.0
            else:
                rng[2 + l, i, i + 1:mid[i]] = 1.0
        lvl_mask[l] = (blk[:, None] == blk[None, :]) & upper[:, None] & (~upper[None, :])
    lvl_mask[GLA_LEVELS] = np.eye(C)
    rng = rng.reshape((2 + GLA_LEVELS) * C, C)
    rng3 = np.concatenate([rng, rng, rng], axis=1)
    lvl_mask = np.tile(lvl_mask, (1, 1, GLA_HEADS))
    hk = np.kron(np.eye(GLA_HEADS), np.ones((C, GLA_DK)))
    hv = np.kron(np.eye(GLA_HEADS), np.ones((C, GLA_DV)))
    hs = np.kron(np.eye(GLA_HEADS), np.ones((GLA_DV, GLA_DK)))
    return (jnp.asarray(rng3, BF16), jnp.asarray(lvl_mask, F32), jnp.asarray(hk, BF16),
            jnp.asarray(hv, BF16), jnp.asarray(hs, F32))


def _gla_kernel(q_ref, k_ref, v_ref, g_ref, r_ref, rng_ref, lm_ref, hk_ref, hv_ref, hs_ref, nw_ref,
                o_ref, st_ref, *, ts):
    C = GLA_CHUNK

    @pl.when(pl.program_id(1) == 0)
    def _():
        st_ref[...] = jnp.zeros(st_ref.shape, F32)

    for c in range(ts // C):
        rows = pl.ds(c * C, C)
        q = q_ref[0, rows, :].astype(F32)
        k = k_ref[0, rows, :].astype(F32)
        v = v_ref[0, rows, :]
        g = g_ref[0, rows, :]
        g_hi = g.astype(BF16)
        rem = g - g_hi.astype(F32)
        g_mid = rem.astype(BF16)
        g_lo = (rem - g_mid.astype(F32)).astype(BF16)
        g3 = jnp.concatenate([g_hi, g_mid, g_lo], axis=0)
        f = jnp.exp(jnp.dot(rng_ref[...], g3, preferred_element_type=F32))

        q_in = (q * f[0:C]).astype(BF16)
        k_out = (k * f[C:2 * C]).astype(BF16)
        decay = f[C - 1:C]

        attn = jnp.zeros((C, GLA_HEADS * C), F32)
        for l in range(GLA_LEVELS + 1):
            if l < GLA_LEVELS:
                fl = f[(2 + l) * C:(3 + l) * C]
                ql = (q * fl).astype(BF16)
                kl = (k * fl).astype(BF16)
            else:
                ql = q.astype(BF16)
                kl = k.astype(BF16)
            k_bd = jnp.concatenate([kl] * GLA_HEADS, axis=0) * hk_ref[...]
            a = lax.dot_general(ql, k_bd, (((1,), (1,)), ((), ())), preferred_element_type=F32)
            attn = attn + a * lm_ref[l]

        v_bd = jnp.concatenate([v] * GLA_HEADS, axis=0) * hv_ref[...]
        st = st_ref[...]
        o = jnp.dot(attn.astype(BF16), v_bd, preferred_element_type=F32)
        o = o + lax.dot_general(q_in, st.astype(BF16), (((1,), (1,)), ((), ())), preferred_element_type=F32)
        upd = lax.dot_general(v, k_out, (((0,), (0,)), ((), ())), preferred_element_type=F32)
        st_ref[...] = st * decay + upd * hs_ref[...]

        parts = []
        for h in range(GLA_HEADS):
            oh = o[:, h * GLA_DV:(h + 1) * GLA_DV]
            ms = jnp.mean(oh * oh, axis=1, keepdims=True)
            parts.append(oh * lax.rsqrt(ms + LN_EPS) * nw_ref[...])
        on = jnp.concatenate(parts, axis=1)
        o_ref[0, rows, :] = (on * _silu(r_ref[0, rows, :].astype(F32))).astype(BF16)


def _gla(gq, gk, gv, gl, gr, nw, ts):
    B, S, _ = gq.shape
    rng3, lvl_mask, hk, hv, hs = _gla_tables()
    row = lambda b, s: (b, s, 0)
    const2 = lambda b, s: (0, 0)
    const3 = lambda b, s: (0, 0, 0)
    return pl.pallas_call(
        functools.partial(_gla_kernel, ts=ts),
        grid=(B, S // ts),
        in_specs=[pl.BlockSpec((1, ts, GLA_KW), row),
                  pl.BlockSpec((1, ts, GLA_KW), row),
                  pl.BlockSpec((1, ts, GLA_VW), row),
                  pl.BlockSpec((1, ts, GLA_KW), row),
                  pl.BlockSpec((1, ts, GLA_VW), row),
                  pl.BlockSpec(rng3.shape, const2),
                  pl.BlockSpec(lvl_mask.shape, const3),
                  pl.BlockSpec(hk.shape, const2),
                  pl.BlockSpec(hv.shape, const2),
                  pl.BlockSpec(hs.shape, const2),
                  pl.BlockSpec((1, GLA_DV), const2)],
        out_specs=pl.BlockSpec((1, ts, GLA_VW), row),
        out_shape=jax.ShapeDtypeStruct((B, S, GLA_VW), BF16),
        scratch_shapes=[pltpu.VMEM((GLA_VW, GLA_KW), F32)],
        compiler_params=pltpu.CompilerParams(
            dimension_semantics=("arbitrary", "arbitrary"), vmem_limit_bytes=VMEM_LIMIT),
        name="gla",
    )(gq, gk, gv, gl, gr, rng3, lvl_mask, hk, hv, hs, nw)


def _outproj_kernel(x_ref, d_ref, g_ref, lg_ref, lb_ref, gt_ref, wd_ref, wg_ref, ag_ref, ab_ref, o_ref):
    h = _layer_norm(x_ref[0], lg_ref[...], lb_ref[...])
    mix = (jnp.dot(d_ref[0], wd_ref[...], preferred_element_type=F32)
           + jnp.dot(g_ref[0], wg_ref[...], preferred_element_type=F32))
    y = DN_ALPHA * h + (1.0 + gt_ref[0]) * mix
    o_ref[0] = _layer_norm(y, ag_ref[...], ab_ref[...])


def _outproj(x, d_out, g_out, ln_g, ln_b, gt, w_d, w_g, ag, ab, tm):
    B, S, D = x.shape
    row = lambda b, s: (b, s, 0)
    const2 = lambda b, s: (0, 0)
    per_b = lambda b, s: (b, 0, 0)
    return pl.pallas_call(
        _outproj_kernel,
        grid=(B, S // tm),
        in_specs=[pl.BlockSpec((1, tm, D), row),
                  pl.BlockSpec((1, tm, d_out.shape[2]), row),
                  pl.BlockSpec((1, tm, g_out.shape[2]), row),
                  pl.BlockSpec((1, D), const2),
                  pl.BlockSpec((1, D), const2),
                  pl.BlockSpec((1, 1, D), per_b),
                  pl.BlockSpec(w_d.shape, const2),
                  pl.BlockSpec(w_g.shape, const2),
                  pl.BlockSpec((1, D), const2),
                  pl.BlockSpec((1, D), const2)],
        out_specs=pl.BlockSpec((1, tm, D), row),
        out_shape=jax.ShapeDtypeStruct((B, S, D), F32),
        compiler_params=pltpu.CompilerParams(
            dimension_semantics=("arbitrary", "arbitrary"), vmem_limit_bytes=VMEM_LIMIT),
        name="outproj",
    )(x, d_out, g_out, ln_g, ln_b, gt, w_d, w_g, ag, ab)


def _ffn_kernel(h_ref, sc_ref, sh_ref, gt_ref, wu_ref, cw_ref, cb_ref, wd_ref, fg_ref, fb_ref, o_ref,
                carry_ref, ubuf_ref, acc_ref, *, tm, tf, nf):
    HALO = 8

    @pl.when(pl.program_id(1) == 0)
    def _():
        carry_ref[...] = jnp.zeros(carry_ref.shape, F32)

    h = h_ref[0]
    u = (h * (1.0 + sc_ref[0]) + sh_ref[0]).astype(BF16)
    acc_ref[...] = jnp.zeros(acc_ref.shape, F32)

    def conv_half(half, f):
        up = jnp.dot(u, wu_ref[half, f], preferred_element_type=F32)
        ubuf_ref[half, 0:HALO, :] = carry_ref[half, f]
        ubuf_ref[half, HALO:HALO + tm, :] = up
        carry_ref[half, f] = up[tm - HALO:tm, :]
        cw = cw_ref[half, f]
        return (cb_ref[half, f]
                + cw[0:1] * ubuf_ref[half, HALO - 2:HALO - 2 + tm, :]
                + cw[1:2] * ubuf_ref[half, HALO - 1:HALO - 1 + tm, :]
                + cw[2:3] * up)

    def body(f, carry):
        a = conv_half(0, f)
        b = conv_half(1, f)
        act = (_silu(a) * b).astype(BF16)
        acc_ref[...] += jnp.dot(act, wd_ref[f], preferred_element_type=F32)
        return carry

    lax.fori_loop(0, nf, body, 0)
    y = DN_ALPHA * h + (1.0 + gt_ref[0]) * acc_ref[...]
    o_ref[0] = _layer_norm(y, fg_ref[...], fb_ref[...])


def _ffn(h2, sc, sh, gt, wu, cw, cb, wd, fg, fb, tm):
    B, S, D = h2.shape
    _, nf, _, tf = wu.shape
    row = lambda b, s: (b, s, 0)
    const2 = lambda b, s: (0, 0)
    const3 = lambda b, s: (0, 0, 0)
    const4 = lambda b, s: (0, 0, 0, 0)
    per_b = lambda b, s: (b, 0, 0)
    return pl.pallas_call(
        functools.partial(_ffn_kernel, tm=tm, tf=tf, nf=nf),
        grid=(B, S // tm),
        in_specs=[pl.BlockSpec((1, tm, D), row),
                  pl.BlockSpec((1, 1, D), per_b),
                  pl.BlockSpec((1, 1, D), per_b),
                  pl.BlockSpec((1, 1, D), per_b),
                  pl.BlockSpec(wu.shape, const4),
                  pl.BlockSpec(cw.shape, const4),
                  pl.BlockSpec(cb.shape, const4),
                  pl.BlockSpec(wd.shape, const3),
                  pl.BlockSpec((1, D), const2),
                  pl.BlockSpec((1, D), const2)],
        out_specs=pl.BlockSpec((1, tm, D), row),
        out_shape=jax.ShapeDtypeStruct((B, S, D), F32),
        scratch_shapes=[pltpu.VMEM((2, nf, 8, tf), F32),
                        pltpu.VMEM((2, 8 + tm, tf), F32),
                        pltpu.VMEM((tm, D), F32)],
        compiler_params=pltpu.CompilerParams(
            dimension_semantics=("arbitrary", "arbitrary"), vmem_limit_bytes=VMEM_LIMIT),
        name="ffn",
    )(h2, sc, sh, gt, wu, cw, cb, wd, fg, fb)


def kernel(x, c, positions, ln_in_g, ln_in_b, w_ada, b_ada, w_in, lambda_q1, lambda_k1, lambda_q2, lambda_k2, diff_norm_w, gla_w_gate_up, gla_b_gate, gla_norm_w, w_out, ln_attn_g, ln_attn_b, w_up, conv_w, conv_b, w_down, ln_ffn_g, ln_ffn_b):
    B, S, D = x.shape
    assert D == D_MODEL and w_ada.shape[0] == 1
    tm = min(512, S)
    tf = 256
    nf = D_FF // tf

    c_pad = jnp.pad(c, ((0, 8 - B % 8 if B % 8 else 0), (0, 0)))
    ada = _ada(c_pad, w_ada[0], b_ada)[:B]
    sh_a, sc_a, gt_a, sh_f, sc_f, gt_f = [t[:, None, :] for t in jnp.split(ada, 6, axis=-1)]

    ln_g = ln_in_g[None, :]
    ln_b = ln_in_b[None, :]

    w_main = w_in[0, :, :N_MAIN]
    w_gg = jnp.pad(w_in[0, :, N_MAIN:], ((0, 0), (0, N_PROJ - w_in.shape[2])))
    w_proj = jnp.concatenate([w_main, w_gg], axis=1).astype(BF16)
    w_gate = jnp.pad(gla_w_gate_up[0], ((0, LANES - GLA_RANK), (0, 0))).astype(BF16)
    inv = ROPE_THETA ** (-jnp.arange(0, DIFF_DH, 2, dtype=F32) / DIFF_DH)
    inv = jnp.tile(inv, LANES // inv.shape[0])[None, :]
    qk, dv, gq, gk, gv, gr, gl = _proj(x, positions[:, :, None], inv, ln_g, ln_b, sc_a, sh_a,
                                       w_proj, w_gate, gla_b_gate, tm)

    d_out = _diffattn(qk, dv, lambda_q1, lambda_k1, lambda_q2, lambda_k2, diff_norm_w, min(512, S))
    g_out = _gla(gq, gk, gv, gl, gr, gla_norm_w, min(512, S))

    w_o = w_out[0].astype(BF16)
    h2 = _outproj(x, d_out, g_out, ln_g, ln_b, gt_a, w_o[:d_out.shape[2]], w_o[d_out.shape[2]:],
                  ln_attn_g, ln_attn_b, tm)

    wu = w_up[0].astype(BF16).reshape(D, 2, nf, tf).transpose(1, 2, 0, 3)
    cw = conv_w[0].reshape(CONV_W, 2, nf, tf).transpose(1, 2, 0, 3)
    cb = conv_b[0].reshape(2, nf, 1, tf)
    wd = w_down[0].astype(BF16).reshape(nf, tf, D)
    return _ffn(h2, sc_f, sh_f, gt_f, wu, cw, cb, wd, ln_ffn_g, ln_ffn_b, tm)
```

```python
import functools
import math

import numpy as np
import jax
import jax.numpy as jnp
from jax import lax
from jax.experimental import pallas as pl
from jax.experimental.pallas import tpu as pltpu

F32 = jnp.float32
BF16 = jnp.bfloat16

D_MODEL = 1024
DIFF_DH = 64
DIFF_HEADS = 4
HEAD_W = 2 * DIFF_DH
GLA_HEADS = 4
GLA_DK = 64
GLA_DV = 128
GLA_KW = GLA_HEADS * GLA_DK
GLA_VW = GLA_HEADS * GLA_DV
GLA_RANK = 16
GLA_TAU = 16.0
GLA_CHUNK = 64
GLA_LEVELS = 6
D_FF = 2816
CONV_W = 3
ROPE_THETA = 10000.0
LN_EPS = 1e-5
DEPTH = 1
DN_ALPHA = (2.0 * DEPTH) ** 0.25
LAMBDA_INIT = 0.8 - 0.6 * math.exp(-0.3 * 0)

N_MAIN = 3072
N_PROJ = N_MAIN + 128
LANES = 128
NEG_BIG = -1e30

VMEM_LIMIT = 56 * 1024 * 1024


def _layer_norm(x, g, b):
    mu = jnp.mean(x, axis=-1, keepdims=True)
    xc = x - mu
    var = jnp.mean(xc * xc, axis=-1, keepdims=True)
    return xc * lax.rsqrt(var + LN_EPS) * g + b


def _silu(x):
    return x * jax.nn.sigmoid(x)


def _ada_kernel(c_ref, w_ref, b_ref, o_ref):
    ca = _silu(c_ref[...])
    o_ref[...] = jnp.dot(ca.astype(BF16), w_ref[...].astype(BF16),
                         preferred_element_type=F32) + b_ref[...]


def _ada(c_pad, w_ada, b_ada):
    rows, d = c_pad.shape
    n = w_ada.shape[1]
    tn = 1024
    return pl.pallas_call(
        _ada_kernel,
        grid=(n // tn,),
        in_specs=[pl.BlockSpec((rows, d), lambda j: (0, 0)),
                  pl.BlockSpec((d, tn), lambda j: (0, j)),
                  pl.BlockSpec((1, tn), lambda j: (0, j))],
        out_specs=pl.BlockSpec((rows, tn), lambda j: (0, j)),
        out_shape=jax.ShapeDtypeStruct((rows, n), F32),
        name="ada",
    )(c_pad, w_ada, b_ada)


def _proj_kernel(x_ref, pos_ref, inv_ref, lg_ref, lb_ref, sc_ref, sh_ref, w_ref, wg_ref, bg_ref,
                 qk_ref, dv_ref, gq_ref, gk_ref, gv_ref, gr_ref, gl_ref):
    x = x_ref[0]
    h = _layer_norm(x, lg_ref[...], lb_ref[...])
    u = (h * (1.0 + sc_ref[0]) + sh_ref[0]).astype(BF16)
    proj = jnp.dot(u, w_ref[...], preferred_element_type=F32)

    pos = pos_ref[0].astype(F32)
    ang = pos * inv_ref[...]
    cos = jnp.cos(ang)
    sin = jnp.sin(ang)
    lane = lax.broadcasted_iota(jnp.int32, (1, LANES), 1)
    first_half = (lane & 32) == 0
    sin_signed = jnp.where(first_half, -sin, sin)
    scale = 1.0 / math.sqrt(DIFF_DH)
    for j in range(8):
        xg = proj[:, j * LANES:(j + 1) * LANES]
        partner = jnp.where(first_half, pltpu.roll(xg, LANES - 32, axis=1), pltpu.roll(xg, 32, axis=1))
        r = xg * cos + partner * sin_signed
        if j < 4:
            r = r * scale
        qk_ref[0, :, j * LANES:(j + 1) * LANES] = r.astype(BF16)

    dv_ref[0] = proj[:, 1024:1536].astype(BF16)
    gq_ref[0] = (proj[:, 1536:1792] * (GLA_DK ** -0.5)).astype(BF16)
    gk_ref[0] = proj[:, 1792:2048].astype(BF16)
    gv_ref[0] = proj[:, 2048:2560].astype(BF16)
    gr_ref[0] = proj[:, 2560:3072].astype(BF16)
    gg = proj[:, N_MAIN:N_PROJ].astype(BF16)
    z = jnp.dot(gg, wg_ref[...], preferred_element_type=F32) + bg_ref[...]
    log_sig = jnp.minimum(z, 0.0) - jnp.log1p(jnp.exp(-jnp.abs(z)))
    gl_ref[0] = log_sig * (1.0 / GLA_TAU)


def _proj(x, pos3, inv, ln_g, ln_b, sc, sh, w, wg, bg, tm):
    B, S, D = x.shape
    grid = (B, S // tm)
    row = lambda b, s: (b, s, 0)
    const2 = lambda b, s: (0, 0)
    per_b = lambda b, s: (b, 0, 0)
    widths = (1024, 512, GLA_KW, GLA_KW, GLA_VW, GLA_VW)
    out_shape = [jax.ShapeDtypeStruct((B, S, n), BF16) for n in widths]
    out_shape.append(jax.ShapeDtypeStruct((B, S, GLA_KW), F32))
    out_specs = [pl.BlockSpec((1, tm, n), row) for n in widths]
    out_specs.append(pl.BlockSpec((1, tm, GLA_KW), row))
    return pl.pallas_call(
        _proj_kernel,
        grid=grid,
        in_specs=[pl.BlockSpec((1, tm, D), row),
                  pl.BlockSpec((1, tm, 1), row),
                  pl.BlockSpec((1, LANES), const2),
                  pl.BlockSpec((1, D), const2),
                  pl.BlockSpec((1, D), const2),
                  pl.BlockSpec((1, 1, D), per_b),
                  pl.BlockSpec((1, 1, D), per_b),
                  pl.BlockSpec((D, N_PROJ), const2),
                  pl.BlockSpec((LANES, GLA_KW), const2),
                  pl.BlockSpec((1, GLA_KW), const2)],
        out_specs=out_specs,
        out_shape=out_shape,
        compiler_params=pltpu.CompilerParams(
            dimension_semantics=("arbitrary", "arbitrary"), vmem_limit_bytes=VMEM_LIMIT),
        name="proj",
    )(x, pos3, inv, ln_g, ln_b, sc, sh, w, wg, bg)


def _diffattn_kernel(q_ref, k_ref, v_ref, lq1_ref, lk1_ref, lq2_ref, lk2_ref, nw_ref, o_ref,
                     m_ref, l_ref, acc_ref, *, tq):
    i = pl.program_id(2)
    lane = lax.broadcasted_iota(jnp.int32, (1, HEAD_W), 1)
    q = q_ref[0]
    zero = jnp.zeros_like(q)
    qz = jnp.concatenate([jnp.where(lane < DIFF_DH, q, zero), jnp.where(lane >= DIFF_DH, q, zero)], axis=0)

    m_ref[...] = jnp.full(m_ref.shape, NEG_BIG, F32)
    l_ref[...] = jnp.zeros(l_ref.shape, F32)
    acc_ref[...] = jnp.zeros(acc_ref.shape, F32)

    def step(j, masked):
        kb = k_ref[0, pl.ds(pl.multiple_of(j * tq, tq), tq), :]
        vb = v_ref[0, pl.ds(pl.multiple_of(j * tq, tq), tq), :]
        s = lax.dot_general(qz, kb, (((1,), (1,)), ((), ())), preferred_element_type=F32)
        if masked:
            r = lax.broadcasted_iota(jnp.int32, (2 * tq, tq), 0)
            c = lax.broadcasted_iota(jnp.int32, (2 * tq, tq), 1)
            r = jnp.where(r >= tq, r - tq, r)
            s = jnp.where(c <= r, s, NEG_BIG)
        s_fold = s[:, 0:LANES]
        for t in range(1, tq // LANES):
            s_fold = jnp.maximum(s_fold, s[:, t * LANES:(t + 1) * LANES])
        m_prev = m_ref[...]
        m_new = jnp.maximum(m_prev, jnp.max(s_fold, axis=1, keepdims=True))
        alpha = jnp.exp(m_prev - m_new)
        p = jnp.exp(s - pltpu.repeat(m_new, tq // LANES, axis=1))
        p_fold = p[:, 0:LANES]
        for t in range(1, tq // LANES):
            p_fold = p_fold + p[:, t * LANES:(t + 1) * LANES]
        l_ref[...] = alpha * l_ref[...] + p_fold
        acc_ref[...] = alpha * acc_ref[...] + jnp.dot(p.astype(BF16), vb, preferred_element_type=F32)
        m_ref[...] = m_new

    def body(j, carry):
        step(j, False)
        return carry

    lax.fori_loop(0, i, body, 0)
    step(i, True)

    l = jnp.sum(l_ref[...], axis=1, keepdims=True)
    o = acc_ref[...] / l
    lam = (jnp.exp(jnp.sum(lq1_ref[...] * lk1_ref[...], axis=1, keepdims=True))
           - jnp.exp(jnp.sum(lq2_ref[...] * lk2_ref[...], axis=1, keepdims=True)) + LAMBDA_INIT)
    d = o[0:tq] - lam * o[tq:2 * tq]
    ms = jnp.mean(d * d, axis=1, keepdims=True)
    d = d * lax.rsqrt(ms + LN_EPS) * nw_ref[...] * (1.0 - LAMBDA_INIT)
    o_ref[0] = d.astype(BF16)


def _diffattn(qk, dv, lq1, lk1, lq2, lk2, nw, tq):
    B, S, _ = qk.shape
    grid = (B, DIFF_HEADS, S // tq)
    const2 = lambda b, h, i: (0, 0)
    return pl.pallas_call(
        functools.partial(_diffattn_kernel, tq=tq),
        grid=grid,
        in_specs=[pl.BlockSpec((1, tq, HEAD_W), lambda b, h, i: (b, i, h)),
                  pl.BlockSpec((1, S, HEAD_W), lambda b, h, i: (b, 0, DIFF_HEADS + h)),
                  pl.BlockSpec((1, S, HEAD_W), lambda b, h, i: (b, 0, h)),
                  pl.BlockSpec((1, DIFF_DH), const2),
                  pl.BlockSpec((1, DIFF_DH), const2),
                  pl.BlockSpec((1, DIFF_DH), const2),
                  pl.BlockSpec((1, DIFF_DH), const2),
                  pl.BlockSpec((1, HEAD_W), const2)],
        out_specs=pl.BlockSpec((1, tq, HEAD_W), lambda b, h, i: (b, i, h)),
        out_shape=jax.ShapeDtypeStruct((B, S, DIFF_HEADS * HEAD_W), BF16),
        scratch_shapes=[pltpu.VMEM((2 * tq, LANES), F32),
                        pltpu.VMEM((2 * tq, LANES), F32),
                        pltpu.VMEM((2 * tq, HEAD_W), F32)],
        compiler_params=pltpu.CompilerParams(
            dimension_semantics=("arbitrary", "arbitrary", "arbitrary"), vmem_limit_bytes=VMEM_LIMIT),
        name="diffattn",
    )(qk, qk, dv, lq1, lk1, lq2, lk2, nw)


def _gla_tables():
    C = GLA_CHUNK
    t = np.arange(C)
    rng = np.zeros((2 + GLA_LEVELS, C, C), np.float32)
    rng[0] = (t[None, :] <= t[:, None])
    rng[1] = (t[None, :] > t[:, None])
    lvl_mask = np.zeros((GLA_LEVELS + 1, C, C), np.float32)
    for l in range(GLA_LEVELS):
        s = C >> (l + 1)
        blk = t // (2 * s)
        mid = blk * 2 * s + s
        upper = (t % (2 * s)) >= s
        for i in range(C):
            if upper[i]:
                rng[2 + l, i, mid[i]:i + 1] = 1.0
            else:
                rng[2 + l, i, i + 1:mid[i]] = 1.0
        lvl_mask[l] = (blk[:, None] == blk[None, :]) & upper[:, None] & (~upper[None, :])
    lvl_mask[GLA_LEVELS] = np.eye(C)
    rng = rng.reshape((2 + GLA_LEVELS) * C, C)
    rng3 = np.concatenate([rng, rng, rng], axis=1)
    lvl_mask = np.tile(lvl_mask, (1, 1, GLA_HEADS))
    hk = np.kron(np.eye(GLA_HEADS), np.ones((C, GLA_DK)))
    hv = np.kron(np.eye(GLA_HEADS), np.ones((C, GLA_DV)))
    hs = np.kron(np.eye(GLA_HEADS), np.ones((GLA_DV, GLA_DK)))
    return (jnp.asarray(rng3, BF16), jnp.asarray(lvl_mask, F32), jnp.asarray(hk, BF16),
            jnp.asarray(hv, BF16), jnp.asarray(hs, F32))


def _gla_kernel(q_ref, k_ref, v_ref, g_ref, r_ref, rng_ref, lm_ref, hk_ref, hv_ref, hs_ref, nw_ref,
                o_ref, st_ref, *, ts):
    C = GLA_CHUNK

    @pl.when(pl.program_id(1) == 0)
    def _():
        st_ref[...] = jnp.zeros(st_ref.shape, F32)

    for c in range(ts // C):
        rows = pl.ds(c * C, C)
        q = q_ref[0, rows, :].astype(F32)
        k = k_ref[0, rows, :].astype(F32)
        v = v_ref[0, rows, :]
        g = g_ref[0, rows, :]
        g_hi = g.astype(BF16)
        rem = g - g_hi.astype(F32)
        g_mid = rem.astype(BF16)
        g_lo = (rem - g_mid.astype(F32)).astype(BF16)
        g3 = jnp.concatenate([g_hi, g_mid, g_lo], axis=0)
        f = jnp.exp(jnp.dot(rng_ref[...], g3, preferred_element_type=F32))

        q_in = (q * f[0:C]).astype(BF16)
        k_out = (k * f[C:2 * C]).astype(BF16)
        decay = f[C - 1:C]

        attn = jnp.zeros((C, GLA_HEADS * C), F32)
        for l in range(GLA_LEVELS + 1):
            if l < GLA_LEVELS:
                fl = f[(2 + l) * C:(3 + l) * C]
                ql = (q * fl).astype(BF16)
                kl = (k * fl).astype(BF16)
            else:
                ql = q.astype(BF16)
                kl = k.astype(BF16)
            k_bd = jnp.concatenate([kl] * GLA_HEADS, axis=0) * hk_ref[...]
            a = lax.dot_general(ql, k_bd, (((1,), (1,)), ((), ())), preferred_element_type=F32)
            attn = attn + a * lm_ref[l]

        v_bd = jnp.concatenate([v] * GLA_HEADS, axis=0) * hv_ref[...]
        st = st_ref[...]
        o = jnp.dot(attn.astype(BF16), v_bd, preferred_element_type=F32)
        o = o + lax.dot_general(q_in, st.astype(BF16), (((1,), (1,)), ((), ())), preferred_element_type=F32)
        upd = lax.dot_general(v, k_out, (((0,), (0,)), ((), ())), preferred_element_type=F32)
        st_ref[...] = st * decay + upd * hs_ref[...]

        parts = []
        for h in range(GLA_HEADS):
            oh = o[:, h * GLA_DV:(h + 1) * GLA_DV]
            ms = jnp.mean(oh * oh, axis=1, keepdims=True)
            parts.append(oh * lax.rsqrt(ms + LN_EPS) * nw_ref[...])
        on = jnp.concatenate(parts, axis=1)
        o_ref[0, rows, :] = (on * _silu(r_ref[0, rows, :].astype(F32))).astype(BF16)


def _gla(gq, gk, gv, gl, gr, nw, ts):
    B, S, _ = gq.shape
    rng3, lvl_mask, hk, hv, hs = _gla_tables()
    row = lambda b, s: (b, s, 0)
    const2 = lambda b, s: (0, 0)
    const3 = lambda b, s: (0, 0, 0)
    return pl.pallas_call(
        functools.partial(_gla_kernel, ts=ts),
        grid=(B, S // ts),
        in_specs=[pl.BlockSpec((1, ts, GLA_KW), row),
                  pl.BlockSpec((1, ts, GLA_KW), row),
                  pl.BlockSpec((1, ts, GLA_VW), row),
                  pl.BlockSpec((1, ts, GLA_KW), row),
                  pl.BlockSpec((1, ts, GLA_VW), row),
                  pl.BlockSpec(rng3.shape, const2),
                  pl.BlockSpec(lvl_mask.shape, const3),
                  pl.BlockSpec(hk.shape, const2),
                  pl.BlockSpec(hv.shape, const2),
                  pl.BlockSpec(hs.shape, const2),
                  pl.BlockSpec((1, GLA_DV), const2)],
        out_specs=pl.BlockSpec((1, ts, GLA_VW), row),
        out_shape=jax.ShapeDtypeStruct((B, S, GLA_VW), BF16),
        scratch_shapes=[pltpu.VMEM((GLA_VW, GLA_KW), F32)],
        compiler_params=pltpu.CompilerParams(
            dimension_semantics=("arbitrary", "arbitrary"), vmem_limit_bytes=VMEM_LIMIT),
        name="gla",
    )(gq, gk, gv, gl, gr, rng3, lvl_mask, hk, hv, hs, nw)


def _outproj_kernel(x_ref, d_ref, g_ref, lg_ref, lb_ref, gt_ref, wd_ref, wg_ref, ag_ref, ab_ref, o_ref):
    h = _layer_norm(x_ref[0], lg_ref[...], lb_ref[...])
    mix = (jnp.dot(d_ref[0], wd_ref[...], preferred_element_type=F32)
           + jnp.dot(g_ref[0], wg_ref[...], preferred_element_type=F32))
    y = DN_ALPHA * h + (1.0 + gt_ref[0]) * mix
    o_ref[0] = _layer_norm(y, ag_ref[...], ab_ref[...])


def _outproj(x, d_out, g_out, ln_g, ln_b, gt, w_d, w_g, ag, ab, tm):
    B, S, D = x.shape
    row = lambda b, s: (b, s, 0)
    const2 = lambda b, s: (0, 0)
    per_b = lambda b, s: (b, 0, 0)
    return pl.pallas_call(
        _outproj_kernel,
        grid=(B, S // tm),
        in_specs=[pl.BlockSpec((1, tm, D), row),
                  pl.BlockSpec((1, tm, d_out.shape[2]), row),
                  pl.BlockSpec((1, tm, g_out.shape[2]), row),
                  pl.BlockSpec((1, D), const2),
                  pl.BlockSpec((1, D), const2),
                  pl.BlockSpec((1, 1, D), per_b),
                  pl.BlockSpec(w_d.shape, const2),
                  pl.BlockSpec(w_g.shape, const2),
                  pl.BlockSpec((1, D), const2),
                  pl.BlockSpec((1, D), const2)],
        out_specs=pl.BlockSpec((1, tm, D), row),
        out_shape=jax.ShapeDtypeStruct((B, S, D), F32),
        compiler_params=pltpu.CompilerParams(
            dimension_semantics=("arbitrary", "arbitrary"), vmem_limit_bytes=VMEM_LIMIT),
        name="outproj",
    )(x, d_out, g_out, ln_g, ln_b, gt, w_d, w_g, ag, ab)


def _ffn_kernel(h_ref, sc_ref, sh_ref, gt_ref, wu_ref, cw_ref, cb_ref, wd_ref, fg_ref, fb_ref, o_ref,
                carry_ref, ubuf0_ref, ubuf1_ref, act_ref, u_ref, *, tm, tf, nf):
    HALO = 8
    ubufs = (ubuf0_ref, ubuf1_ref)

    @pl.when(pl.program_id(1) == 0)
    def _():
        carry_ref[...] = jnp.zeros(carry_ref.shape, F32)

    h = h_ref[0]
    u_ref[...] = (h * (1.0 + sc_ref[0]) + sh_ref[0]).astype(BF16)

    def up_proj(f, slot):
        for half in range(2):
            up = jnp.dot(u_ref[...], wu_ref[half, f], preferred_element_type=F32)
            ubufs[slot][half, 0:HALO, :] = carry_ref[half, f]
            ubufs[slot][half, HALO:HALO + tm, :] = up
            carry_ref[half, f] = up[tm - HALO:tm, :]

    def gate(f, slot):
        def conv(half):
            cw = cw_ref[half, f]
            buf = ubufs[slot]
            return (cb_ref[half, f]
                    + cw[0:1] * buf[half, HALO - 2:HALO - 2 + tm, :]
                    + cw[1:2] * buf[half, HALO - 1:HALO - 1 + tm, :]
                    + cw[2:3] * buf[half, HALO:HALO + tm, :])
        act_ref[f] = (_silu(conv(0)) * conv(1)).astype(BF16)

    up_proj(0, 0)
    for f in range(nf):
        if f + 1 < nf:
            up_proj(f + 1, (f + 1) % 2)
        gate(f, f % 2)

    hm = tm // 2
    for r in range(2):
        rows = slice(r * hm, (r + 1) * hm)
        ff = jnp.dot(act_ref[0, rows, :], wd_ref[0], preferred_element_type=F32)
        for f in range(1, nf):
            ff = ff + jnp.dot(act_ref[f, rows, :], wd_ref[f], preferred_element_type=F32)
        y = DN_ALPHA * h[rows] + (1.0 + gt_ref[0]) * ff
        o_ref[0, rows, :] = _layer_norm(y, fg_ref[...], fb_ref[...])


def _ffn(h2, sc, sh, gt, wu, cw, cb, wd, fg, fb, tm):
    B, S, D = h2.shape
    _, nf, _, tf = wu.shape
    assert nf % 2 == 1, "the chunk loop is unrolled by two after a one-chunk prologue"
    row = lambda b, s: (b, s, 0)
    const2 = lambda b, s: (0, 0)
    const3 = lambda b, s: (0, 0, 0)
    const4 = lambda b, s: (0, 0, 0, 0)
    per_b = lambda b, s: (b, 0, 0)
    return pl.pallas_call(
        functools.partial(_ffn_kernel, tm=tm, tf=tf, nf=nf),
        grid=(B, S // tm),
        in_specs=[pl.BlockSpec((1, tm, D), row),
                  pl.BlockSpec((1, 1, D), per_b),
                  pl.BlockSpec((1, 1, D), per_b),
                  pl.BlockSpec((1, 1, D), per_b),
                  pl.BlockSpec(wu.shape, const4),
                  pl.BlockSpec(cw.shape, const4),
                  pl.BlockSpec(cb.shape, const4),
                  pl.BlockSpec(wd.shape, const3),
                  pl.BlockSpec((1, D), const2),
                  pl.BlockSpec((1, D), const2)],
        out_specs=pl.BlockSpec((1, tm, D), row),
        out_shape=jax.ShapeDtypeStruct((B, S, D), F32),
        scratch_shapes=[pltpu.VMEM((2, nf, 8, tf), F32),
                        pltpu.VMEM((2, 8 + tm, tf), F32),
                        pltpu.VMEM((2, 8 + tm, tf), F32),
                        pltpu.VMEM((nf, tm, tf), BF16),
                        pltpu.VMEM((tm, D), BF16)],
        compiler_params=pltpu.CompilerParams(
            dimension_semantics=("arbitrary", "arbitrary"), vmem_limit_bytes=VMEM_LIMIT),
        name="ffn",
    )(h2, sc, sh, gt, wu, cw, cb, wd, fg, fb)


def kernel(x, c, positions, ln_in_g, ln_in_b, w_ada, b_ada, w_in, lambda_q1, lambda_k1, lambda_q2, lambda_k2, diff_norm_w, gla_w_gate_up, gla_b_gate, gla_norm_w, w_out, ln_attn_g, ln_attn_b, w_up, conv_w, conv_b, w_down, ln_ffn_g, ln_ffn_b):
    B, S, D = x.shape
    assert D == D_MODEL and w_ada.shape[0] == 1
    tm = min(512, S)
    tf = 256
    nf = D_FF // tf

    c_pad = jnp.pad(c, ((0, 8 - B % 8 if B % 8 else 0), (0, 0)))
    ada = _ada(c_pad, w_ada[0], b_ada)[:B]
    sh_a, sc_a, gt_a, sh_f, sc_f, gt_f = [t[:, None, :] for t in jnp.split(ada, 6, axis=-1)]

    ln_g = ln_in_g[None, :]
    ln_b = ln_in_b[None, :]

    w_main = w_in[0, :, :N_MAIN]
    w_gg = jnp.pad(w_in[0, :, N_MAIN:], ((0, 0), (0, N_PROJ - w_in.shape[2])))
    w_proj = jnp.concatenate([w_main, w_gg], axis=1).astype(BF16)
    w_gate = jnp.pad(gla_w_gate_up[0], ((0, LANES - GLA_RANK), (0, 0))).astype(BF16)
    inv = ROPE_THETA ** (-jnp.arange(0, DIFF_DH, 2, dtype=F32) / DIFF_DH)
    inv = jnp.tile(inv, LANES // inv.shape[0])[None, :]
    qk, dv, gq, gk, gv, gr, gl = _proj(x, positions[:, :, None], inv, ln_g, ln_b, sc_a, sh_a,
                                       w_proj, w_gate, gla_b_gate, tm)

    d_out = _diffattn(qk, dv, lambda_q1, lambda_k1, lambda_q2, lambda_k2, diff_norm_w, min(512, S))
    g_out = _gla(gq, gk, gv, gl, gr, gla_norm_w, min(512, S))

    w_o = w_out[0].astype(BF16)
    h2 = _outproj(x, d_out, g_out, ln_g, ln_b, gt_a, w_o[:d_out.shape[2]], w_o[d_out.shape[2]:],
                  ln_attn_g, ln_attn_b, tm)

    wu = w_up[0].astype(BF16).reshape(D, 2, nf, tf).transpose(1, 2, 0, 3)
    cw = conv_w[0].reshape(CONV_W, 2, nf, tf).transpose(1, 2, 0, 3)
    cb = conv_b[0].reshape(2, nf, 1, tf)
    wd = w_down[0].astype(BF16).reshape(nf, tf, D)
    return _ffn(h2, sc_f, sh_f, gt_f, wu, cw, cb, wd, ln_ffn_g, ln_ffn_b, tm)
```

```python
import functools
import math

import numpy as np
import jax
import jax.numpy as jnp
from jax import lax
from jax.experimental import pallas as pl
from jax.experimental.pallas import tpu as pltpu

F32 = jnp.float32
BF16 = jnp.bfloat16

D_MODEL = 1024
DIFF_DH = 64
DIFF_HEADS = 4
HEAD_W = 2 * DIFF_DH
GLA_HEADS = 4
GLA_DK = 64
GLA_DV = 128
GLA_KW = GLA_HEADS * GLA_DK
GLA_VW = GLA_HEADS * GLA_DV
GLA_RANK = 16
GLA_TAU = 16.0
GLA_CHUNK = 64
GLA_LEVELS = 6
D_FF = 2816
CONV_W = 3
ROPE_THETA = 10000.0
LN_EPS = 1e-5
DEPTH = 1
DN_ALPHA = (2.0 * DEPTH) ** 0.25
LAMBDA_INIT = 0.8 - 0.6 * math.exp(-0.3 * 0)

N_MAIN = 3072
N_PROJ = N_MAIN + 128
LANES = 128
NEG_BIG = -1e30
LOG2_E = math.log2(math.e)

VMEM_LIMIT = 56 * 1024 * 1024


def _layer_norm(x, g, b):
    mu = jnp.mean(x, axis=-1, keepdims=True)
    xc = x - mu
    var = jnp.mean(xc * xc, axis=-1, keepdims=True)
    return xc * lax.rsqrt(var + LN_EPS) * g + b


def _silu(x):
    return x * jax.nn.sigmoid(x)


def _ada_kernel(c_ref, w_ref, b_ref, o_ref):
    ca = _silu(c_ref[...])
    o_ref[...] = jnp.dot(ca.astype(BF16), w_ref[...].astype(BF16),
                         preferred_element_type=F32) + b_ref[...]


def _ada(c_pad, w_ada, b_ada):
    rows, d = c_pad.shape
    n = w_ada.shape[1]
    tn = 1024
    return pl.pallas_call(
        _ada_kernel,
        grid=(n // tn,),
        in_specs=[pl.BlockSpec((rows, d), lambda j: (0, 0)),
                  pl.BlockSpec((d, tn), lambda j: (0, j)),
                  pl.BlockSpec((1, tn), lambda j: (0, j))],
        out_specs=pl.BlockSpec((rows, tn), lambda j: (0, j)),
        out_shape=jax.ShapeDtypeStruct((rows, n), F32),
        name="ada",
    )(c_pad, w_ada, b_ada)


def _proj_kernel(x_ref, pos_ref, inv_ref, lg_ref, lb_ref, sc_ref, sh_ref, w_ref, wg_ref, bg_ref,
                 qk_ref, dv_ref, gq_ref, gk_ref, gv_ref, gr_ref, gl_ref):
    x = x_ref[0]
    h = _layer_norm(x, lg_ref[...], lb_ref[...])
    u = (h * (1.0 + sc_ref[0]) + sh_ref[0]).astype(BF16)
    proj = jnp.dot(u, w_ref[...], preferred_element_type=F32)

    pos = pos_ref[0].astype(F32)
    ang = pos * inv_ref[...]
    cos = jnp.cos(ang)
    sin = jnp.sin(ang)
    lane = lax.broadcasted_iota(jnp.int32, (1, LANES), 1)
    first_half = (lane & 32) == 0
    sin_signed = jnp.where(first_half, -sin, sin)
    scale = LOG2_E / math.sqrt(DIFF_DH)
    for j in range(8):
        xg = proj[:, j * LANES:(j + 1) * LANES]
        partner = jnp.where(first_half, pltpu.roll(xg, LANES - 32, axis=1), pltpu.roll(xg, 32, axis=1))
        r = xg * cos + partner * sin_signed
        if j < 4:
            r = r * scale
        qk_ref[0, :, j * LANES:(j + 1) * LANES] = r.astype(BF16)

    dv_ref[0] = proj[:, 1024:1536].astype(BF16)
    gq_ref[0] = (proj[:, 1536:1792] * (GLA_DK ** -0.5)).astype(BF16)
    gk_ref[0] = proj[:, 1792:2048].astype(BF16)
    gv_ref[0] = proj[:, 2048:2560].astype(BF16)
    gr_ref[0] = proj[:, 2560:3072].astype(BF16)
    gg = proj[:, N_MAIN:N_PROJ].astype(BF16)
    z = jnp.dot(gg, wg_ref[...], preferred_element_type=F32) + bg_ref[...]
    log_sig = jnp.minimum(z, 0.0) - jnp.log1p(jnp.exp(-jnp.abs(z)))
    gl_ref[0] = log_sig * (1.0 / GLA_TAU)


def _proj(x, pos3, inv, ln_g, ln_b, sc, sh, w, wg, bg, tm):
    B, S, D = x.shape
    grid = (B, S // tm)
    row = lambda b, s: (b, s, 0)
    const2 = lambda b, s: (0, 0)
    per_b = lambda b, s: (b, 0, 0)
    widths = (1024, 512, GLA_KW, GLA_KW, GLA_VW, GLA_VW)
    out_shape = [jax.ShapeDtypeStruct((B, S, n), BF16) for n in widths]
    out_shape.append(jax.ShapeDtypeStruct((B, S, GLA_KW), F32))
    out_specs = [pl.BlockSpec((1, tm, n), row) for n in widths]
    out_specs.append(pl.BlockSpec((1, tm, GLA_KW), row))
    return pl.pallas_call(
        _proj_kernel,
        grid=grid,
        in_specs=[pl.BlockSpec((1, tm, D), row),
                  pl.BlockSpec((1, tm, 1), row),
                  pl.BlockSpec((1, LANES), const2),
                  pl.BlockSpec((1, D), const2),
                  pl.BlockSpec((1, D), const2),
                  pl.BlockSpec((1, 1, D), per_b),
                  pl.BlockSpec((1, 1, D), per_b),
                  pl.BlockSpec((D, N_PROJ), const2),
                  pl.BlockSpec((LANES, GLA_KW), const2),
                  pl.BlockSpec((1, GLA_KW), const2)],
        out_specs=out_specs,
        out_shape=out_shape,
        compiler_params=pltpu.CompilerParams(
            dimension_semantics=("arbitrary", "arbitrary"), vmem_limit_bytes=VMEM_LIMIT),
        name="proj",
    )(x, pos3, inv, ln_g, ln_b, sc, sh, w, wg, bg)


def _diffattn_kernel(qa_ref, qb_ref, k_ref, v_ref, lq1_ref, lk1_ref, lq2_ref, lk2_ref, nw_ref,
                     oa_ref, ob_ref, qz_ref, m_ref, acc_ref, *, tq, nt):
    i = pl.program_id(2)
    lane = lax.broadcasted_iota(jnp.int32, (1, HEAD_W), 1)
    for t, q_ref in enumerate((qa_ref, qb_ref)):
        q = q_ref[0]
        zero = jnp.zeros_like(q)
        qz_ref[t] = jnp.concatenate([jnp.where(lane < DIFF_DH, q, zero), jnp.where(lane >= DIFF_DH, q, zero)], axis=0)
    m_ref[...] = jnp.full(m_ref.shape, NEG_BIG, F32)
    acc_ref[...] = jnp.zeros(acc_ref.shape, F32)
    ones = jnp.ones((tq, LANES), BF16)

    def block_step(own, blk, masked):
        rows = pl.ds(pl.multiple_of(blk * tq, tq), tq)
        kb = k_ref[0, rows, :]
        vb = jnp.concatenate([v_ref[0, rows, :], ones], axis=1)
        s = lax.dot_general(qz_ref[own], kb, (((1,), (1,)), ((), ())), preferred_element_type=F32)
        if masked:
            r = lax.broadcasted_iota(jnp.int32, (2 * tq, tq), 0)
            c = lax.broadcasted_iota(jnp.int32, (2 * tq, tq), 1)
            r = jnp.where(r >= tq, r - tq, r)
            s = jnp.where(c <= r, s, NEG_BIG)
        s_fold = s[:, 0:LANES]
        for t in range(1, tq // LANES):
            s_fold = jnp.maximum(s_fold, s[:, t * LANES:(t + 1) * LANES])
        m_prev = m_ref[own]
        m_new = jnp.maximum(m_prev, jnp.max(s_fold, axis=1, keepdims=True))
        alpha = jnp.exp2(m_prev - m_new)
        p = jnp.exp2(s - jnp.concatenate([m_new] * (tq // LANES), axis=1))
        acc_ref[own] = (jnp.concatenate([alpha, alpha], axis=1) * acc_ref[own]
                        + jnp.dot(p.astype(BF16), vb, preferred_element_type=F32))
        m_ref[own] = m_new

    block_step(0, i, True)
    for t in range(1, nt):
        is_a = t <= i
        block_step(jnp.where(is_a, 0, 1), jnp.where(is_a, i - t, t - i - 1), False)
    block_step(1, nt - 1 - i, True)

    lam = (jnp.exp(jnp.sum(lq1_ref[...] * lk1_ref[...], axis=1, keepdims=True))
           - jnp.exp(jnp.sum(lq2_ref[...] * lk2_ref[...], axis=1, keepdims=True)) + LAMBDA_INIT)
    for t, o_ref in enumerate((oa_ref, ob_ref)):
        acc = acc_ref[t]
        o = acc[:, 0:HEAD_W] / acc[:, HEAD_W:2 * HEAD_W]
        d = o[0:tq] - lam * o[tq:2 * tq]
        ms = jnp.mean(d * d, axis=1, keepdims=True)
        o_ref[0] = (d * lax.rsqrt(ms + LN_EPS) * nw_ref[...] * (1.0 - LAMBDA_INIT)).astype(BF16)


def _diffattn(qk, dv, lq1, lk1, lq2, lk2, nw, tq):
    B, S, _ = qk.shape
    nt = S // tq
    assert nt % 2 == 0, "query tiles are processed in pairs (i, nt-1-i)"
    half = nt // 2
    const2 = lambda b, h, i: (0, 0)
    out = jax.ShapeDtypeStruct((B, S // 2, DIFF_HEADS * HEAD_W), BF16)
    lo, hi = pl.pallas_call(
        functools.partial(_diffattn_kernel, tq=tq, nt=nt),
        grid=(B, DIFF_HEADS, half),
        in_specs=[pl.BlockSpec((1, tq, HEAD_W), lambda b, h, i: (b, i, h)),
                  pl.BlockSpec((1, tq, HEAD_W), lambda b, h, i: (b, nt - 1 - i, h)),
                  pl.BlockSpec((1, S, HEAD_W), lambda b, h, i: (b, 0, DIFF_HEADS + h)),
                  pl.BlockSpec((1, S, HEAD_W), lambda b, h, i: (b, 0, h)),
                  pl.BlockSpec((1, DIFF_DH), const2),
                  pl.BlockSpec((1, DIFF_DH), const2),
                  pl.BlockSpec((1, DIFF_DH), const2),
                  pl.BlockSpec((1, DIFF_DH), const2),
                  pl.BlockSpec((1, HEAD_W), const2)],
        out_specs=[pl.BlockSpec((1, tq, HEAD_W), lambda b, h, i: (b, i, h)),
                   pl.BlockSpec((1, tq, HEAD_W), lambda b, h, i: (b, half - 1 - i, h))],
        out_shape=[out, out],
        scratch_shapes=[pltpu.VMEM((2, 2 * tq, HEAD_W), BF16),
                        pltpu.VMEM((2, 2 * tq, LANES), F32),
                        pltpu.VMEM((2, 2 * tq, 2 * HEAD_W), F32)],
        compiler_params=pltpu.CompilerParams(
            dimension_semantics=("arbitrary", "arbitrary", "arbitrary"), vmem_limit_bytes=VMEM_LIMIT),
        name="diffattn",
    )(qk, qk, qk, dv, lq1, lk1, lq2, lk2, nw)
    return jnp.concatenate([lo, hi], axis=1)


def _gla_tables():
    C = GLA_CHUNK
    t = np.arange(C)
    rng = np.zeros((2 + GLA_LEVELS, C, C), np.float32)
    rng[0] = (t[None, :] <= t[:, None])
    rng[1] = (t[None, :] > t[:, None])
    lvl_mask = np.zeros((GLA_LEVELS + 1, C, C), np.float32)
    for l in range(GLA_LEVELS):
        s = C >> (l + 1)
        blk = t // (2 * s)
        mid = blk * 2 * s + s
        upper = (t % (2 * s)) >= s
        for i in range(C):
            if upper[i]:
                rng[2 + l, i, mid[i]:i + 1] = 1.0
            else:
                rng[2 + l, i, i + 1:mid[i]] = 1.0
        lvl_mask[l] = (blk[:, None] == blk[None, :]) & upper[:, None] & (~upper[None, :])
    lvl_mask[GLA_LEVELS] = np.eye(C)
    rng = rng.reshape((2 + GLA_LEVELS) * C, C)
    rng3 = np.concatenate([rng, rng, rng], axis=1)
    lvl_mask = np.tile(lvl_mask, (1, 1, GLA_HEADS))
    hk = np.kron(np.eye(GLA_HEADS), np.ones((C, GLA_DK)))
    hv = np.kron(np.eye(GLA_HEADS), np.ones((C, GLA_DV)))
    hs = np.kron(np.eye(GLA_HEADS), np.ones((GLA_DV, GLA_DK)))
    return (jnp.asarray(rng3, BF16), jnp.asarray(lvl_mask, F32), jnp.asarray(hk, BF16),
            jnp.asarray(hv, BF16), jnp.asarray(hs, F32))


def _gla_kernel(q_ref, k_ref, v_ref, g_ref, r_ref, rng_ref, lm_ref, hk_ref, hv_ref, hs_ref, nw_ref,
                o_ref, st_ref, *, ts):
    C = GLA_CHUNK

    @pl.when(pl.program_id(1) == 0)
    def _():
        st_ref[...] = jnp.zeros(st_ref.shape, F32)

    for c in range(ts // C):
        rows = pl.ds(c * C, C)
        q = q_ref[0, rows, :].astype(F32)
        k = k_ref[0, rows, :].astype(F32)
        v = v_ref[0, rows, :]
        g = g_ref[0, rows, :]
        g_hi = g.astype(BF16)
        rem = g - g_hi.astype(F32)
        g_mid = rem.astype(BF16)
        g_lo = (rem - g_mid.astype(F32)).astype(BF16)
        g3 = jnp.concatenate([g_hi, g_mid, g_lo], axis=0)
        f = jnp.exp(jnp.dot(rng_ref[...], g3, preferred_element_type=F32))

        q_in = (q * f[0:C]).astype(BF16)
        k_out = (k * f[C:2 * C]).astype(BF16)
        decay = f[C - 1:C]

        attn = jnp.zeros((C, GLA_HEADS * C), F32)
        for l in range(GLA_LEVELS + 1):
            if l < GLA_LEVELS:
                fl = f[(2 + l) * C:(3 + l) * C]
                ql = (q * fl).astype(BF16)
                kl = (k * fl).astype(BF16)
            else:
                ql = q.astype(BF16)
                kl = k.astype(BF16)
            k_bd = jnp.concatenate([kl] * GLA_HEADS, axis=0) * hk_ref[...]
            a = lax.dot_general(ql, k_bd, (((1,), (1,)), ((), ())), preferred_element_type=F32)
            attn = attn + a * lm_ref[l]

        v_bd = jnp.concatenate([v] * GLA_HEADS, axis=0) * hv_ref[...]
        st = st_ref[...]
        o = jnp.dot(attn.astype(BF16), v_bd, preferred_element_type=F32)
        o = o + lax.dot_general(q_in, st.astype(BF16), (((1,), (1,)), ((), ())), preferred_element_type=F32)
        upd = lax.dot_general(v, k_out, (((0,), (0,)), ((), ())), preferred_element_type=F32)
        st_ref[...] = st * decay + upd * hs_ref[...]

        parts = []
        for h in range(GLA_HEADS):
            oh = o[:, h * GLA_DV:(h + 1) * GLA_DV]
            ms = jnp.mean(oh * oh, axis=1, keepdims=True)
            parts.append(oh * lax.rsqrt(ms + LN_EPS) * nw_ref[...])
        on = jnp.concatenate(parts, axis=1)
        o_ref[0, rows, :] = (on * _silu(r_ref[0, rows, :].astype(F32))).astype(BF16)


def _gla(gq, gk, gv, gl, gr, nw, ts):
    B, S, _ = gq.shape
    rng3, lvl_mask, hk, hv, hs = _gla_tables()
    row = lambda b, s: (b, s, 0)
    const2 = lambda b, s: (0, 0)
    const3 = lambda b, s: (0, 0, 0)
    return pl.pallas_call(
        functools.partial(_gla_kernel, ts=ts),
        grid=(B, S // ts),
        in_specs=[pl.BlockSpec((1, ts, GLA_KW), row),
                  pl.BlockSpec((1, ts, GLA_KW), row),
                  pl.BlockSpec((1, ts, GLA_VW), row),
                  pl.BlockSpec((1, ts, GLA_KW), row),
                  pl.BlockSpec((1, ts, GLA_VW), row),
                  pl.BlockSpec(rng3.shape, const2),
                  pl.BlockSpec(lvl_mask.shape, const3),
                  pl.BlockSpec(hk.shape, const2),
                  pl.BlockSpec(hv.shape, const2),
                  pl.BlockSpec(hs.shape, const2),
                  pl.BlockSpec((1, GLA_DV), const2)],
        out_specs=pl.BlockSpec((1, ts, GLA_VW), row),
        out_shape=jax.ShapeDtypeStruct((B, S, GLA_VW), BF16),
        scratch_shapes=[pltpu.VMEM((GLA_VW, GLA_KW), F32)],
        compiler_params=pltpu.CompilerParams(
            dimension_semantics=("arbitrary", "arbitrary"), vmem_limit_bytes=VMEM_LIMIT),
        name="gla",
    )(gq, gk, gv, gl, gr, rng3, lvl_mask, hk, hv, hs, nw)


def _outproj_kernel(x_ref, d_ref, g_ref, lg_ref, lb_ref, gt_ref, wd_ref, wg_ref, ag_ref, ab_ref, o_ref):
    h = _layer_norm(x_ref[0], lg_ref[...], lb_ref[...])
    mix = (jnp.dot(d_ref[0], wd_ref[...], preferred_element_type=F32)
           + jnp.dot(g_ref[0], wg_ref[...], preferred_element_type=F32))
    y = DN_ALPHA * h + (1.0 + gt_ref[0]) * mix
    o_ref[0] = _layer_norm(y, ag_ref[...], ab_ref[...])


def _outproj(x, d_out, g_out, ln_g, ln_b, gt, w_d, w_g, ag, ab, tm):
    B, S, D = x.shape
    row = lambda b, s: (b, s, 0)
    const2 = lambda b, s: (0, 0)
    per_b = lambda b, s: (b, 0, 0)
    return pl.pallas_call(
        _outproj_kernel,
        grid=(B, S // tm),
        in_specs=[pl.BlockSpec((1, tm, D), row),
                  pl.BlockSpec((1, tm, d_out.shape[2]), row),
                  pl.BlockSpec((1, tm, g_out.shape[2]), row),
                  pl.BlockSpec((1, D), const2),
                  pl.BlockSpec((1, D), const2),
                  pl.BlockSpec((1, 1, D), per_b),
                  pl.BlockSpec(w_d.shape, const2),
                  pl.BlockSpec(w_g.shape, const2),
                  pl.BlockSpec((1, D), const2),
                  pl.BlockSpec((1, D), const2)],
        out_specs=pl.BlockSpec((1, tm, D), row),
        out_shape=jax.ShapeDtypeStruct((B, S, D), F32),
        compiler_params=pltpu.CompilerParams(
            dimension_semantics=("arbitrary", "arbitrary"), vmem_limit_bytes=VMEM_LIMIT),
        name="outproj",
    )(x, d_out, g_out, ln_g, ln_b, gt, w_d, w_g, ag, ab)


def _ffn_kernel(h_ref, sc_ref, sh_ref, gt_ref, wu_ref, cw_ref, cb_ref, wd_ref, fg_ref, fb_ref, o_ref,
                carry_ref, ubuf0_ref, ubuf1_ref, act_ref, u_ref, *, tm, tf, nf):
    HALO = 8
    ubufs = (ubuf0_ref, ubuf1_ref)

    @pl.when(pl.program_id(1) == 0)
    def _():
        carry_ref[...] = jnp.zeros(carry_ref.shape, F32)

    h = h_ref[0]
    u_ref[...] = (h * (1.0 + sc_ref[0]) + sh_ref[0]).astype(BF16)

    def up_proj(f, slot):
        for half in range(2):
            up = jnp.dot(u_ref[...], wu_ref[half, f], preferred_element_type=F32)
            ubufs[slot][half, 0:HALO, :] = carry_ref[half, f]
            ubufs[slot][half, HALO:HALO + tm, :] = up
            carry_ref[half, f] = up[tm - HALO:tm, :]

    def gate(f, slot):
        def conv(half):
            cw = cw_ref[half, f]
            buf = ubufs[slot]
            return (cb_ref[half, f]
                    + cw[0:1] * buf[half, HALO - 2:HALO - 2 + tm, :]
                    + cw[1:2] * buf[half, HALO - 1:HALO - 1 + tm, :]
                    + cw[2:3] * buf[half, HALO:HALO + tm, :])
        act_ref[f] = (_silu(conv(0)) * conv(1)).astype(BF16)

    up_proj(0, 0)
    for f in range(nf):
        if f + 1 < nf:
            up_proj(f + 1, (f + 1) % 2)
        gate(f, f % 2)

    hm = tm // 2
    for r in range(2):
        rows = slice(r * hm, (r + 1) * hm)
        ff = jnp.dot(act_ref[0, rows, :], wd_ref[0], preferred_element_type=F32)
        for f in range(1, nf):
            ff = ff + jnp.dot(act_ref[f, rows, :], wd_ref[f], preferred_element_type=F32)
        y = DN_ALPHA * h[rows] + (1.0 + gt_ref[0]) * ff
        o_ref[0, rows, :] = _layer_norm(y, fg_ref[...], fb_ref[...])


def _ffn(h2, sc, sh, gt, wu, cw, cb, wd, fg, fb, tm):
    B, S, D = h2.shape
    _, nf, _, tf = wu.shape
    assert nf % 2 == 1, "the chunk loop is unrolled by two after a one-chunk prologue"
    row = lambda b, s: (b, s, 0)
    const2 = lambda b, s: (0, 0)
    const3 = lambda b, s: (0, 0, 0)
    const4 = lambda b, s: (0, 0, 0, 0)
    per_b = lambda b, s: (b, 0, 0)
    return pl.pallas_call(
        functools.partial(_ffn_kernel, tm=tm, tf=tf, nf=nf),
        grid=(B, S // tm),
        in_specs=[pl.BlockSpec((1, tm, D), row),
                  pl.BlockSpec((1, 1, D), per_b),
                  pl.BlockSpec((1, 1, D), per_b),
                  pl.BlockSpec((1, 1, D), per_b),
                  pl.BlockSpec(wu.shape, const4),
                  pl.BlockSpec(cw.shape, const4),
                  pl.BlockSpec(cb.shape, const4),
                  pl.BlockSpec(wd.shape, const3),
                  pl.BlockSpec((1, D), const2),
                  pl.BlockSpec((1, D), const2)],
        out_specs=pl.BlockSpec((1, tm, D), row),
        out_shape=jax.ShapeDtypeStruct((B, S, D), F32),
        scratch_shapes=[pltpu.VMEM((2, nf, 8, tf), F32),
                        pltpu.VMEM((2, 8 + tm, tf), F32),
                        pltpu.VMEM((2, 8 + tm, tf), F32),
                        pltpu.VMEM((nf, tm, tf), BF16),
                        pltpu.VMEM((tm, D), BF16)],
        compiler_params=pltpu.CompilerParams(
            dimension_semantics=("arbitrary", "arbitrary"), vmem_limit_bytes=VMEM_LIMIT),
        name="ffn",
    )(h2, sc, sh, gt, wu, cw, cb, wd, fg, fb)


def kernel(x, c, positions, ln_in_g, ln_in_b, w_ada, b_ada, w_in, lambda_q1, lambda_k1, lambda_q2, lambda_k2, diff_norm_w, gla_w_gate_up, gla_b_gate, gla_norm_w, w_out, ln_attn_g, ln_attn_b, w_up, conv_w, conv_b, w_down, ln_ffn_g, ln_ffn_b):
    B, S, D = x.shape
    assert D == D_MODEL and w_ada.shape[0] == 1
    tm = min(512, S)
    tf = 256
    nf = D_FF // tf

    c_pad = jnp.pad(c, ((0, 8 - B % 8 if B % 8 else 0), (0, 0)))
    ada = _ada(c_pad, w_ada[0], b_ada)[:B]
    sh_a, sc_a, gt_a, sh_f, sc_f, gt_f = [t[:, None, :] for t in jnp.split(ada, 6, axis=-1)]

    ln_g = ln_in_g[None, :]
    ln_b = ln_in_b[None, :]

    w_main = w_in[0, :, :N_MAIN]
    w_gg = jnp.pad(w_in[0, :, N_MAIN:], ((0, 0), (0, N_PROJ - w_in.shape[2])))
    w_proj = jnp.concatenate([w_main, w_gg], axis=1).astype(BF16)
    w_gate = jnp.pad(gla_w_gate_up[0], ((0, LANES - GLA_RANK), (0, 0))).astype(BF16)
    inv = ROPE_THETA ** (-jnp.arange(0, DIFF_DH, 2, dtype=F32) / DIFF_DH)
    inv = jnp.tile(inv, LANES // inv.shape[0])[None, :]
    qk, dv, gq, gk, gv, gr, gl = _proj(x, positions[:, :, None], inv, ln_g, ln_b, sc_a, sh_a,
                                       w_proj, w_gate, gla_b_gate, tm)

    d_out = _diffattn(qk, dv, lambda_q1, lambda_k1, lambda_q2, lambda_k2, diff_norm_w, min(512, S))
    g_out = _gla(gq, gk, gv, gl, gr, gla_norm_w, min(512, S))

    w_o = w_out[0].astype(BF16)
    h2 = _outproj(x, d_out, g_out, ln_g, ln_b, gt_a, w_o[:d_out.shape[2]], w_o[d_out.shape[2]:],
                  ln_attn_g, ln_attn_b, tm)

    wu = w_up[0].astype(BF16).reshape(D, 2, nf, tf).transpose(1, 2, 0, 3)
    cw = conv_w[0].reshape(CONV_W, 2, nf, tf).transpose(1, 2, 0, 3)
    cb = conv_b[0].reshape(2, nf, 1, tf)
    wd = w_down[0].astype(BF16).reshape(nf, tf, D)
    return _ffn(h2, sc_f, sh_f, gt_f, wu, cw, cb, wd, ln_ffn_g, ln_ffn_b, tm)
```

```python
import functools
import math

import numpy as np
import jax
import jax.numpy as jnp
from jax import lax
from jax.experimental import pallas as pl
from jax.experimental.pallas import tpu as pltpu

F32 = jnp.float32
BF16 = jnp.bfloat16

D_MODEL = 1024
DIFF_DH = 64
DIFF_HEADS = 4
HEAD_W = 2 * DIFF_DH
GLA_HEADS = 4
GLA_DK = 64
GLA_DV = 128
GLA_KW = GLA_HEADS * GLA_DK
GLA_VW = GLA_HEADS * GLA_DV
GLA_RANK = 16
GLA_TAU = 16.0
GLA_CHUNK = 64
GLA_LEVELS = 6
D_FF = 2816
CONV_W = 3
ROPE_THETA = 10000.0
LN_EPS = 1e-5
DEPTH = 1
DN_ALPHA = (2.0 * DEPTH) ** 0.25
LAMBDA_INIT = 0.8 - 0.6 * math.exp(-0.3 * 0)

N_MAIN = 3072
N_PROJ = N_MAIN + 128
LANES = 128
NEG_BIG = -1e30
LOG2_E = math.log2(math.e)

VMEM_LIMIT = 56 * 1024 * 1024


def _layer_norm(x, g, b):
    mu = jnp.mean(x, axis=-1, keepdims=True)
    xc = x - mu
    var = jnp.mean(xc * xc, axis=-1, keepdims=True)
    return xc * lax.rsqrt(var + LN_EPS) * g + b


def _silu(x):
    return x * jax.nn.sigmoid(x)


def _ada_kernel(c_ref, w_ref, b_ref, o_ref):
    ca = _silu(c_ref[...])
    o_ref[...] = jnp.dot(ca.astype(BF16), w_ref[...].astype(BF16),
                         preferred_element_type=F32) + b_ref[...]


def _ada(c_pad, w_ada, b_ada):
    rows, d = c_pad.shape
    n = w_ada.shape[1]
    tn = 1024
    return pl.pallas_call(
        _ada_kernel,
        grid=(n // tn,),
        in_specs=[pl.BlockSpec((rows, d), lambda j: (0, 0)),
                  pl.BlockSpec((d, tn), lambda j: (0, j)),
                  pl.BlockSpec((1, tn), lambda j: (0, j))],
        out_specs=pl.BlockSpec((rows, tn), lambda j: (0, j)),
        out_shape=jax.ShapeDtypeStruct((rows, n), F32),
        name="ada",
    )(c_pad, w_ada, b_ada)


def _proj_kernel(x_ref, pos_ref, inv_ref, lg_ref, lb_ref, sc_ref, sh_ref, w_ref, wg_ref, bg_ref,
                 qk_ref, dv_ref, gq_ref, gk_ref, gv_ref, gr_ref, gl_ref, *, tm):
    ang_t = inv_ref[...] * pos_ref[0].astype(F32)
    reps = LANES // ang_t.shape[0]
    cos = jnp.concatenate([jnp.cos(ang_t)] * reps, axis=0).T
    sin = jnp.concatenate([jnp.sin(ang_t)] * reps, axis=0).T
    lane = lax.broadcasted_iota(jnp.int32, (1, LANES), 1)
    first_half = (lane & 32) == 0
    sin_signed = jnp.where(first_half, -sin, sin)
    scale = LOG2_E / math.sqrt(DIFF_DH)

    hm = tm // 2
    for part in range(2):
        rows = slice(part * hm, (part + 1) * hm)
        h = _layer_norm(x_ref[0, rows, :], lg_ref[...], lb_ref[...])
        u = (h * (1.0 + sc_ref[0]) + sh_ref[0]).astype(BF16)
        proj = jnp.dot(u, w_ref[...], preferred_element_type=F32)
        for j in range(8):
            xg = proj[:, j * LANES:(j + 1) * LANES]
            partner = jnp.where(first_half, pltpu.roll(xg, LANES - 32, axis=1), pltpu.roll(xg, 32, axis=1))
            r = xg * cos[rows] + partner * sin_signed[rows]
            if j < 4:
                r = r * scale
            qk_ref[0, rows, j * LANES:(j + 1) * LANES] = r.astype(BF16)
        dv_ref[0, rows, :] = proj[:, 1024:1536].astype(BF16)
        gq_ref[0, rows, :] = (proj[:, 1536:1792] * (GLA_DK ** -0.5)).astype(BF16)
        gk_ref[0, rows, :] = proj[:, 1792:2048].astype(BF16)
        gv_ref[0, rows, :] = proj[:, 2048:2560].astype(BF16)
        gr_ref[0, rows, :] = proj[:, 2560:3072].astype(BF16)
        gg = proj[:, N_MAIN:N_PROJ].astype(BF16)
        z = jnp.dot(gg, wg_ref[...], preferred_element_type=F32) + bg_ref[...]
        log_sig = jnp.minimum(z, 0.0) - jnp.log1p(jnp.exp(-jnp.abs(z)))
        gl_ref[0, rows, :] = log_sig * (1.0 / GLA_TAU)


def _proj(x, pos_row, inv_col, ln_g, ln_b, sc, sh, w, wg, bg, tm):
    B, S, D = x.shape
    grid = (B, S // tm)
    row = lambda b, s: (b, s, 0)
    const2 = lambda b, s: (0, 0)
    per_b = lambda b, s: (b, 0, 0)
    widths = (1024, 512, GLA_KW, GLA_KW, GLA_VW, GLA_VW)
    out_shape = [jax.ShapeDtypeStruct((B, S, n), BF16) for n in widths]
    out_shape.append(jax.ShapeDtypeStruct((B, S, GLA_KW), F32))
    out_specs = [pl.BlockSpec((1, tm, n), row) for n in widths]
    out_specs.append(pl.BlockSpec((1, tm, GLA_KW), row))
    return pl.pallas_call(
        functools.partial(_proj_kernel, tm=tm),
        grid=grid,
        in_specs=[pl.BlockSpec((1, tm, D), row),
                  pl.BlockSpec((1, 1, tm), lambda b, s: (b, 0, s)),
                  pl.BlockSpec(inv_col.shape, const2),
                  pl.BlockSpec((1, D), const2),
                  pl.BlockSpec((1, D), const2),
                  pl.BlockSpec((1, 1, D), per_b),
                  pl.BlockSpec((1, 1, D), per_b),
                  pl.BlockSpec((D, N_PROJ), const2),
                  pl.BlockSpec((LANES, GLA_KW), const2),
                  pl.BlockSpec((1, GLA_KW), const2)],
        out_specs=out_specs,
        out_shape=out_shape,
        compiler_params=pltpu.CompilerParams(
            dimension_semantics=("arbitrary", "arbitrary"), vmem_limit_bytes=VMEM_LIMIT),
        name="proj",
    )(x, pos_row, inv_col, ln_g, ln_b, sc, sh, w, wg, bg)


def _diffattn_kernel(qa_ref, qb_ref, k_ref, v_ref, lq1_ref, lk1_ref, lq2_ref, lk2_ref, nw_ref,
                     oa_ref, ob_ref, qz_ref, m_ref, acc_ref, *, tq, nt):
    i = pl.program_id(2)
    lane = lax.broadcasted_iota(jnp.int32, (1, HEAD_W), 1)
    for t, q_ref in enumerate((qa_ref, qb_ref)):
        q = q_ref[0]
        zero = jnp.zeros_like(q)
        qz_ref[t] = jnp.concatenate([jnp.where(lane < DIFF_DH, q, zero), jnp.where(lane >= DIFF_DH, q, zero)], axis=0)
    m_ref[...] = jnp.full(m_ref.shape, NEG_BIG, F32)
    acc_ref[...] = jnp.zeros(acc_ref.shape, F32)
    ones = jnp.ones((tq, LANES), BF16)

    def block_step(own, blk, masked):
        rows = pl.ds(pl.multiple_of(blk * tq, tq), tq)
        kb = k_ref[0, rows, :]
        vb = jnp.concatenate([v_ref[0, rows, :], ones], axis=1)
        s = lax.dot_general(qz_ref[own], kb, (((1,), (1,)), ((), ())), preferred_element_type=F32)
        if masked:
            r = lax.broadcasted_iota(jnp.int32, (2 * tq, tq), 0)
            c = lax.broadcasted_iota(jnp.int32, (2 * tq, tq), 1)
            r = jnp.where(r >= tq, r - tq, r)
            s = jnp.where(c <= r, s, NEG_BIG)
        s_fold = s[:, 0:LANES]
        for t in range(1, tq // LANES):
            s_fold = jnp.maximum(s_fold, s[:, t * LANES:(t + 1) * LANES])
        m_prev = m_ref[own]
        m_new = jnp.maximum(m_prev, jnp.max(s_fold, axis=1, keepdims=True))
        alpha = jnp.exp2(m_prev - m_new)
        p = jnp.exp2(s - jnp.concatenate([m_new] * (tq // LANES), axis=1))
        acc_ref[own] = (jnp.concatenate([alpha, alpha], axis=1) * acc_ref[own]
                        + jnp.dot(p.astype(BF16), vb, preferred_element_type=F32))
        m_ref[own] = m_new

    block_step(0, i, True)
    for t in range(1, nt):
        is_a = t <= i
        block_step(jnp.where(is_a, 0, 1), jnp.where(is_a, i - t, t - i - 1), False)
    block_step(1, nt - 1 - i, True)

    lam = (jnp.exp(jnp.sum(lq1_ref[...] * lk1_ref[...], axis=1, keepdims=True))
           - jnp.exp(jnp.sum(lq2_ref[...] * lk2_ref[...], axis=1, keepdims=True)) + LAMBDA_INIT)
    for t, o_ref in enumerate((oa_ref, ob_ref)):
        acc = acc_ref[t]
        o = acc[:, 0:HEAD_W] / acc[:, HEAD_W:2 * HEAD_W]
        d = o[0:tq] - lam * o[tq:2 * tq]
        ms = jnp.mean(d * d, axis=1, keepdims=True)
        o_ref[0] = (d * lax.rsqrt(ms + LN_EPS) * nw_ref[...] * (1.0 - LAMBDA_INIT)).astype(BF16)


def _diffattn(qk, dv, lq1, lk1, lq2, lk2, nw, tq):
    B, S, _ = qk.shape
    nt = S // tq
    assert nt % 2 == 0, "query tiles are processed in pairs (i, nt-1-i)"
    half = nt // 2
    const2 = lambda b, h, i: (0, 0)
    out = jax.ShapeDtypeStruct((B, S // 2, DIFF_HEADS * HEAD_W), BF16)
    lo, hi = pl.pallas_call(
        functools.partial(_diffattn_kernel, tq=tq, nt=nt),
        grid=(B, DIFF_HEADS, half),
        in_specs=[pl.BlockSpec((1, tq, HEAD_W), lambda b, h, i: (b, i, h)),
                  pl.BlockSpec((1, tq, HEAD_W), lambda b, h, i: (b, nt - 1 - i, h)),
                  pl.BlockSpec((1, S, HEAD_W), lambda b, h, i: (b, 0, DIFF_HEADS + h)),
                  pl.BlockSpec((1, S, HEAD_W), lambda b, h, i: (b, 0, h)),
                  pl.BlockSpec((1, DIFF_DH), const2),
                  pl.BlockSpec((1, DIFF_DH), const2),
                  pl.BlockSpec((1, DIFF_DH), const2),
                  pl.BlockSpec((1, DIFF_DH), const2),
                  pl.BlockSpec((1, HEAD_W), const2)],
        out_specs=[pl.BlockSpec((1, tq, HEAD_W), lambda b, h, i: (b, i, h)),
                   pl.BlockSpec((1, tq, HEAD_W), lambda b, h, i: (b, half - 1 - i, h))],
        out_shape=[out, out],
        scratch_shapes=[pltpu.VMEM((2, 2 * tq, HEAD_W), BF16),
                        pltpu.VMEM((2, 2 * tq, LANES), F32),
                        pltpu.VMEM((2, 2 * tq, 2 * HEAD_W), F32)],
        compiler_params=pltpu.CompilerParams(
            dimension_semantics=("arbitrary", "arbitrary", "arbitrary"), vmem_limit_bytes=VMEM_LIMIT),
        name="diffattn",
    )(qk, qk, qk, dv, lq1, lk1, lq2, lk2, nw)
    return lo, hi


def _gla_tables():
    C = GLA_CHUNK
    t = np.arange(C)
    rng = np.zeros((2 + GLA_LEVELS, C, C), np.float32)
    rng[0] = (t[None, :] <= t[:, None])
    rng[1] = (t[None, :] > t[:, None])
    lvl_mask = np.zeros((GLA_LEVELS + 1, C, C), np.float32)
    for l in range(GLA_LEVELS):
        s = C >> (l + 1)
        blk = t // (2 * s)
        mid = blk * 2 * s + s
        upper = (t % (2 * s)) >= s
        for i in range(C):
            if upper[i]:
                rng[2 + l, i, mid[i]:i + 1] = 1.0
            else:
                rng[2 + l, i, i + 1:mid[i]] = 1.0
        lvl_mask[l] = (blk[:, None] == blk[None, :]) & upper[:, None] & (~upper[None, :])
    lvl_mask[GLA_LEVELS] = np.eye(C)
    rng = rng.reshape((2 + GLA_LEVELS) * C, C)
    rng3 = np.concatenate([rng, rng, rng], axis=1)
    lvl_mask = np.tile(lvl_mask, (1, 1, GLA_HEADS))
    hk = np.kron(np.eye(GLA_HEADS), np.ones((C, GLA_DK)))
    hv = np.kron(np.eye(GLA_HEADS), np.ones((C, GLA_DV)))
    hs = np.kron(np.eye(GLA_HEADS), np.ones((GLA_DV, GLA_DK)))
    return (jnp.asarray(rng3, BF16), jnp.asarray(lvl_mask, F32), jnp.asarray(hk, BF16),
            jnp.asarray(hv, BF16), jnp.asarray(hs, F32))


def _gla_kernel(q_ref, k_ref, v_ref, g_ref, r_ref, rng_ref, lm_ref, hk_ref, hv_ref, hs_ref, nw_ref,
                o_ref, st_ref, *, ts):
    C = GLA_CHUNK

    @pl.when(pl.program_id(1) == 0)
    def _():
        st_ref[...] = jnp.zeros(st_ref.shape, F32)

    def gate_sums(c):
        g = g_ref[0, pl.ds(c * C, C), :]
        g_hi = g.astype(BF16)
        rem = g - g_hi.astype(F32)
        g_mid = rem.astype(BF16)
        g_lo = (rem - g_mid.astype(F32)).astype(BF16)
        g3 = jnp.concatenate([g_hi, g_mid, g_lo], axis=0)
        return jnp.dot(rng_ref[...], g3, preferred_element_type=F32)

    def decays(c, e):
        rows = pl.ds(c * C, C)
        q = q_ref[0, rows, :].astype(F32)
        k = k_ref[0, rows, :].astype(F32)
        f = jnp.exp(e)
        levels = []
        for l in range(GLA_LEVELS + 1):
            if l < GLA_LEVELS:
                fl = f[(2 + l) * C:(3 + l) * C]
                ql = (q * fl).astype(BF16)
                kl = (k * fl).astype(BF16)
            else:
                ql = q.astype(BF16)
                kl = k.astype(BF16)
            levels.append((ql, jnp.concatenate([kl] * GLA_HEADS, axis=0) * hk_ref[...]))
        return dict(q_in=(q * f[0:C]).astype(BF16),
                    k_out=(k * f[C:2 * C]).astype(BF16),
                    decay=f[C - 1:C],
                    levels=levels)

    def intra(d):
        attn = jnp.zeros((C, GLA_HEADS * C), F32)
        for l, (ql, k_bd) in enumerate(d["levels"]):
            a = lax.dot_general(ql, k_bd, (((1,), (1,)), ((), ())), preferred_element_type=F32)
            attn = attn + a * lm_ref[l]
        return attn.astype(BF16)

    def output(c, d, attn):
        rows = pl.ds(c * C, C)
        v = v_ref[0, rows, :]
        v_bd = jnp.concatenate([v] * GLA_HEADS, axis=0) * hv_ref[...]
        st = st_ref[...]
        o = jnp.dot(attn, v_bd, preferred_element_type=F32)
        o = o + lax.dot_general(d["q_in"], st.astype(BF16), (((1,), (1,)), ((), ())), preferred_element_type=F32)
        upd = lax.dot_general(v, d["k_out"], (((0,), (0,)), ((), ())), preferred_element_type=F32)
        st_ref[...] = st * d["decay"] + upd * hs_ref[...]
        parts = []
        for h in range(GLA_HEADS):
            oh = o[:, h * GLA_DV:(h + 1) * GLA_DV]
            ms = jnp.mean(oh * oh, axis=1, keepdims=True)
            parts.append(oh * lax.rsqrt(ms + LN_EPS) * nw_ref[...])
        on = jnp.concatenate(parts, axis=1)
        o_ref[0, rows, :] = (on * _silu(r_ref[0, rows, :].astype(F32))).astype(BF16)

    n = ts // C
    esum, dec, att = {0: gate_sums(0)}, {}, {}
    for t in range(n + 2):
        if 0 <= t - 1 < n:
            att[t - 1] = intra(dec[t - 1])
        if 0 <= t - 2 < n:
            output(t - 2, dec.pop(t - 2), att.pop(t - 2))
        if t + 1 < n:
            esum[t + 1] = gate_sums(t + 1)
        if t < n:
            dec[t] = decays(t, esum.pop(t))


def _gla(gq, gk, gv, gl, gr, nw, ts):
    B, S, _ = gq.shape
    rng3, lvl_mask, hk, hv, hs = _gla_tables()
    row = lambda b, s: (b, s, 0)
    const2 = lambda b, s: (0, 0)
    const3 = lambda b, s: (0, 0, 0)
    return pl.pallas_call(
        functools.partial(_gla_kernel, ts=ts),
        grid=(B, S // ts),
        in_specs=[pl.BlockSpec((1, ts, GLA_KW), row),
                  pl.BlockSpec((1, ts, GLA_KW), row),
                  pl.BlockSpec((1, ts, GLA_VW), row),
                  pl.BlockSpec((1, ts, GLA_KW), row),
                  pl.BlockSpec((1, ts, GLA_VW), row),
                  pl.BlockSpec(rng3.shape, const2),
                  pl.BlockSpec(lvl_mask.shape, const3),
                  pl.BlockSpec(hk.shape, const2),
                  pl.BlockSpec(hv.shape, const2),
                  pl.BlockSpec(hs.shape, const2),
                  pl.BlockSpec((1, GLA_DV), const2)],
        out_specs=pl.BlockSpec((1, ts, GLA_VW), row),
        out_shape=jax.ShapeDtypeStruct((B, S, GLA_VW), BF16),
        scratch_shapes=[pltpu.VMEM((GLA_VW, GLA_KW), F32)],
        compiler_params=pltpu.CompilerParams(
            dimension_semantics=("arbitrary", "arbitrary"), vmem_limit_bytes=VMEM_LIMIT),
        name="gla",
    )(gq, gk, gv, gl, gr, rng3, lvl_mask, hk, hv, hs, nw)


def _mlp_kernel(x_ref, dlo_ref, dhi_ref, g_ref, lg_ref, lb_ref, gta_ref, wod_ref, wog_ref, ag_ref, ab_ref,
                sc_ref, sh_ref, gtf_ref, wu_ref, cw_ref, cb_ref, wd_ref, fg_ref, fb_ref, o_ref,
                carry_ref, ubuf0_ref, ubuf1_ref, act_ref, u_ref, h_ref, *, tm, tf, nf, half_tiles):
    HALO = 8
    F = nf * tf
    ubufs = (ubuf0_ref, ubuf1_ref)
    s = pl.program_id(1)

    @pl.when(s == 0)
    def _():
        carry_ref[...] = jnp.zeros(carry_ref.shape, F32)

    d_out = jnp.where(s < half_tiles, dlo_ref[0], dhi_ref[0])
    hm = tm // 2
    for r in range(2):
        rows = slice(r * hm, (r + 1) * hm)
        h_in = _layer_norm(x_ref[0, rows, :], lg_ref[...], lb_ref[...])
        mix = (jnp.dot(d_out[rows], wod_ref[...], preferred_element_type=F32)
               + jnp.dot(g_ref[0, rows, :], wog_ref[...], preferred_element_type=F32))
        h = _layer_norm(DN_ALPHA * h_in + (1.0 + gta_ref[0]) * mix, ag_ref[...], ab_ref[...])
        h_ref[rows, :] = h
        u_ref[rows, :] = (h * (1.0 + sc_ref[0]) + sh_ref[0]).astype(BF16)

    def up_proj(f, slot):
        for half in range(2):
            cols = slice(half * F + f * tf, half * F + (f + 1) * tf)
            up = jnp.dot(u_ref[...], wu_ref[:, cols], preferred_element_type=F32)
            ubufs[slot][half, 0:HALO, :] = carry_ref[half, f]
            ubufs[slot][half, HALO:HALO + tm, :] = up
            carry_ref[half, f] = up[tm - HALO:tm, :]

    def gate(f, slot):
        def conv(half):
            cols = slice(half * F + f * tf, half * F + (f + 1) * tf)
            cw = cw_ref[:, cols]
            buf = ubufs[slot]
            return (cb_ref[:, cols]
                    + cw[0:1] * buf[half, HALO - 2:HALO - 2 + tm, :]
                    + cw[1:2] * buf[half, HALO - 1:HALO - 1 + tm, :]
                    + cw[2:3] * buf[half, HALO:HALO + tm, :])
        act_ref[f] = (_silu(conv(0)) * conv(1)).astype(BF16)

    up_proj(0, 0)
    for f in range(nf):
        if f + 1 < nf:
            up_proj(f + 1, (f + 1) % 2)
        gate(f, f % 2)

    for r in range(2):
        rows = slice(r * hm, (r + 1) * hm)
        ff = jnp.dot(act_ref[0, rows, :], wd_ref[0], preferred_element_type=F32)
        for f in range(1, nf):
            ff = ff + jnp.dot(act_ref[f, rows, :], wd_ref[f], preferred_element_type=F32)
        y = DN_ALPHA * h_ref[rows, :] + (1.0 + gtf_ref[0]) * ff
        o_ref[0, rows, :] = _layer_norm(y, fg_ref[...], fb_ref[...])


def _mlp(x, d_lo, d_hi, g_out, ln_g, ln_b, gt_a, w_od, w_og, ag, ab, sc, sh, gt_f, wu, cw, cb, wd, fg, fb, tm):
    B, S, D = x.shape
    nf, tf, _ = wd.shape
    half_tiles = d_lo.shape[1] // tm
    row = lambda b, s: (b, s, 0)
    per_b = lambda b, s: (b, 0, 0)

    def resident(shape):
        return pl.BlockSpec(shape, lambda b, s: (0,) * len(shape), pipeline_mode=pl.Buffered(1))

    return pl.pallas_call(
        functools.partial(_mlp_kernel, tm=tm, tf=tf, nf=nf, half_tiles=half_tiles),
        grid=(B, S // tm),
        in_specs=[pl.BlockSpec((1, tm, D), row),
                  pl.BlockSpec((1, tm, d_lo.shape[2]), lambda b, s: (b, jnp.minimum(s, half_tiles - 1), 0)),
                  pl.BlockSpec((1, tm, d_hi.shape[2]), lambda b, s: (b, jnp.maximum(s - half_tiles, 0), 0)),
                  pl.BlockSpec((1, tm, g_out.shape[2]), row),
                  resident((1, D)), resident((1, D)),
                  pl.BlockSpec((1, 1, D), per_b),
                  resident(w_od.shape), resident(w_og.shape),
                  resident((1, D)), resident((1, D)),
                  pl.BlockSpec((1, 1, D), per_b),
                  pl.BlockSpec((1, 1, D), per_b),
                  pl.BlockSpec((1, 1, D), per_b),
                  resident(wu.shape), resident(cw.shape), resident(cb.shape), resident(wd.shape),
                  resident((1, D)), resident((1, D))],
        out_specs=pl.BlockSpec((1, tm, D), row),
        out_shape=jax.ShapeDtypeStruct((B, S, D), F32),
        scratch_shapes=[pltpu.VMEM((2, nf, 8, tf), F32),
                        pltpu.VMEM((2, 8 + tm, tf), F32),
                        pltpu.VMEM((2, 8 + tm, tf), F32),
                        pltpu.VMEM((nf, tm, tf), BF16),
                        pltpu.VMEM((tm, D), BF16),
                        pltpu.VMEM((tm, D), F32)],
        compiler_params=pltpu.CompilerParams(
            dimension_semantics=("arbitrary", "arbitrary"), vmem_limit_bytes=VMEM_LIMIT),
        name="mlp",
    )(x, d_lo, d_hi, g_out, ln_g, ln_b, gt_a, w_od, w_og, ag, ab, sc, sh, gt_f, wu, cw, cb, wd, fg, fb)


def kernel(x, c, positions, ln_in_g, ln_in_b, w_ada, b_ada, w_in, lambda_q1, lambda_k1, lambda_q2, lambda_k2, diff_norm_w, gla_w_gate_up, gla_b_gate, gla_norm_w, w_out, ln_attn_g, ln_attn_b, w_up, conv_w, conv_b, w_down, ln_ffn_g, ln_ffn_b):
    B, S, D = x.shape
    assert D == D_MODEL and w_ada.shape[0] == 1
    tm = min(512, S)
    tf = 256
    nf = D_FF // tf

    c_pad = jnp.pad(c, ((0, 8 - B % 8 if B % 8 else 0), (0, 0)))
    ada = _ada(c_pad, w_ada[0], b_ada)[:B]
    sh_a, sc_a, gt_a, sh_f, sc_f, gt_f = [t[:, None, :] for t in jnp.split(ada, 6, axis=-1)]

    ln_g = ln_in_g[None, :]
    ln_b = ln_in_b[None, :]

    w_main = w_in[0, :, :N_MAIN]
    w_gg = jnp.pad(w_in[0, :, N_MAIN:], ((0, 0), (0, N_PROJ - w_in.shape[2])))
    w_proj = jnp.concatenate([w_main, w_gg], axis=1).astype(BF16)
    w_gate = jnp.pad(gla_w_gate_up[0], ((0, LANES - GLA_RANK), (0, 0))).astype(BF16)
    inv = ROPE_THETA ** (-jnp.arange(0, DIFF_DH, 2, dtype=F32) / DIFF_DH)
    qk, dv, gq, gk, gv, gr, gl = _proj(x, positions[:, None, :], inv[:, None], ln_g, ln_b, sc_a, sh_a,
                                       w_proj, w_gate, gla_b_gate, tm)

    d_lo, d_hi = _diffattn(qk, dv, lambda_q1, lambda_k1, lambda_q2, lambda_k2, diff_norm_w, tm)
    g_out = _gla(gq, gk, gv, gl, gr, gla_norm_w, min(512, S))

    w_o = w_out[0].astype(BF16)
    dw = d_lo.shape[2]
    wd = w_down[0].astype(BF16).reshape(nf, tf, D)
    return _mlp(x, d_lo, d_hi, g_out, ln_g, ln_b, gt_a, w_o[:dw], w_o[dw:], ln_attn_g, ln_attn_b,
                sc_f, sh_f, gt_f, w_up[0].astype(BF16), conv_w[0], conv_b, wd, ln_ffn_g, ln_ffn_b, tm)
```

```python
import functools
import math

import numpy as np
import jax
import jax.numpy as jnp
from jax import lax
from jax.experimental import pallas as pl
from jax.experimental.pallas import tpu as pltpu

F32 = jnp.float32
BF16 = jnp.bfloat16

D_MODEL = 1024
DIFF_DH = 64
DIFF_HEADS = 4
HEAD_W = 2 * DIFF_DH
GLA_HEADS = 4
GLA_DK = 64
GLA_DV = 128
GLA_KW = GLA_HEADS * GLA_DK
GLA_VW = GLA_HEADS * GLA_DV
GLA_RANK = 16
GLA_TAU = 16.0
GLA_CHUNK = 64
GLA_LEVELS = 6
GLA_MILD_DECAY = 30.0
D_FF = 2816
CONV_W = 3
ROPE_THETA = 10000.0
LN_EPS = 1e-5
DEPTH = 1
DN_ALPHA = (2.0 * DEPTH) ** 0.25
LAMBDA_INIT = 0.8 - 0.6 * math.exp(-0.3 * 0)

N_MAIN = 3072
N_PROJ = N_MAIN + 128
LANES = 128
NEG_BIG = -1e30
LOG2_E = math.log2(math.e)

VMEM_LIMIT = 56 * 1024 * 1024


def _layer_norm(x, g, b):
    mu = jnp.mean(x, axis=-1, keepdims=True)
    xc = x - mu
    var = jnp.mean(xc * xc, axis=-1, keepdims=True)
    return xc * lax.rsqrt(var + LN_EPS) * g + b


def _silu(x):
    return x * jax.nn.sigmoid(x)


def _ada_kernel(c_ref, w_ref, b_ref, o_ref):
    ca = _silu(c_ref[...])
    o_ref[...] = jnp.dot(ca.astype(BF16), w_ref[...].astype(BF16),
                         preferred_element_type=F32) + b_ref[...]


def _ada(c_pad, w_ada, b_ada):
    rows, d = c_pad.shape
    n = w_ada.shape[1]
    tn = 1024
    return pl.pallas_call(
        _ada_kernel,
        grid=(n // tn,),
        in_specs=[pl.BlockSpec((rows, d), lambda j: (0, 0)),
                  pl.BlockSpec((d, tn), lambda j: (0, j)),
                  pl.BlockSpec((1, tn), lambda j: (0, j))],
        out_specs=pl.BlockSpec((rows, tn), lambda j: (0, j)),
        out_shape=jax.ShapeDtypeStruct((rows, n), F32),
        name="ada",
    )(c_pad, w_ada, b_ada)


def _proj_kernel(x_ref, pos_ref, inv_ref, lg_ref, lb_ref, sc_ref, sh_ref, w_ref, wg_ref, bg_ref,
                 qk_ref, dv_ref, gq_ref, gk_ref, gv_ref, gr_ref, gl_ref, *, tm):
    ang_t = inv_ref[...] * pos_ref[0].astype(F32)
    reps = LANES // ang_t.shape[0]
    cos = jnp.concatenate([jnp.cos(ang_t)] * reps, axis=0).T
    sin = jnp.concatenate([jnp.sin(ang_t)] * reps, axis=0).T
    lane = lax.broadcasted_iota(jnp.int32, (1, LANES), 1)
    first_half = (lane & 32) == 0
    sin_signed = jnp.where(first_half, -sin, sin)
    scale = LOG2_E / math.sqrt(DIFF_DH)

    hm = tm // 2
    for part in range(2):
        rows = slice(part * hm, (part + 1) * hm)
        h = _layer_norm(x_ref[0, rows, :], lg_ref[...], lb_ref[...])
        u = (h * (1.0 + sc_ref[0]) + sh_ref[0]).astype(BF16)
        proj = jnp.dot(u, w_ref[...], preferred_element_type=F32)
        for j in range(8):
            xg = proj[:, j * LANES:(j + 1) * LANES]
            partner = jnp.where(first_half, pltpu.roll(xg, LANES - 32, axis=1), pltpu.roll(xg, 32, axis=1))
            r = xg * cos[rows] + partner * sin_signed[rows]
            if j < 4:
                r = r * scale
            qk_ref[0, rows, j * LANES:(j + 1) * LANES] = r.astype(BF16)
        dv_ref[0, rows, :] = proj[:, 1024:1536].astype(BF16)
        gq_ref[0, rows, :] = (proj[:, 1536:1792] * (GLA_DK ** -0.5)).astype(BF16)
        gk_ref[0, rows, :] = proj[:, 1792:2048].astype(BF16)
        gv_ref[0, rows, :] = proj[:, 2048:2560].astype(BF16)
        gr_ref[0, rows, :] = proj[:, 2560:3072].astype(BF16)
        gg = proj[:, N_MAIN:N_PROJ].astype(BF16)
        z = jnp.dot(gg, wg_ref[...], preferred_element_type=F32) + bg_ref[...]
        log_sig = jnp.minimum(z, 0.0) - jnp.log1p(jnp.exp(-jnp.abs(z)))
        gl_ref[0, rows, :] = log_sig * (1.0 / GLA_TAU)


def _proj(x, pos_row, inv_col, ln_g, ln_b, sc, sh, w, wg, bg, tm):
    B, S, D = x.shape
    grid = (B, S // tm)
    row = lambda b, s: (b, s, 0)
    const2 = lambda b, s: (0, 0)
    per_b = lambda b, s: (b, 0, 0)
    widths = (1024, 512, GLA_KW, GLA_KW, GLA_VW, GLA_VW)
    out_shape = [jax.ShapeDtypeStruct((B, S, n), BF16) for n in widths]
    out_shape.append(jax.ShapeDtypeStruct((B, S, GLA_KW), F32))
    out_specs = [pl.BlockSpec((1, tm, n), row) for n in widths]
    out_specs.append(pl.BlockSpec((1, tm, GLA_KW), row))
    return pl.pallas_call(
        functools.partial(_proj_kernel, tm=tm),
        grid=grid,
        in_specs=[pl.BlockSpec((1, tm, D), row),
                  pl.BlockSpec((1, 1, tm), lambda b, s: (b, 0, s)),
                  pl.BlockSpec(inv_col.shape, const2),
                  pl.BlockSpec((1, D), const2),
                  pl.BlockSpec((1, D), const2),
                  pl.BlockSpec((1, 1, D), per_b),
                  pl.BlockSpec((1, 1, D), per_b),
                  pl.BlockSpec((D, N_PROJ), const2),
                  pl.BlockSpec((LANES, GLA_KW), const2),
                  pl.BlockSpec((1, GLA_KW), const2)],
        out_specs=out_specs,
        out_shape=out_shape,
        compiler_params=pltpu.CompilerParams(
            dimension_semantics=("arbitrary", "arbitrary"), vmem_limit_bytes=VMEM_LIMIT),
        name="proj",
    )(x, pos_row, inv_col, ln_g, ln_b, sc, sh, w, wg, bg)


def _diffattn_kernel(qa_ref, qb_ref, k_ref, v_ref, lq1_ref, lk1_ref, lq2_ref, lk2_ref, nw_ref,
                     oa_ref, ob_ref, qz_ref, m_ref, acc_ref, *, tq, nt):
    i = pl.program_id(2)
    lane = lax.broadcasted_iota(jnp.int32, (1, HEAD_W), 1)
    for t, q_ref in enumerate((qa_ref, qb_ref)):
        q = q_ref[0]
        zero = jnp.zeros_like(q)
        qz_ref[t] = jnp.concatenate([jnp.where(lane < DIFF_DH, q, zero), jnp.where(lane >= DIFF_DH, q, zero)], axis=0)
    m_ref[...] = jnp.full(m_ref.shape, NEG_BIG, F32)
    acc_ref[...] = jnp.zeros(acc_ref.shape, F32)
    ones = jnp.ones((tq, LANES), BF16)

    def block_step(own, blk, masked):
        rows = pl.ds(pl.multiple_of(blk * tq, tq), tq)
        kb = k_ref[0, rows, :]
        vb = jnp.concatenate([v_ref[0, rows, :], ones], axis=1)
        s = lax.dot_general(qz_ref[own], kb, (((1,), (1,)), ((), ())), preferred_element_type=F32)
        if masked:
            r = lax.broadcasted_iota(jnp.int32, (2 * tq, tq), 0)
            c = lax.broadcasted_iota(jnp.int32, (2 * tq, tq), 1)
            r = jnp.where(r >= tq, r - tq, r)
            s = jnp.where(c <= r, s, NEG_BIG)
        s_fold = s[:, 0:LANES]
        for t in range(1, tq // LANES):
            s_fold = jnp.maximum(s_fold, s[:, t * LANES:(t + 1) * LANES])
        m_prev = m_ref[own]
        m_new = jnp.maximum(m_prev, jnp.max(s_fold, axis=1, keepdims=True))
        alpha = jnp.exp2(m_prev - m_new)
        p = jnp.exp2(s - jnp.concatenate([m_new] * (tq // LANES), axis=1))
        acc_ref[own] = (jnp.concatenate([alpha, alpha], axis=1) * acc_ref[own]
                        + jnp.dot(p.astype(BF16), vb, preferred_element_type=F32))
        m_ref[own] = m_new

    block_step(0, i, True)
    for t in range(1, nt):
        is_a = t <= i
        block_step(jnp.where(is_a, 0, 1), jnp.where(is_a, i - t, t - i - 1), False)
    block_step(1, nt - 1 - i, True)

    lam = (jnp.exp(jnp.sum(lq1_ref[...] * lk1_ref[...], axis=1, keepdims=True))
           - jnp.exp(jnp.sum(lq2_ref[...] * lk2_ref[...], axis=1, keepdims=True)) + LAMBDA_INIT)
    for t, o_ref in enumerate((oa_ref, ob_ref)):
        acc = acc_ref[t]
        o = acc[:, 0:HEAD_W] / acc[:, HEAD_W:2 * HEAD_W]
        d = o[0:tq] - lam * o[tq:2 * tq]
        ms = jnp.mean(d * d, axis=1, keepdims=True)
        o_ref[0] = (d * lax.rsqrt(ms + LN_EPS) * nw_ref[...] * (1.0 - LAMBDA_INIT)).astype(BF16)


def _diffattn(qk, dv, lq1, lk1, lq2, lk2, nw, tq):
    B, S, _ = qk.shape
    nt = S // tq
    assert nt % 2 == 0, "query tiles are processed in pairs (i, nt-1-i)"
    half = nt // 2
    const2 = lambda b, h, i: (0, 0)
    out = jax.ShapeDtypeStruct((B, S // 2, DIFF_HEADS * HEAD_W), BF16)
    lo, hi = pl.pallas_call(
        functools.partial(_diffattn_kernel, tq=tq, nt=nt),
        grid=(B, DIFF_HEADS, half),
        in_specs=[pl.BlockSpec((1, tq, HEAD_W), lambda b, h, i: (b, i, h)),
                  pl.BlockSpec((1, tq, HEAD_W), lambda b, h, i: (b, nt - 1 - i, h)),
                  pl.BlockSpec((1, S, HEAD_W), lambda b, h, i: (b, 0, DIFF_HEADS + h)),
                  pl.BlockSpec((1, S, HEAD_W), lambda b, h, i: (b, 0, h)),
                  pl.BlockSpec((1, DIFF_DH), const2),
                  pl.BlockSpec((1, DIFF_DH), const2),
                  pl.BlockSpec((1, DIFF_DH), const2),
                  pl.BlockSpec((1, DIFF_DH), const2),
                  pl.BlockSpec((1, HEAD_W), const2)],
        out_specs=[pl.BlockSpec((1, tq, HEAD_W), lambda b, h, i: (b, i, h)),
                   pl.BlockSpec((1, tq, HEAD_W), lambda b, h, i: (b, half - 1 - i, h))],
        out_shape=[out, out],
        scratch_shapes=[pltpu.VMEM((2, 2 * tq, HEAD_W), BF16),
                        pltpu.VMEM((2, 2 * tq, LANES), F32),
                        pltpu.VMEM((2, 2 * tq, 2 * HEAD_W), F32)],
        compiler_params=pltpu.CompilerParams(
            dimension_semantics=("arbitrary", "arbitrary", "arbitrary"), vmem_limit_bytes=VMEM_LIMIT),
        name="diffattn",
    )(qk, qk, qk, dv, lq1, lk1, lq2, lk2, nw)
    return lo, hi


def _gla_tables():
    C = GLA_CHUNK
    t = np.arange(C)
    rng = np.zeros((2 + GLA_LEVELS, C, C), np.float32)
    rng[0] = (t[None, :] <= t[:, None])
    rng[1] = (t[None, :] > t[:, None])
    lvl_mask = np.zeros((GLA_LEVELS + 1, C, C), np.float32)
    for l in range(GLA_LEVELS):
        s = C >> (l + 1)
        blk = t // (2 * s)
        mid = blk * 2 * s + s
        upper = (t % (2 * s)) >= s
        for i in range(C):
            if upper[i]:
                rng[2 + l, i, mid[i]:i + 1] = 1.0
            else:
                rng[2 + l, i, i + 1:mid[i]] = 1.0
        lvl_mask[l] = (blk[:, None] == blk[None, :]) & upper[:, None] & (~upper[None, :])
    lvl_mask[GLA_LEVELS] = np.eye(C)
    rng = rng.reshape((2 + GLA_LEVELS) * C, C)
    rng3 = np.concatenate([rng, rng, rng], axis=1)
    lvl_mask = np.tile(lvl_mask, (1, 1, GLA_HEADS))
    hk = np.kron(np.eye(GLA_HEADS), np.ones((C, GLA_DK)))
    hv = np.kron(np.eye(GLA_HEADS), np.ones((C, GLA_DV)))
    hs = np.kron(np.eye(GLA_HEADS), np.ones((GLA_DV, GLA_DK)))
    return (jnp.asarray(rng3, BF16), jnp.asarray(lvl_mask, F32), jnp.asarray(hk, BF16),
            jnp.asarray(hv, BF16), jnp.asarray(hs, F32))


def _gla_kernel(q_ref, k_ref, v_ref, g_ref, r_ref, rng_ref, lm_ref, tril_ref, hk_ref, hv_ref, hs_ref, nw_ref,
                o_ref, st_ref, *, ts):
    C = GLA_CHUNK

    @pl.when(pl.program_id(1) == 0)
    def _():
        st_ref[...] = jnp.zeros(st_ref.shape, F32)

    def gate_sums(c, mild):
        g = g_ref[0, pl.ds(c * C, C), :]
        g_hi = g.astype(BF16)
        rem = g - g_hi.astype(F32)
        g_mid = rem.astype(BF16)
        g_lo = (rem - g_mid.astype(F32)).astype(BF16)
        g3 = jnp.concatenate([g_hi, g_mid, g_lo], axis=0)
        rng = rng_ref[0:C, :] if mild else rng_ref[...]
        return jnp.dot(rng, g3, preferred_element_type=F32)

    def block_diag(kl):
        return jnp.concatenate([kl] * GLA_HEADS, axis=0) * hk_ref[...]

    def decays_mild(c, b):
        rows = pl.ds(c * C, C)
        q = q_ref[0, rows, :].astype(F32)
        k = k_ref[0, rows, :].astype(F32)
        b_last = b[C - 1:C]
        ref = b[C // 2 - 1:C // 2]
        levels = [((q * jnp.exp(b - ref)).astype(BF16), block_diag((k * jnp.exp(ref - b)).astype(BF16)))]
        return dict(q_in=(q * jnp.exp(b)).astype(BF16),
                    k_out=(k * jnp.exp(b_last - b)).astype(BF16),
                    decay=jnp.exp(b_last),
                    levels=levels)

    def decays(c, e):
        rows = pl.ds(c * C, C)
        q = q_ref[0, rows, :].astype(F32)
        k = k_ref[0, rows, :].astype(F32)
        f = jnp.exp(e)
        levels = []
        for l in range(GLA_LEVELS + 1):
            if l < GLA_LEVELS:
                fl = f[(2 + l) * C:(3 + l) * C]
                ql = (q * fl).astype(BF16)
                kl = (k * fl).astype(BF16)
            else:
                ql = q.astype(BF16)
                kl = k.astype(BF16)
            levels.append((ql, block_diag(kl)))
        return dict(q_in=(q * f[0:C]).astype(BF16),
                    k_out=(k * f[C:2 * C]).astype(BF16),
                    decay=f[C - 1:C],
                    levels=levels)

    def intra(d, mild):
        if mild:
            ql, k_bd = d["levels"][0]
            a = lax.dot_general(ql, k_bd, (((1,), (1,)), ((), ())), preferred_element_type=F32)
            return jnp.where(tril_ref[...] > 0.5, a, 0.0).astype(BF16)
        attn = jnp.zeros((C, GLA_HEADS * C), F32)
        for l, (ql, k_bd) in enumerate(d["levels"]):
            a = lax.dot_general(ql, k_bd, (((1,), (1,)), ((), ())), preferred_element_type=F32)
            attn = attn + a * lm_ref[l]
        return attn.astype(BF16)

    def output(c, d, attn):
        rows = pl.ds(c * C, C)
        v = v_ref[0, rows, :]
        v_bd = jnp.concatenate([v] * GLA_HEADS, axis=0) * hv_ref[...]
        st = st_ref[...]
        o = jnp.dot(attn, v_bd, preferred_element_type=F32)
        o = o + lax.dot_general(d["q_in"], st.astype(BF16), (((1,), (1,)), ((), ())), preferred_element_type=F32)
        upd = lax.dot_general(v, d["k_out"], (((0,), (0,)), ((), ())), preferred_element_type=F32)
        st_ref[...] = st * d["decay"] + upd * hs_ref[...]
        parts = []
        for h in range(GLA_HEADS):
            oh = o[:, h * GLA_DV:(h + 1) * GLA_DV]
            ms = jnp.mean(oh * oh, axis=1, keepdims=True)
            parts.append(oh * lax.rsqrt(ms + LN_EPS) * nw_ref[...])
        on = jnp.concatenate(parts, axis=1)
        o_ref[0, rows, :] = (on * _silu(r_ref[0, rows, :].astype(F32))).astype(BF16)

    n = ts // C

    def run(mild):
        prep = decays_mild if mild else decays
        esum, dec, att = {0: gate_sums(0, mild)}, {}, {}
        for t in range(n + 2):
            if 0 <= t - 1 < n:
                att[t - 1] = intra(dec[t - 1], mild)
            if 0 <= t - 2 < n:
                output(t - 2, dec.pop(t - 2), att.pop(t - 2))
            if t + 1 < n:
                esum[t + 1] = gate_sums(t + 1, mild)
            if t < n:
                dec[t] = prep(t, esum.pop(t))

    totals = [jnp.sum(g_ref[0, pl.ds(c * C, C), :], axis=0, keepdims=True) for c in range(n)]
    mild = jnp.min(jnp.concatenate(totals, axis=0)) >= -GLA_MILD_DECAY

    @pl.when(mild)
    def _():
        run(True)

    @pl.when(jnp.logical_not(mild))
    def _():
        run(False)


def _gla(gq, gk, gv, gl, gr, nw, ts):
    B, S, _ = gq.shape
    rng3, lvl_mask, hk, hv, hs = _gla_tables()
    tril = jnp.sum(lvl_mask, axis=0)
    row = lambda b, s: (b, s, 0)
    const2 = lambda b, s: (0, 0)
    const3 = lambda b, s: (0, 0, 0)
    return pl.pallas_call(
        functools.partial(_gla_kernel, ts=ts),
        grid=(B, S // ts),
        in_specs=[pl.BlockSpec((1, ts, GLA_KW), row),
                  pl.BlockSpec((1, ts, GLA_KW), row),
                  pl.BlockSpec((1, ts, GLA_VW), row),
                  pl.BlockSpec((1, ts, GLA_KW), row),
                  pl.BlockSpec((1, ts, GLA_VW), row),
                  pl.BlockSpec(rng3.shape, const2),
                  pl.BlockSpec(lvl_mask.shape, const3),
                  pl.BlockSpec(tril.shape, const2),
                  pl.BlockSpec(hk.shape, const2),
                  pl.BlockSpec(hv.shape, const2),
                  pl.BlockSpec(hs.shape, const2),
                  pl.BlockSpec((1, GLA_DV), const2)],
        out_specs=pl.BlockSpec((1, ts, GLA_VW), row),
        out_shape=jax.ShapeDtypeStruct((B, S, GLA_VW), BF16),
        scratch_shapes=[pltpu.VMEM((GLA_VW, GLA_KW), F32)],
        compiler_params=pltpu.CompilerParams(
            dimension_semantics=("arbitrary", "arbitrary"), vmem_limit_bytes=VMEM_LIMIT),
        name="gla",
    )(gq, gk, gv, gl, gr, rng3, lvl_mask, tril, hk, hv, hs, nw)


def _mlp_kernel(x_ref, dlo_ref, dhi_ref, g_ref, lg_ref, lb_ref, gta_ref, wod_ref, wog_ref, ag_ref, ab_ref,
                sc_ref, sh_ref, gtf_ref, wu_ref, cw_ref, cb_ref, wd_ref, fg_ref, fb_ref, o_ref,
                carry_ref, ubuf0_ref, ubuf1_ref, act_ref, u_ref, h_ref, *, tm, tf, nf, half_tiles):
    HALO = 8
    F = nf * tf
    ubufs = (ubuf0_ref, ubuf1_ref)
    s = pl.program_id(1)

    @pl.when(s == 0)
    def _():
        carry_ref[...] = jnp.zeros(carry_ref.shape, F32)

    d_out = jnp.where(s < half_tiles, dlo_ref[0], dhi_ref[0])
    hm = tm // 2
    for r in range(2):
        rows = slice(r * hm, (r + 1) * hm)
        h_in = _layer_norm(x_ref[0, rows, :], lg_ref[...], lb_ref[...])
        mix = (jnp.dot(d_out[rows], wod_ref[...], preferred_element_type=F32)
               + jnp.dot(g_ref[0, rows, :], wog_ref[...], preferred_element_type=F32))
        h = _layer_norm(DN_ALPHA * h_in + (1.0 + gta_ref[0]) * mix, ag_ref[...], ab_ref[...])
        h_ref[rows, :] = h
        u_ref[rows, :] = (h * (1.0 + sc_ref[0]) + sh_ref[0]).astype(BF16)

    def up_proj(f, slot):
        for half in range(2):
            cols = slice(half * F + f * tf, half * F + (f + 1) * tf)
            up = jnp.dot(u_ref[...], wu_ref[:, cols], preferred_element_type=F32)
            ubufs[slot][half, 0:HALO, :] = carry_ref[half, f]
            ubufs[slot][half, HALO:HALO + tm, :] = up
            carry_ref[half, f] = up[tm - HALO:tm, :]

    def gate(f, slot):
        def conv(half):
            cols = slice(half * F + f * tf, half * F + (f + 1) * tf)
            cw = cw_ref[:, cols]
            buf = ubufs[slot]
            return (cb_ref[:, cols]
                    + cw[0:1] * buf[half, HALO - 2:HALO - 2 + tm, :]
                    + cw[1:2] * buf[half, HALO - 1:HALO - 1 + tm, :]
                    + cw[2:3] * buf[half, HALO:HALO + tm, :])
        act_ref[f] = (_silu(conv(0)) * conv(1)).astype(BF16)

    up_proj(0, 0)
    for f in range(nf):
        if f + 1 < nf:
            up_proj(f + 1, (f + 1) % 2)
        gate(f, f % 2)

    for r in range(2):
        rows = slice(r * hm, (r + 1) * hm)
        ff = jnp.dot(act_ref[0, rows, :], wd_ref[0], preferred_element_type=F32)
        for f in range(1, nf):
            ff = ff + jnp.dot(act_ref[f, rows, :], wd_ref[f], preferred_element_type=F32)
        y = DN_ALPHA * h_ref[rows, :] + (1.0 + gtf_ref[0]) * ff
        o_ref[0, rows, :] = _layer_norm(y, fg_ref[...], fb_ref[...])


def _mlp(x, d_lo, d_hi, g_out, ln_g, ln_b, gt_a, w_od, w_og, ag, ab, sc, sh, gt_f, wu, cw, cb, wd, fg, fb, tm):
    B, S, D = x.shape
    nf, tf, _ = wd.shape
    half_tiles = d_lo.shape[1] // tm
    row = lambda b, s: (b, s, 0)
    per_b = lambda b, s: (b, 0, 0)

    def resident(shape):
        return pl.BlockSpec(shape, lambda b, s: (0,) * len(shape), pipeline_mode=pl.Buffered(1))

    return pl.pallas_call(
        functools.partial(_mlp_kernel, tm=tm, tf=tf, nf=nf, half_tiles=half_tiles),
        grid=(B, S // tm),
        in_specs=[pl.BlockSpec((1, tm, D), row),
                  pl.BlockSpec((1, tm, d_lo.shape[2]), lambda b, s: (b, jnp.minimum(s, half_tiles - 1), 0)),
                  pl.BlockSpec((1, tm, d_hi.shape[2]), lambda b, s: (b, jnp.maximum(s - half_tiles, 0), 0)),
                  pl.BlockSpec((1, tm, g_out.shape[2]), row),
                  resident((1, D)), resident((1, D)),
                  pl.BlockSpec((1, 1, D), per_b),
                  resident(w_od.shape), resident(w_og.shape),
                  resident((1, D)), resident((1, D)),
                  pl.BlockSpec((1, 1, D), per_b),
                  pl.BlockSpec((1, 1, D), per_b),
                  pl.BlockSpec((1, 1, D), per_b),
                  resident(wu.shape), resident(cw.shape), resident(cb.shape), resident(wd.shape),
                  resident((1, D)), resident((1, D))],
        out_specs=pl.BlockSpec((1, tm, D), row),
        out_shape=jax.ShapeDtypeStruct((B, S, D), F32),
        scratch_shapes=[pltpu.VMEM((2, nf, 8, tf), F32),
                        pltpu.VMEM((2, 8 + tm, tf), F32),
                        pltpu.VMEM((2, 8 + tm, tf), F32),
                        pltpu.VMEM((nf, tm, tf), BF16),
                        pltpu.VMEM((tm, D), BF16),
                        pltpu.VMEM((tm, D), F32)],
        compiler_params=pltpu.CompilerParams(
            dimension_semantics=("arbitrary", "arbitrary"), vmem_limit_bytes=VMEM_LIMIT),
        name="mlp",
    )(x, d_lo, d_hi, g_out, ln_g, ln_b, gt_a, w_od, w_og, ag, ab, sc, sh, gt_f, wu, cw, cb, wd, fg, fb)


def kernel(x, c, positions, ln_in_g, ln_in_b, w_ada, b_ada, w_in, lambda_q1, lambda_k1, lambda_q2, lambda_k2, diff_norm_w, gla_w_gate_up, gla_b_gate, gla_norm_w, w_out, ln_attn_g, ln_attn_b, w_up, conv_w, conv_b, w_down, ln_ffn_g, ln_ffn_b):
    B, S, D = x.shape
    assert D == D_MODEL and w_ada.shape[0] == 1
    tm = min(512, S)
    tf = 256
    nf = D_FF // tf

    c_pad = jnp.pad(c, ((0, 8 - B % 8 if B % 8 else 0), (0, 0)))
    ada = _ada(c_pad, w_ada[0], b_ada)[:B]
    sh_a, sc_a, gt_a, sh_f, sc_f, gt_f = [t[:, None, :] for t in jnp.split(ada, 6, axis=-1)]

    ln_g = ln_in_g[None, :]
    ln_b = ln_in_b[None, :]

    w_main = w_in[0, :, :N_MAIN]
    w_gg = jnp.pad(w_in[0, :, N_MAIN:], ((0, 0), (0, N_PROJ - w_in.shape[2])))
    w_proj = jnp.concatenate([w_main, w_gg], axis=1).astype(BF16)
    w_gate = jnp.pad(gla_w_gate_up[0], ((0, LANES - GLA_RANK), (0, 0))).astype(BF16)
    inv = ROPE_THETA ** (-jnp.arange(0, DIFF_DH, 2, dtype=F32) / DIFF_DH)
    qk, dv, gq, gk, gv, gr, gl = _proj(x, positions[:, None, :], inv[:, None], ln_g, ln_b, sc_a, sh_a,
                                       w_proj, w_gate, gla_b_gate, tm)

    d_lo, d_hi = _diffattn(qk, dv, lambda_q1, lambda_k1, lambda_q2, lambda_k2, diff_norm_w, tm)
    g_out = _gla(gq, gk, gv, gl, gr, gla_norm_w, min(512, S))

    w_o = w_out[0].astype(BF16)
    dw = d_lo.shape[2]
    wd = w_down[0].astype(BF16).reshape(nf, tf, D)
    return _mlp(x, d_lo, d_hi, g_out, ln_g, ln_b, gt_a, w_o[:dw], w_o[dw:], ln_attn_g, ln_attn_b,
                sc_f, sh_f, gt_f, w_up[0].astype(BF16), conv_w[0], conv_b, wd, ln_ffn_g, ln_ffn_b, tm)
```

```python
import functools
import math

import numpy as np
import jax
import jax.numpy as jnp
from jax import lax
from jax.experimental import pallas as pl
from jax.experimental.pallas import tpu as pltpu

F32 = jnp.float32
BF16 = jnp.bfloat16

D_MODEL = 1024
DIFF_DH = 64
DIFF_HEADS = 4
HEAD_W = 2 * DIFF_DH
GLA_HEADS = 4
GLA_DK = 64
GLA_DV = 128
GLA_KW = GLA_HEADS * GLA_DK
GLA_VW = GLA_HEADS * GLA_DV
GLA_RANK = 16
GLA_TAU = 16.0
GLA_CHUNK = 64
GLA_LEVELS = 6
GLA_MILD_DECAY = 30.0
D_FF = 2816
CONV_W = 3
ROPE_THETA = 10000.0
LN_EPS = 1e-5
DEPTH = 1
DN_ALPHA = (2.0 * DEPTH) ** 0.25
LAMBDA_INIT = 0.8 - 0.6 * math.exp(-0.3 * 0)

N_MAIN = 3072
N_PROJ = N_MAIN + 128
LANES = 128
NEG_BIG = -1e30
LOG2_E = math.log2(math.e)

VMEM_LIMIT = 56 * 1024 * 1024


def _layer_norm(x, g, b):
    mu = jnp.mean(x, axis=-1, keepdims=True)
    xc = x - mu
    var = jnp.mean(xc * xc, axis=-1, keepdims=True)
    return xc * lax.rsqrt(var + LN_EPS) * g + b


def _silu(x):
    return x * jax.nn.sigmoid(x)


def _ada_kernel(c_ref, w_ref, b_ref, o_ref):
    ca = _silu(c_ref[...])
    o_ref[...] = jnp.dot(ca.astype(BF16), w_ref[...].astype(BF16),
                         preferred_element_type=F32) + b_ref[...]


def _ada(c_pad, w_ada, b_ada):
    rows, d = c_pad.shape
    n = w_ada.shape[1]
    tn = 1024
    return pl.pallas_call(
        _ada_kernel,
        grid=(n // tn,),
        in_specs=[pl.BlockSpec((rows, d), lambda j: (0, 0)),
                  pl.BlockSpec((d, tn), lambda j: (0, j)),
                  pl.BlockSpec((1, tn), lambda j: (0, j))],
        out_specs=pl.BlockSpec((rows, tn), lambda j: (0, j)),
        out_shape=jax.ShapeDtypeStruct((rows, n), F32),
        name="ada",
    )(c_pad, w_ada, b_ada)


def _proj_kernel(x_ref, pos_ref, inv_ref, lg_ref, lb_ref, sc_ref, sh_ref, w_ref, wg_ref, bg_ref,
                 qk_ref, dv_ref, gq_ref, gk_ref, gv_ref, gr_ref, gl_ref, *, tm):
    ang_t = inv_ref[...] * pos_ref[0].astype(F32)
    reps = LANES // ang_t.shape[0]
    cos = jnp.concatenate([jnp.cos(ang_t)] * reps, axis=0).T
    sin = jnp.concatenate([jnp.sin(ang_t)] * reps, axis=0).T
    lane = lax.broadcasted_iota(jnp.int32, (1, LANES), 1)
    first_half = (lane & 32) == 0
    sin_signed = jnp.where(first_half, -sin, sin)
    scale = LOG2_E / math.sqrt(DIFF_DH)

    hm = tm // 2
    for part in range(2):
        rows = slice(part * hm, (part + 1) * hm)
        h = _layer_norm(x_ref[0, rows, :], lg_ref[...], lb_ref[...])
        u = (h * (1.0 + sc_ref[0]) + sh_ref[0]).astype(BF16)
        proj = jnp.dot(u, w_ref[...], preferred_element_type=F32)
        for j in range(8):
            xg = proj[:, j * LANES:(j + 1) * LANES]
            partner = jnp.where(first_half, pltpu.roll(xg, LANES - 32, axis=1), pltpu.roll(xg, 32, axis=1))
            r = xg * cos[rows] + partner * sin_signed[rows]
            if j < 4:
                r = r * scale
            qk_ref[0, rows, j * LANES:(j + 1) * LANES] = r.astype(BF16)
        dv_ref[0, rows, :] = proj[:, 1024:1536].astype(BF16)
        gq_ref[0, rows, :] = (proj[:, 1536:1792] * (GLA_DK ** -0.5)).astype(BF16)
        gk_ref[0, rows, :] = proj[:, 1792:2048].astype(BF16)
        gv_ref[0, rows, :] = proj[:, 2048:2560].astype(BF16)
        gr_ref[0, rows, :] = proj[:, 2560:3072].astype(BF16)
        gg = proj[:, N_MAIN:N_PROJ].astype(BF16)
        z = jnp.dot(gg, wg_ref[...], preferred_element_type=F32) + bg_ref[...]
        log_sig = jnp.minimum(z, 0.0) - jnp.log1p(jnp.exp(-jnp.abs(z)))
        gl_ref[0, rows, :] = log_sig * (1.0 / GLA_TAU)


def _proj(x, pos_row, inv_col, ln_g, ln_b, sc, sh, w, wg, bg, tm):
    B, S, D = x.shape
    grid = (B, S // tm)
    row = lambda b, s: (b, s, 0)
    const2 = lambda b, s: (0, 0)
    per_b = lambda b, s: (b, 0, 0)
    widths = (1024, 512, GLA_KW, GLA_KW, GLA_VW, GLA_VW)
    out_shape = [jax.ShapeDtypeStruct((B, S, n), BF16) for n in widths]
    out_shape.append(jax.ShapeDtypeStruct((B, S, GLA_KW), F32))
    out_specs = [pl.BlockSpec((1, tm, n), row) for n in widths]
    out_specs.append(pl.BlockSpec((1, tm, GLA_KW), row))
    return pl.pallas_call(
        functools.partial(_proj_kernel, tm=tm),
        grid=grid,
        in_specs=[pl.BlockSpec((1, tm, D), row),
                  pl.BlockSpec((1, 1, tm), lambda b, s: (b, 0, s)),
                  pl.BlockSpec(inv_col.shape, const2),
                  pl.BlockSpec((1, D), const2),
                  pl.BlockSpec((1, D), const2),
                  pl.BlockSpec((1, 1, D), per_b),
                  pl.BlockSpec((1, 1, D), per_b),
                  pl.BlockSpec((D, N_PROJ), const2),
                  pl.BlockSpec((LANES, GLA_KW), const2),
                  pl.BlockSpec((1, GLA_KW), const2)],
        out_specs=out_specs,
        out_shape=out_shape,
        compiler_params=pltpu.CompilerParams(
            dimension_semantics=("arbitrary", "arbitrary"), vmem_limit_bytes=VMEM_LIMIT),
        name="proj",
    )(x, pos_row, inv_col, ln_g, ln_b, sc, sh, w, wg, bg)


def _diffattn_kernel(qa_ref, qb_ref, k_ref, v_ref, lq1_ref, lk1_ref, lq2_ref, lk2_ref, nw_ref,
                     oa_ref, ob_ref, qz_ref, m_ref, acc_ref, *, tq, nt, hps):
    i = pl.program_id(2)
    lane = lax.broadcasted_iota(jnp.int32, (1, HEAD_W), 1)
    for hh in range(hps):
        cols = slice(hh * HEAD_W, (hh + 1) * HEAD_W)
        for t, q_ref in enumerate((qa_ref, qb_ref)):
            q = q_ref[0, :, cols]
            zero = jnp.zeros_like(q)
            qz_ref[hh, t] = jnp.concatenate([jnp.where(lane < DIFF_DH, q, zero),
                                             jnp.where(lane >= DIFF_DH, q, zero)], axis=0)
    m_ref[...] = jnp.full(m_ref.shape, NEG_BIG, F32)
    acc_ref[...] = jnp.zeros(acc_ref.shape, F32)
    ones = jnp.ones((tq, LANES), BF16)

    def block_step(hh, own, blk, masked):
        cols = slice(hh * HEAD_W, (hh + 1) * HEAD_W)
        rows = pl.ds(pl.multiple_of(blk * tq, tq), tq)
        kb = k_ref[0, rows, cols]
        vb = jnp.concatenate([v_ref[0, rows, cols], ones], axis=1)
        s = lax.dot_general(qz_ref[hh, own], kb, (((1,), (1,)), ((), ())), preferred_element_type=F32)
        if masked:
            r = lax.broadcasted_iota(jnp.int32, (2 * tq, tq), 0)
            c = lax.broadcasted_iota(jnp.int32, (2 * tq, tq), 1)
            r = jnp.where(r >= tq, r - tq, r)
            s = jnp.where(c <= r, s, NEG_BIG)
        s_fold = s[:, 0:LANES]
        for t in range(1, tq // LANES):
            s_fold = jnp.maximum(s_fold, s[:, t * LANES:(t + 1) * LANES])
        m_prev = m_ref[hh, own]
        m_new = jnp.maximum(m_prev, jnp.max(s_fold, axis=1, keepdims=True))
        alpha = jnp.exp2(m_prev - m_new)
        p = jnp.exp2(s - jnp.concatenate([m_new] * (tq // LANES), axis=1))
        acc_ref[hh, own] = (jnp.concatenate([alpha, alpha], axis=1) * acc_ref[hh, own]
                            + jnp.dot(p.astype(BF16), vb, preferred_element_type=F32))
        m_ref[hh, own] = m_new

    for hh in range(hps):
        block_step(hh, 0, i, True)
    for t in range(1, nt):
        is_a = t <= i
        for hh in range(hps):
            block_step(hh, jnp.where(is_a, 0, 1), jnp.where(is_a, i - t, t - i - 1), False)
    for hh in range(hps):
        block_step(hh, 1, nt - 1 - i, True)

    lam = (jnp.exp(jnp.sum(lq1_ref[...] * lk1_ref[...], axis=1, keepdims=True))
           - jnp.exp(jnp.sum(lq2_ref[...] * lk2_ref[...], axis=1, keepdims=True)) + LAMBDA_INIT)
    for hh in range(hps):
        cols = slice(hh * HEAD_W, (hh + 1) * HEAD_W)
        for t, o_ref in enumerate((oa_ref, ob_ref)):
            acc = acc_ref[hh, t]
            o = acc[:, 0:HEAD_W] / acc[:, HEAD_W:2 * HEAD_W]
            d = o[0:tq] - lam * o[tq:2 * tq]
            ms = jnp.mean(d * d, axis=1, keepdims=True)
            o_ref[0, :, cols] = (d * lax.rsqrt(ms + LN_EPS) * nw_ref[...] * (1.0 - LAMBDA_INIT)).astype(BF16)


def _diffattn(qk, dv, lq1, lk1, lq2, lk2, nw, tq, hps):
    B, S, _ = qk.shape
    nt = S // tq
    assert nt % 2 == 0, "query tiles are processed in pairs (i, nt-1-i)"
    assert DIFF_HEADS % hps == 0
    half = nt // 2
    groups = DIFF_HEADS // hps
    gw = hps * HEAD_W
    const2 = lambda b, h, i: (0, 0)
    out = jax.ShapeDtypeStruct((B, S // 2, DIFF_HEADS * HEAD_W), BF16)
    lo, hi = pl.pallas_call(
        functools.partial(_diffattn_kernel, tq=tq, nt=nt, hps=hps),
        grid=(B, groups, half),
        in_specs=[pl.BlockSpec((1, tq, gw), lambda b, h, i: (b, i, h)),
                  pl.BlockSpec((1, tq, gw), lambda b, h, i: (b, nt - 1 - i, h)),
                  pl.BlockSpec((1, S, gw), lambda b, h, i: (b, 0, groups + h)),
                  pl.BlockSpec((1, S, gw), lambda b, h, i: (b, 0, h)),
                  pl.BlockSpec((1, DIFF_DH), const2),
                  pl.BlockSpec((1, DIFF_DH), const2),
                  pl.BlockSpec((1, DIFF_DH), const2),
                  pl.BlockSpec((1, DIFF_DH), const2),
                  pl.BlockSpec((1, HEAD_W), const2)],
        out_specs=[pl.BlockSpec((1, tq, gw), lambda b, h, i: (b, i, h)),
                   pl.BlockSpec((1, tq, gw), lambda b, h, i: (b, half - 1 - i, h))],
        out_shape=[out, out],
        scratch_shapes=[pltpu.VMEM((hps, 2, 2 * tq, HEAD_W), BF16),
                        pltpu.VMEM((hps, 2, 2 * tq, LANES), F32),
                        pltpu.VMEM((hps, 2, 2 * tq, 2 * HEAD_W), F32)],
        compiler_params=pltpu.CompilerParams(
            dimension_semantics=("arbitrary", "arbitrary", "arbitrary"), vmem_limit_bytes=VMEM_LIMIT),
        name="diffattn",
    )(qk, qk, qk, dv, lq1, lk1, lq2, lk2, nw)
    return lo, hi


def _gla_tables():
    C = GLA_CHUNK
    t = np.arange(C)
    rng = np.zeros((2 + GLA_LEVELS, C, C), np.float32)
    rng[0] = (t[None, :] <= t[:, None])
    rng[1] = (t[None, :] > t[:, None])
    lvl_mask = np.zeros((GLA_LEVELS + 1, C, C), np.float32)
    for l in range(GLA_LEVELS):
        s = C >> (l + 1)
        blk = t // (2 * s)
        mid = blk * 2 * s + s
        upper = (t % (2 * s)) >= s
        for i in range(C):
            if upper[i]:
                rng[2 + l, i, mid[i]:i + 1] = 1.0
            else:
                rng[2 + l, i, i + 1:mid[i]] = 1.0
        lvl_mask[l] = (blk[:, None] == blk[None, :]) & upper[:, None] & (~upper[None, :])
    lvl_mask[GLA_LEVELS] = np.eye(C)
    rng = rng.reshape((2 + GLA_LEVELS) * C, C)
    rng3 = np.concatenate([rng, rng, rng], axis=1)
    lvl_mask = np.tile(lvl_mask, (1, 1, GLA_HEADS))
    hk = np.kron(np.eye(GLA_HEADS), np.ones((C, GLA_DK)))
    hv = np.kron(np.eye(GLA_HEADS), np.ones((C, GLA_DV)))
    hs = np.kron(np.eye(GLA_HEADS), np.ones((GLA_DV, GLA_DK)))
    return (jnp.asarray(rng3, BF16), jnp.asarray(lvl_mask, F32), jnp.asarray(hk, BF16),
            jnp.asarray(hv, BF16), jnp.asarray(hs, F32))


def _gla_kernel(q_ref, k_ref, v_ref, g_ref, r_ref, rng_ref, lm_ref, tril_ref, hk_ref, hv_ref, hs_ref, nw_ref,
                o_ref, st_ref, *, ts):
    C = GLA_CHUNK

    @pl.when(pl.program_id(1) == 0)
    def _():
        st_ref[...] = jnp.zeros(st_ref.shape, F32)

    def gate_sums(c, mild):
        g = g_ref[0, pl.ds(c * C, C), :]
        g_hi = g.astype(BF16)
        rem = g - g_hi.astype(F32)
        g_mid = rem.astype(BF16)
        g_lo = (rem - g_mid.astype(F32)).astype(BF16)
        g3 = jnp.concatenate([g_hi, g_mid, g_lo], axis=0)
        rng = rng_ref[0:C, :] if mild else rng_ref[...]
        return jnp.dot(rng, g3, preferred_element_type=F32)

    def block_diag(kl):
        return jnp.concatenate([kl] * GLA_HEADS, axis=0) * hk_ref[...]

    def decays_mild(c, b):
        rows = pl.ds(c * C, C)
        q = q_ref[0, rows, :].astype(F32)
        k = k_ref[0, rows, :].astype(F32)
        b_last = b[C - 1:C]
        ref = b[C // 2 - 1:C // 2]
        levels = [((q * jnp.exp(b - ref)).astype(BF16), block_diag((k * jnp.exp(ref - b)).astype(BF16)))]
        return dict(q_in=(q * jnp.exp(b)).astype(BF16),
                    k_out=(k * jnp.exp(b_last - b)).astype(BF16),
                    decay=jnp.exp(b_last),
                    levels=levels)

    def decays(c, e):
        rows = pl.ds(c * C, C)
        q = q_ref[0, rows, :].astype(F32)
        k = k_ref[0, rows, :].astype(F32)
        f = jnp.exp(e)
        levels = []
        for l in range(GLA_LEVELS + 1):
            if l < GLA_LEVELS:
                fl = f[(2 + l) * C:(3 + l) * C]
                ql = (q * fl).astype(BF16)
                kl = (k * fl).astype(BF16)
            else:
                ql = q.astype(BF16)
                kl = k.astype(BF16)
            levels.append((ql, block_diag(kl)))
        return dict(q_in=(q * f[0:C]).astype(BF16),
                    k_out=(k * f[C:2 * C]).astype(BF16),
                    decay=f[C - 1:C],
                    levels=levels)

    def intra(d, mild):
        if mild:
            ql, k_bd = d["levels"][0]
            a = lax.dot_general(ql, k_bd, (((1,), (1,)), ((), ())), preferred_element_type=F32)
            return jnp.where(tril_ref[...] > 0.5, a, 0.0).astype(BF16)
        attn = jnp.zeros((C, GLA_HEADS * C), F32)
        for l, (ql, k_bd) in enumerate(d["levels"]):
            a = lax.dot_general(ql, k_bd, (((1,), (1,)), ((), ())), preferred_element_type=F32)
            attn = attn + a * lm_ref[l]
        return attn.astype(BF16)

    def output(c, d, attn):
        rows = pl.ds(c * C, C)
        v = v_ref[0, rows, :]
        v_bd = jnp.concatenate([v] * GLA_HEADS, axis=0) * hv_ref[...]
        st = st_ref[...]
        o = jnp.dot(attn, v_bd, preferred_element_type=F32)
        o = o + lax.dot_general(d["q_in"], st.astype(BF16), (((1,), (1,)), ((), ())), preferred_element_type=F32)
        upd = lax.dot_general(v, d["k_out"], (((0,), (0,)), ((), ())), preferred_element_type=F32)
        st_ref[...] = st * d["decay"] + upd * hs_ref[...]
        parts = []
        for h in range(GLA_HEADS):
            oh = o[:, h * GLA_DV:(h + 1) * GLA_DV]
            ms = jnp.mean(oh * oh, axis=1, keepdims=True)
            parts.append(oh * lax.rsqrt(ms + LN_EPS) * nw_ref[...])
        on = jnp.concatenate(parts, axis=1)
        o_ref[0, rows, :] = (on * _silu(r_ref[0, rows, :].astype(F32))).astype(BF16)

    n = ts // C

    def run(mild):
        prep = decays_mild if mild else decays
        esum, dec, att = {0: gate_sums(0, mild)}, {}, {}
        for t in range(n + 2):
            if 0 <= t - 1 < n:
                att[t - 1] = intra(dec[t - 1], mild)
            if 0 <= t - 2 < n:
                output(t - 2, dec.pop(t - 2), att.pop(t - 2))
            if t + 1 < n:
                esum[t + 1] = gate_sums(t + 1, mild)
            if t < n:
                dec[t] = prep(t, esum.pop(t))

    totals = [jnp.sum(g_ref[0, pl.ds(c * C, C), :], axis=0, keepdims=True) for c in range(n)]
    mild = jnp.min(jnp.concatenate(totals, axis=0)) >= -GLA_MILD_DECAY

    @pl.when(mild)
    def _():
        run(True)

    @pl.when(jnp.logical_not(mild))
    def _():
        run(False)


def _gla(gq, gk, gv, gl, gr, nw, ts):
    B, S, _ = gq.shape
    rng3, lvl_mask, hk, hv, hs = _gla_tables()
    tril = jnp.sum(lvl_mask, axis=0)
    row = lambda b, s: (b, s, 0)
    const2 = lambda b, s: (0, 0)
    const3 = lambda b, s: (0, 0, 0)
    return pl.pallas_call(
        functools.partial(_gla_kernel, ts=ts),
        grid=(B, S // ts),
        in_specs=[pl.BlockSpec((1, ts, GLA_KW), row),
                  pl.BlockSpec((1, ts, GLA_KW), row),
                  pl.BlockSpec((1, ts, GLA_VW), row),
                  pl.BlockSpec((1, ts, GLA_KW), row),
                  pl.BlockSpec((1, ts, GLA_VW), row),
                  pl.BlockSpec(rng3.shape, const2),
                  pl.BlockSpec(lvl_mask.shape, const3),
                  pl.BlockSpec(tril.shape, const2),
                  pl.BlockSpec(hk.shape, const2),
                  pl.BlockSpec(hv.shape, const2),
                  pl.BlockSpec(hs.shape, const2),
                  pl.BlockSpec((1, GLA_DV), const2)],
        out_specs=pl.BlockSpec((1, ts, GLA_VW), row),
        out_shape=jax.ShapeDtypeStruct((B, S, GLA_VW), BF16),
        scratch_shapes=[pltpu.VMEM((GLA_VW, GLA_KW), F32)],
        compiler_params=pltpu.CompilerParams(
            dimension_semantics=("arbitrary", "arbitrary"), vmem_limit_bytes=VMEM_LIMIT),
        name="gla",
    )(gq, gk, gv, gl, gr, rng3, lvl_mask, tril, hk, hv, hs, nw)


def _mlp_kernel(x_ref, dlo_ref, dhi_ref, g_ref, lg_ref, lb_ref, gta_ref, wod_ref, wog_ref, ag_ref, ab_ref,
                sc_ref, sh_ref, gtf_ref, wu_ref, cw_ref, cb_ref, wd_ref, fg_ref, fb_ref, o_ref,
                carry_ref, ubuf0_ref, ubuf1_ref, act_ref, u_ref, h_ref, *, tm, tf, nf, half_tiles):
    HALO = 8
    F = nf * tf
    ubufs = (ubuf0_ref, ubuf1_ref)
    s = pl.program_id(1)

    @pl.when(s == 0)
    def _():
        carry_ref[...] = jnp.zeros(carry_ref.shape, F32)

    d_out = jnp.where(s < half_tiles, dlo_ref[0], dhi_ref[0])
    hm = tm // 2
    for r in range(2):
        rows = slice(r * hm, (r + 1) * hm)
        h_in = _layer_norm(x_ref[0, rows, :], lg_ref[...], lb_ref[...])
        mix = (jnp.dot(d_out[rows], wod_ref[...], preferred_element_type=F32)
               + jnp.dot(g_ref[0, rows, :], wog_ref[...], preferred_element_type=F32))
        h = _layer_norm(DN_ALPHA * h_in + (1.0 + gta_ref[0]) * mix, ag_ref[...], ab_ref[...])
        h_ref[rows, :] = h
        u_ref[rows, :] = (h * (1.0 + sc_ref[0]) + sh_ref[0]).astype(BF16)

    def up_proj(f, slot):
        for half in range(2):
            cols = slice(half * F + f * tf, half * F + (f + 1) * tf)
            up = jnp.dot(u_ref[...], wu_ref[:, cols], preferred_element_type=F32)
            ubufs[slot][half, 0:HALO, :] = carry_ref[half, f]
            ubufs[slot][half, HALO:HALO + tm, :] = up
            carry_ref[half, f] = up[tm - HALO:tm, :]

    def gate(f, slot):
        def conv(half):
            cols = slice(half * F + f * tf, half * F + (f + 1) * tf)
            cw = cw_ref[:, cols]
            buf = ubufs[slot]
            return (cb_ref[:, cols]
                    + cw[0:1] * buf[half, HALO - 2:HALO - 2 + tm, :]
                    + cw[1:2] * buf[half, HALO - 1:HALO - 1 + tm, :]
                    + cw[2:3] * buf[half, HALO:HALO + tm, :])
        act_ref[f] = (_silu(conv(0)) * conv(1)).astype(BF16)

    up_proj(0, 0)
    for f in range(nf):
        if f + 1 < nf:
            up_proj(f + 1, (f + 1) % 2)
        gate(f, f % 2)

    for r in range(2):
        rows = slice(r * hm, (r + 1) * hm)
        ff = jnp.dot(act_ref[0, rows, :], wd_ref[0], preferred_element_type=F32)
        for f in range(1, nf):
            ff = ff + jnp.dot(act_ref[f, rows, :], wd_ref[f], preferred_element_type=F32)
        y = DN_ALPHA * h_ref[rows, :] + (1.0 + gtf_ref[0]) * ff
        o_ref[0, rows, :] = _layer_norm(y, fg_ref[...], fb_ref[...])


def _mlp(x, d_lo, d_hi, g_out, ln_g, ln_b, gt_a, w_od, w_og, ag, ab, sc, sh, gt_f, wu, cw, cb, wd, fg, fb, tm):
    B, S, D = x.shape
    nf, tf, _ = wd.shape
    half_tiles = d_lo.shape[1] // tm
    row = lambda b, s: (b, s, 0)
    per_b = lambda b, s: (b, 0, 0)

    def resident(shape):
        return pl.BlockSpec(shape, lambda b, s: (0,) * len(shape), pipeline_mode=pl.Buffered(1))

    return pl.pallas_call(
        functools.partial(_mlp_kernel, tm=tm, tf=tf, nf=nf, half_tiles=half_tiles),
        grid=(B, S // tm),
        in_specs=[pl.BlockSpec((1, tm, D), row),
                  pl.BlockSpec((1, tm, d_lo.shape[2]), lambda b, s: (b, jnp.minimum(s, half_tiles - 1), 0)),
                  pl.BlockSpec((1, tm, d_hi.shape[2]), lambda b, s: (b, jnp.maximum(s - half_tiles, 0), 0)),
                  pl.BlockSpec((1, tm, g_out.shape[2]), row),
                  resident((1, D)), resident((1, D)),
                  pl.BlockSpec((1, 1, D), per_b),
                  resident(w_od.shape), resident(w_og.shape),
                  resident((1, D)), resident((1, D)),
                  pl.BlockSpec((1, 1, D), per_b),
                  pl.BlockSpec((1, 1, D), per_b),
                  pl.BlockSpec((1, 1, D), per_b),
                  resident(wu.shape), resident(cw.shape), resident(cb.shape), resident(wd.shape),
                  resident((1, D)), resident((1, D))],
        out_specs=pl.BlockSpec((1, tm, D), row),
        out_shape=jax.ShapeDtypeStruct((B, S, D), F32),
        scratch_shapes=[pltpu.VMEM((2, nf, 8, tf), F32),
                        pltpu.VMEM((2, 8 + tm, tf), F32),
                        pltpu.VMEM((2, 8 + tm, tf), F32),
                        pltpu.VMEM((nf, tm, tf), BF16),
                        pltpu.VMEM((tm, D), BF16),
                        pltpu.VMEM((tm, D), F32)],
        compiler_params=pltpu.CompilerParams(
            dimension_semantics=("arbitrary", "arbitrary"), vmem_limit_bytes=VMEM_LIMIT),
        name="mlp",
    )(x, d_lo, d_hi, g_out, ln_g, ln_b, gt_a, w_od, w_og, ag, ab, sc, sh, gt_f, wu, cw, cb, wd, fg, fb)


def kernel(x, c, positions, ln_in_g, ln_in_b, w_ada, b_ada, w_in, lambda_q1, lambda_k1, lambda_q2, lambda_k2, diff_norm_w, gla_w_gate_up, gla_b_gate, gla_norm_w, w_out, ln_attn_g, ln_attn_b, w_up, conv_w, conv_b, w_down, ln_ffn_g, ln_ffn_b):
    B, S, D = x.shape
    assert D == D_MODEL and w_ada.shape[0] == 1
    tm = min(512, S)
    tf = 256
    nf = D_FF // tf

    c_pad = jnp.pad(c, ((0, 8 - B % 8 if B % 8 else 0), (0, 0)))
    ada = _ada(c_pad, w_ada[0], b_ada)[:B]
    sh_a, sc_a, gt_a, sh_f, sc_f, gt_f = [t[:, None, :] for t in jnp.split(ada, 6, axis=-1)]

    ln_g = ln_in_g[None, :]
    ln_b = ln_in_b[None, :]

    w_main = w_in[0, :, :N_MAIN]
    w_gg = jnp.pad(w_in[0, :, N_MAIN:], ((0, 0), (0, N_PROJ - w_in.shape[2])))
    w_proj = jnp.concatenate([w_main, w_gg], axis=1).astype(BF16)
    w_gate = jnp.pad(gla_w_gate_up[0], ((0, LANES - GLA_RANK), (0, 0))).astype(BF16)
    inv = ROPE_THETA ** (-jnp.arange(0, DIFF_DH, 2, dtype=F32) / DIFF_DH)
    qk, dv, gq, gk, gv, gr, gl = _proj(x, positions[:, None, :], inv[:, None], ln_g, ln_b, sc_a, sh_a,
                                       w_proj, w_gate, gla_b_gate, tm)

    d_lo, d_hi = _diffattn(qk, dv, lambda_q1, lambda_k1, lambda_q2, lambda_k2, diff_norm_w, tm, 2)
    g_out = _gla(gq, gk, gv, gl, gr, gla_norm_w, min(512, S))

    w_o = w_out[0].astype(BF16)
    dw = d_lo.shape[2]
    wd = w_down[0].astype(BF16).reshape(nf, tf, D)
    return _mlp(x, d_lo, d_hi, g_out, ln_g, ln_b, gt_a, w_o[:dw], w_o[dw:], ln_attn_g, ln_attn_b,
                sc_f, sh_f, gt_f, w_up[0].astype(BF16), conv_w[0], conv_b, wd, ln_ffn_g, ln_ffn_b, tm)
```

```python
import functools
import math

import numpy as np
import jax
import jax.numpy as jnp
from jax import lax
from jax.experimental import pallas as pl
from jax.experimental.pallas import tpu as pltpu

F32 = jnp.float32
BF16 = jnp.bfloat16

D_MODEL = 1024
DIFF_DH = 64
DIFF_HEADS = 4
HEAD_W = 2 * DIFF_DH
GLA_HEADS = 4
GLA_DK = 64
GLA_DV = 128
GLA_KW = GLA_HEADS * GLA_DK
GLA_VW = GLA_HEADS * GLA_DV
GLA_RANK = 16
GLA_TAU = 16.0
GLA_CHUNK = 64
GLA_LEVELS = 6
GLA_MILD_DECAY = 30.0
D_FF = 2816
CONV_W = 3
ROPE_THETA = 10000.0
LN_EPS = 1e-5
DEPTH = 1
DN_ALPHA = (2.0 * DEPTH) ** 0.25
LAMBDA_INIT = 0.8 - 0.6 * math.exp(-0.3 * 0)

N_MAIN = 3072
N_PROJ = N_MAIN + 128
LANES = 128
NEG_BIG = -1e30
LOG2_E = math.log2(math.e)

VMEM_LIMIT = 56 * 1024 * 1024


def _layer_norm(x, g, b):
    mu = jnp.mean(x, axis=-1, keepdims=True)
    xc = x - mu
    var = jnp.mean(xc * xc, axis=-1, keepdims=True)
    return xc * lax.rsqrt(var + LN_EPS) * g + b


def _silu(x):
    return x * jax.nn.sigmoid(x)


def _ada_kernel(c_ref, w_ref, b_ref, o_ref):
    ca = _silu(c_ref[...])
    o_ref[...] = jnp.dot(ca.astype(BF16), w_ref[...].astype(BF16),
                         preferred_element_type=F32) + b_ref[...]


def _ada(c_pad, w_ada, b_ada):
    rows, d = c_pad.shape
    n = w_ada.shape[1]
    tn = 1024
    return pl.pallas_call(
        _ada_kernel,
        grid=(n // tn,),
        in_specs=[pl.BlockSpec((rows, d), lambda j: (0, 0)),
                  pl.BlockSpec((d, tn), lambda j: (0, j)),
                  pl.BlockSpec((1, tn), lambda j: (0, j))],
        out_specs=pl.BlockSpec((rows, tn), lambda j: (0, j)),
        out_shape=jax.ShapeDtypeStruct((rows, n), F32),
        name="ada",
    )(c_pad, w_ada, b_ada)


def _proj_kernel(x_ref, pos_ref, inv_ref, lg_ref, lb_ref, sc_ref, sh_ref, w_ref, wg_ref, bg_ref,
                 qk_ref, dv_ref, gq_ref, gk_ref, gv_ref, gr_ref, gl_ref, *, tm):
    ang_t = inv_ref[...] * pos_ref[0].astype(F32)
    reps = LANES // ang_t.shape[0]
    cos = jnp.concatenate([jnp.cos(ang_t)] * reps, axis=0).T
    sin = jnp.concatenate([jnp.sin(ang_t)] * reps, axis=0).T
    lane = lax.broadcasted_iota(jnp.int32, (1, LANES), 1)
    first_half = (lane & 32) == 0
    sin_signed = jnp.where(first_half, -sin, sin)
    scale = LOG2_E / math.sqrt(DIFF_DH)

    hm = tm // 2
    for part in range(2):
        rows = slice(part * hm, (part + 1) * hm)
        h = _layer_norm(x_ref[0, rows, :], lg_ref[...], lb_ref[...])
        u = (h * (1.0 + sc_ref[0]) + sh_ref[0]).astype(BF16)
        proj = jnp.dot(u, w_ref[...], preferred_element_type=F32)
        for j in range(8):
            xg = proj[:, j * LANES:(j + 1) * LANES]
            partner = jnp.where(first_half, pltpu.roll(xg, LANES - 32, axis=1), pltpu.roll(xg, 32, axis=1))
            r = xg * cos[rows] + partner * sin_signed[rows]
            if j < 4:
                r = r * scale
            qk_ref[0, rows, j * LANES:(j + 1) * LANES] = r.astype(BF16)
        dv_ref[0, rows, :] = proj[:, 1024:1536].astype(BF16)
        gq_ref[0, rows, :] = (proj[:, 1536:1792] * (GLA_DK ** -0.5)).astype(BF16)
        gk_ref[0, rows, :] = proj[:, 1792:2048].astype(BF16)
        gv_ref[0, rows, :] = proj[:, 2048:2560].astype(BF16)
        gr_ref[0, rows, :] = proj[:, 2560:3072].astype(BF16)
        gg = proj[:, N_MAIN:N_PROJ].astype(BF16)
        z = jnp.dot(gg, wg_ref[...], preferred_element_type=F32) + bg_ref[...]
        log_sig = jnp.minimum(z, 0.0) - jnp.log1p(jnp.exp(-jnp.abs(z)))
        gl_ref[0, rows, :] = log_sig * (1.0 / GLA_TAU)


def _proj(x, pos_row, inv_col, ln_g, ln_b, sc, sh, w, wg, bg, tm):
    B, S, D = x.shape
    grid = (B, S // tm)
    row = lambda b, s: (b, s, 0)
    const2 = lambda b, s: (0, 0)
    per_b = lambda b, s: (b, 0, 0)
    widths = (1024, 512, GLA_KW, GLA_KW, GLA_VW, GLA_VW)
    out_shape = [jax.ShapeDtypeStruct((B, S, n), BF16) for n in widths]
    out_shape.append(jax.ShapeDtypeStruct((B, S, GLA_KW), F32))
    out_specs = [pl.BlockSpec((1, tm, n), row) for n in widths]
    out_specs.append(pl.BlockSpec((1, tm, GLA_KW), row))
    return pl.pallas_call(
        functools.partial(_proj_kernel, tm=tm),
        grid=grid,
        in_specs=[pl.BlockSpec((1, tm, D), row),
                  pl.BlockSpec((1, 1, tm), lambda b, s: (b, 0, s)),
                  pl.BlockSpec(inv_col.shape, const2),
                  pl.BlockSpec((1, D), const2),
                  pl.BlockSpec((1, D), const2),
                  pl.BlockSpec((1, 1, D), per_b),
                  pl.BlockSpec((1, 1, D), per_b),
                  pl.BlockSpec((D, N_PROJ), const2),
                  pl.BlockSpec((LANES, GLA_KW), const2),
                  pl.BlockSpec((1, GLA_KW), const2)],
        out_specs=out_specs,
        out_shape=out_shape,
        compiler_params=pltpu.CompilerParams(
            dimension_semantics=("arbitrary", "arbitrary"), vmem_limit_bytes=VMEM_LIMIT),
        name="proj",
    )(x, pos_row, inv_col, ln_g, ln_b, sc, sh, w, wg, bg)


def _diffattn_kernel(qa_ref, qb_ref, k_ref, v_ref, lq1_ref, lk1_ref, lq2_ref, lk2_ref, nw_ref,
                     oa_ref, ob_ref, qz_ref, m_ref, acc_ref, *, tq, nt, hps):
    i = pl.program_id(2)
    lane = lax.broadcasted_iota(jnp.int32, (1, HEAD_W), 1)
    for hh in range(hps):
        cols = slice(hh * HEAD_W, (hh + 1) * HEAD_W)
        for t, q_ref in enumerate((qa_ref, qb_ref)):
            q = q_ref[0, :, cols]
            zero = jnp.zeros_like(q)
            qz_ref[hh, t] = jnp.concatenate([jnp.where(lane < DIFF_DH, q, zero),
                                             jnp.where(lane >= DIFF_DH, q, zero)], axis=0)
    m_ref[...] = jnp.full(m_ref.shape, NEG_BIG, F32)
    acc_ref[...] = jnp.zeros(acc_ref.shape, F32)
    ones = jnp.ones((tq, LANES), BF16)

    def block_step(hh, own, blk, masked):
        cols = slice(hh * HEAD_W, (hh + 1) * HEAD_W)
        rows = pl.ds(pl.multiple_of(blk * tq, tq), tq)
        kb = k_ref[0, rows, cols]
        vb = jnp.concatenate([v_ref[0, rows, cols], ones], axis=1)
        s = lax.dot_general(qz_ref[hh, own], kb, (((1,), (1,)), ((), ())), preferred_element_type=F32)
        if masked:
            r = lax.broadcasted_iota(jnp.int32, (2 * tq, tq), 0)
            c = lax.broadcasted_iota(jnp.int32, (2 * tq, tq), 1)
            r = jnp.where(r >= tq, r - tq, r)
            s = jnp.where(c <= r, s, NEG_BIG)
        s_fold = s[:, 0:LANES]
        for t in range(1, tq // LANES):
            s_fold = jnp.maximum(s_fold, s[:, t * LANES:(t + 1) * LANES])
        m_prev = m_ref[hh, own]
        m_new = jnp.maximum(m_prev, jnp.max(s_fold, axis=1, keepdims=True))
        alpha = jnp.exp2(m_prev - m_new)
        p = jnp.exp2(s - jnp.concatenate([m_new] * (tq // LANES), axis=1))
        acc_ref[hh, own] = (jnp.concatenate([alpha, alpha], axis=1) * acc_ref[hh, own]
                            + jnp.dot(p.astype(BF16), vb, preferred_element_type=F32))
        m_ref[hh, own] = m_new

    for hh in range(hps):
        block_step(hh, 0, i, True)
    for t in range(1, nt):
        is_a = t <= i
        for hh in range(hps):
            block_step(hh, jnp.where(is_a, 0, 1), jnp.where(is_a, i - t, t - i - 1), False)
    for hh in range(hps):
        block_step(hh, 1, nt - 1 - i, True)

    lam = (jnp.exp(jnp.sum(lq1_ref[...] * lk1_ref[...], axis=1, keepdims=True))
           - jnp.exp(jnp.sum(lq2_ref[...] * lk2_ref[...], axis=1, keepdims=True)) + LAMBDA_INIT)
    for hh in range(hps):
        cols = slice(hh * HEAD_W, (hh + 1) * HEAD_W)
        for t, o_ref in enumerate((oa_ref, ob_ref)):
            acc = acc_ref[hh, t]
            o = acc[:, 0:HEAD_W] / acc[:, HEAD_W:2 * HEAD_W]
            d = o[0:tq] - lam * o[tq:2 * tq]
            ms = jnp.mean(d * d, axis=1, keepdims=True)
            o_ref[0, :, cols] = (d * lax.rsqrt(ms + LN_EPS) * nw_ref[...] * (1.0 - LAMBDA_INIT)).astype(BF16)


def _diffattn(qk, dv, lq1, lk1, lq2, lk2, nw, tq, hps):
    B, S, _ = qk.shape
    nt = S // tq
    assert nt % 2 == 0, "query tiles are processed in pairs (i, nt-1-i)"
    assert DIFF_HEADS % hps == 0
    half = nt // 2
    groups = DIFF_HEADS // hps
    gw = hps * HEAD_W
    const2 = lambda b, h, i: (0, 0)
    out = jax.ShapeDtypeStruct((B, S // 2, DIFF_HEADS * HEAD_W), BF16)
    lo, hi = pl.pallas_call(
        functools.partial(_diffattn_kernel, tq=tq, nt=nt, hps=hps),
        grid=(B, groups, half),
        in_specs=[pl.BlockSpec((1, tq, gw), lambda b, h, i: (b, i, h)),
                  pl.BlockSpec((1, tq, gw), lambda b, h, i: (b, nt - 1 - i, h)),
                  pl.BlockSpec((1, S, gw), lambda b, h, i: (b, 0, groups + h)),
                  pl.BlockSpec((1, S, gw), lambda b, h, i: (b, 0, h)),
                  pl.BlockSpec((1, DIFF_DH), const2),
                  pl.BlockSpec((1, DIFF_DH), const2),
                  pl.BlockSpec((1, DIFF_DH), const2),
                  pl.BlockSpec((1, DIFF_DH), const2),
                  pl.BlockSpec((1, HEAD_W), const2)],
        out_specs=[pl.BlockSpec((1, tq, gw), lambda b, h, i: (b, i, h)),
                   pl.BlockSpec((1, tq, gw), lambda b, h, i: (b, half - 1 - i, h))],
        out_shape=[out, out],
        scratch_shapes=[pltpu.VMEM((hps, 2, 2 * tq, HEAD_W), BF16),
                        pltpu.VMEM((hps, 2, 2 * tq, LANES), F32),
                        pltpu.VMEM((hps, 2, 2 * tq, 2 * HEAD_W), F32)],
        compiler_params=pltpu.CompilerParams(
            dimension_semantics=("arbitrary", "arbitrary", "arbitrary"), vmem_limit_bytes=VMEM_LIMIT),
        name="diffattn",
    )(qk, qk, qk, dv, lq1, lk1, lq2, lk2, nw)
    return lo, hi


def _gla_tables():
    C = GLA_CHUNK
    t = np.arange(C)
    rng = np.zeros((2 + GLA_LEVELS, C, C), np.float32)
    rng[0] = (t[None, :] <= t[:, None])
    rng[1] = (t[None, :] > t[:, None])
    lvl_mask = np.zeros((GLA_LEVELS + 1, C, C), np.float32)
    for l in range(GLA_LEVELS):
        s = C >> (l + 1)
        blk = t // (2 * s)
        mid = blk * 2 * s + s
        upper = (t % (2 * s)) >= s
        for i in range(C):
            if upper[i]:
                rng[2 + l, i, mid[i]:i + 1] = 1.0
            else:
                rng[2 + l, i, i + 1:mid[i]] = 1.0
        lvl_mask[l] = (blk[:, None] == blk[None, :]) & upper[:, None] & (~upper[None, :])
    lvl_mask[GLA_LEVELS] = np.eye(C)
    rng = rng.reshape((2 + GLA_LEVELS) * C, C)
    rng3 = np.concatenate([rng, rng, rng], axis=1)
    lvl_mask = np.tile(lvl_mask, (1, 1, GLA_HEADS))
    hk = np.kron(np.eye(GLA_HEADS), np.ones((C, GLA_DK)))
    hv = np.kron(np.eye(GLA_HEADS), np.ones((C, GLA_DV)))
    hs = np.kron(np.eye(GLA_HEADS), np.ones((GLA_DV, GLA_DK)))
    return (jnp.asarray(rng3, BF16), jnp.asarray(lvl_mask, F32), jnp.asarray(hk, BF16),
            jnp.asarray(hv, BF16), jnp.asarray(hs, F32))


def _gla_kernel(q_ref, k_ref, v_ref, g_ref, r_ref, rng_ref, lm_ref, tril_ref, hk_ref, hv_ref, hs_ref, nw_ref,
                o_ref, st_ref, *, ts):
    C = GLA_CHUNK

    @pl.when(pl.program_id(1) == 0)
    def _():
        st_ref[...] = jnp.zeros(st_ref.shape, F32)

    def gate_sums(c, mild):
        g = g_ref[0, pl.ds(c * C, C), :]
        g_hi = g.astype(BF16)
        rem = g - g_hi.astype(F32)
        g_mid = rem.astype(BF16)
        g_lo = (rem - g_mid.astype(F32)).astype(BF16)
        g3 = jnp.concatenate([g_hi, g_mid, g_lo], axis=0)
        rng = rng_ref[0:C, :] if mild else rng_ref[...]
        return jnp.dot(rng, g3, preferred_element_type=F32)

    def block_diag(kl):
        return jnp.concatenate([kl] * GLA_HEADS, axis=0) * hk_ref[...]

    def decays_mild(c, b):
        rows = pl.ds(c * C, C)
        q = q_ref[0, rows, :].astype(F32)
        k = k_ref[0, rows, :].astype(F32)
        b_last = b[C - 1:C]
        ref = b[C // 2 - 1:C // 2]
        levels = [((q * jnp.exp(b - ref)).astype(BF16), block_diag((k * jnp.exp(ref - b)).astype(BF16)))]
        return dict(q_in=(q * jnp.exp(b)).astype(BF16),
                    k_out=(k * jnp.exp(b_last - b)).astype(BF16),
                    decay=jnp.exp(b_last),
                    levels=levels)

    def decays(c, e):
        rows = pl.ds(c * C, C)
        q = q_ref[0, rows, :].astype(F32)
        k = k_ref[0, rows, :].astype(F32)
        f = jnp.exp(e)
        levels = []
        for l in range(GLA_LEVELS + 1):
            if l < GLA_LEVELS:
                fl = f[(2 + l) * C:(3 + l) * C]
                ql = (q * fl).astype(BF16)
                kl = (k * fl).astype(BF16)
            else:
                ql = q.astype(BF16)
                kl = k.astype(BF16)
            levels.append((ql, block_diag(kl)))
        return dict(q_in=(q * f[0:C]).astype(BF16),
                    k_out=(k * f[C:2 * C]).astype(BF16),
                    decay=f[C - 1:C],
                    levels=levels)

    def intra(d, mild):
        if mild:
            ql, k_bd = d["levels"][0]
            a = lax.dot_general(ql, k_bd, (((1,), (1,)), ((), ())), preferred_element_type=F32)
            return jnp.where(tril_ref[...] > 0.5, a, 0.0).astype(BF16)
        attn = jnp.zeros((C, GLA_HEADS * C), F32)
        for l, (ql, k_bd) in enumerate(d["levels"]):
            a = lax.dot_general(ql, k_bd, (((1,), (1,)), ((), ())), preferred_element_type=F32)
            attn = attn + a * lm_ref[l]
        return attn.astype(BF16)

    def output(c, d, attn):
        rows = pl.ds(c * C, C)
        v = v_ref[0, rows, :]
        v_bd = jnp.concatenate([v] * GLA_HEADS, axis=0) * hv_ref[...]
        st = st_ref[...]
        o = jnp.dot(attn, v_bd, preferred_element_type=F32)
        o = o + lax.dot_general(d["q_in"], st.astype(BF16), (((1,), (1,)), ((), ())), preferred_element_type=F32)
        upd = lax.dot_general(v, d["k_out"], (((0,), (0,)), ((), ())), preferred_element_type=F32)
        st_ref[...] = st * d["decay"] + upd * hs_ref[...]
        parts = []
        for h in range(GLA_HEADS):
            oh = o[:, h * GLA_DV:(h + 1) * GLA_DV]
            ms = jnp.mean(oh * oh, axis=1, keepdims=True)
            parts.append(oh * lax.rsqrt(ms + LN_EPS) * nw_ref[...])
        on = jnp.concatenate(parts, axis=1)
        o_ref[0, rows, :] = (on * _silu(r_ref[0, rows, :].astype(F32))).astype(BF16)

    n = ts // C

    def run(mild):
        prep = decays_mild if mild else decays
        esum, dec, att = {0: gate_sums(0, mild)}, {}, {}
        for t in range(n + 2):
            if 0 <= t - 1 < n:
                att[t - 1] = intra(dec[t - 1], mild)
            if 0 <= t - 2 < n:
                output(t - 2, dec.pop(t - 2), att.pop(t - 2))
            if t + 1 < n:
                esum[t + 1] = gate_sums(t + 1, mild)
            if t < n:
                dec[t] = prep(t, esum.pop(t))

    totals = [jnp.sum(g_ref[0, pl.ds(c * C, C), :], axis=0, keepdims=True) for c in range(n)]
    mild = jnp.min(jnp.concatenate(totals, axis=0)) >= -GLA_MILD_DECAY

    @pl.when(mild)
    def _():
        run(True)

    @pl.when(jnp.logical_not(mild))
    def _():
        run(False)


def _gla(gq, gk, gv, gl, gr, nw, ts):
    B, S, _ = gq.shape
    rng3, lvl_mask, hk, hv, hs = _gla_tables()
    tril = jnp.sum(lvl_mask, axis=0)
    row = lambda b, s: (b, s, 0)
    const2 = lambda b, s: (0, 0)
    const3 = lambda b, s: (0, 0, 0)
    return pl.pallas_call(
        functools.partial(_gla_kernel, ts=ts),
        grid=(B, S // ts),
        in_specs=[pl.BlockSpec((1, ts, GLA_KW), row),
                  pl.BlockSpec((1, ts, GLA_KW), row),
                  pl.BlockSpec((1, ts, GLA_VW), row),
                  pl.BlockSpec((1, ts, GLA_KW), row),
                  pl.BlockSpec((1, ts, GLA_VW), row),
                  pl.BlockSpec(rng3.shape, const2),
                  pl.BlockSpec(lvl_mask.shape, const3),
                  pl.BlockSpec(tril.shape, const2),
                  pl.BlockSpec(hk.shape, const2),
                  pl.BlockSpec(hv.shape, const2),
                  pl.BlockSpec(hs.shape, const2),
                  pl.BlockSpec((1, GLA_DV), const2)],
        out_specs=pl.BlockSpec((1, ts, GLA_VW), row),
        out_shape=jax.ShapeDtypeStruct((B, S, GLA_VW), BF16),
        scratch_shapes=[pltpu.VMEM((GLA_VW, GLA_KW), F32)],
        compiler_params=pltpu.CompilerParams(
            dimension_semantics=("arbitrary", "arbitrary"), vmem_limit_bytes=VMEM_LIMIT),
        name="gla",
    )(gq, gk, gv, gl, gr, rng3, lvl_mask, tril, hk, hv, hs, nw)


def _mlp_kernel(x_ref, dlo_ref, dhi_ref, g_ref, lg_ref, lb_ref, gta_ref, wod_ref, wog_ref, ag_ref, ab_ref,
                sc_ref, sh_ref, gtf_ref, wu_ref, cw_ref, cb_ref, wd_ref, fg_ref, fb_ref, o_ref,
                carry_ref, act_ref, u_ref, h_ref, *, tm, tf, nf, half_tiles):
    HALO = 8
    F = nf * tf
    s = pl.program_id(1)

    @pl.when(s == 0)
    def _():
        carry_ref[...] = jnp.zeros(carry_ref.shape, F32)

    d_out = jnp.where(s < half_tiles, dlo_ref[0], dhi_ref[0])
    hm = tm // 2
    for r in range(2):
        rows = slice(r * hm, (r + 1) * hm)
        h_in = _layer_norm(x_ref[0, rows, :], lg_ref[...], lb_ref[...])
        mix = (jnp.dot(d_out[rows], wod_ref[...], preferred_element_type=F32)
               + jnp.dot(g_ref[0, rows, :], wog_ref[...], preferred_element_type=F32))
        h = _layer_norm(DN_ALPHA * h_in + (1.0 + gta_ref[0]) * mix, ag_ref[...], ab_ref[...])
        h_ref[rows, :] = h
        u_ref[rows, :] = (h * (1.0 + sc_ref[0]) + sh_ref[0]).astype(BF16)

    def up_proj(f):
        halves = []
        for half in range(2):
            cols = slice(half * F + f * tf, half * F + (f + 1) * tf)
            up = jnp.dot(u_ref[...], wu_ref[:, cols], preferred_element_type=F32)
            halves.append(jnp.concatenate([carry_ref[half, f], up], axis=0))
            carry_ref[half, f] = up[tm - HALO:tm, :]
        return halves

    def gate(f, halves):
        def conv(half):
            cols = slice(half * F + f * tf, half * F + (f + 1) * tf)
            cw = cw_ref[:, cols]
            whole = halves[half]
            back1 = pltpu.roll(whole, 1, axis=0)[HALO:HALO + tm]
            back2 = pltpu.roll(whole, 2, axis=0)[HALO:HALO + tm]
            return cb_ref[:, cols] + cw[0:1] * back2 + cw[1:2] * back1 + cw[2:3] * whole[HALO:HALO + tm]
        act_ref[f] = (_silu(conv(0)) * conv(1)).astype(BF16)

    nxt = up_proj(0)
    for f in range(nf):
        cur = nxt
        if f + 1 < nf:
            nxt = up_proj(f + 1)
        gate(f, cur)

    for r in range(2):
        rows = slice(r * hm, (r + 1) * hm)
        ff = jnp.dot(act_ref[0, rows, :], wd_ref[0], preferred_element_type=F32)
        for f in range(1, nf):
            ff = ff + jnp.dot(act_ref[f, rows, :], wd_ref[f], preferred_element_type=F32)
        y = DN_ALPHA * h_ref[rows, :] + (1.0 + gtf_ref[0]) * ff
        o_ref[0, rows, :] = _layer_norm(y, fg_ref[...], fb_ref[...])


def _mlp(x, d_lo, d_hi, g_out, ln_g, ln_b, gt_a, w_od, w_og, ag, ab, sc, sh, gt_f, wu, cw, cb, wd, fg, fb, tm):
    B, S, D = x.shape
    nf, tf, _ = wd.shape
    half_tiles = d_lo.shape[1] // tm
    row = lambda b, s: (b, s, 0)
    per_b = lambda b, s: (b, 0, 0)

    def resident(shape):
        return pl.BlockSpec(shape, lambda b, s: (0,) * len(shape), pipeline_mode=pl.Buffered(1))

    return pl.pallas_call(
        functools.partial(_mlp_kernel, tm=tm, tf=tf, nf=nf, half_tiles=half_tiles),
        grid=(B, S // tm),
        in_specs=[pl.BlockSpec((1, tm, D), row),
                  pl.BlockSpec((1, tm, d_lo.shape[2]), lambda b, s: (b, jnp.minimum(s, half_tiles - 1), 0)),
                  pl.BlockSpec((1, tm, d_hi.shape[2]), lambda b, s: (b, jnp.maximum(s - half_tiles, 0), 0)),
                  pl.BlockSpec((1, tm, g_out.shape[2]), row),
                  resident((1, D)), resident((1, D)),
                  pl.BlockSpec((1, 1, D), per_b),
                  resident(w_od.shape), resident(w_og.shape),
                  resident((1, D)), resident((1, D)),
                  pl.BlockSpec((1, 1, D), per_b),
                  pl.BlockSpec((1, 1, D), per_b),
                  pl.BlockSpec((1, 1, D), per_b),
                  resident(wu.shape), resident(cw.shape), resident(cb.shape), resident(wd.shape),
                  resident((1, D)), resident((1, D))],
        out_specs=pl.BlockSpec((1, tm, D), row),
        out_shape=jax.ShapeDtypeStruct((B, S, D), F32),
        scratch_shapes=[pltpu.VMEM((2, nf, 8, tf), F32),
                        pltpu.VMEM((nf, tm, tf), BF16),
                        pltpu.VMEM((tm, D), BF16),
                        pltpu.VMEM((tm, D), F32)],
        compiler_params=pltpu.CompilerParams(
            dimension_semantics=("arbitrary", "arbitrary"), vmem_limit_bytes=VMEM_LIMIT),
        name="mlp",
    )(x, d_lo, d_hi, g_out, ln_g, ln_b, gt_a, w_od, w_og, ag, ab, sc, sh, gt_f, wu, cw, cb, wd, fg, fb)


def kernel(x, c, positions, ln_in_g, ln_in_b, w_ada, b_ada, w_in, lambda_q1, lambda_k1, lambda_q2, lambda_k2, diff_norm_w, gla_w_gate_up, gla_b_gate, gla_norm_w, w_out, ln_attn_g, ln_attn_b, w_up, conv_w, conv_b, w_down, ln_ffn_g, ln_ffn_b):
    B, S, D = x.shape
    assert D == D_MODEL and w_ada.shape[0] == 1
    tm = min(512, S)
    tf = 256
    nf = D_FF // tf

    c_pad = jnp.pad(c, ((0, 8 - B % 8 if B % 8 else 0), (0, 0)))
    ada = _ada(c_pad, w_ada[0], b_ada)[:B]
    sh_a, sc_a, gt_a, sh_f, sc_f, gt_f = [t[:, None, :] for t in jnp.split(ada, 6, axis=-1)]

    ln_g = ln_in_g[None, :]
    ln_b = ln_in_b[None, :]

    w_main = w_in[0, :, :N_MAIN]
    w_gg = jnp.pad(w_in[0, :, N_MAIN:], ((0, 0), (0, N_PROJ - w_in.shape[2])))
    w_proj = jnp.concatenate([w_main, w_gg], axis=1).astype(BF16)
    w_gate = jnp.pad(gla_w_gate_up[0], ((0, LANES - GLA_RANK), (0, 0))).astype(BF16)
    inv = ROPE_THETA ** (-jnp.arange(0, DIFF_DH, 2, dtype=F32) / DIFF_DH)
    qk, dv, gq, gk, gv, gr, gl = _proj(x, positions[:, None, :], inv[:, None], ln_g, ln_b, sc_a, sh_a,
                                       w_proj, w_gate, gla_b_gate, tm)

    d_lo, d_hi = _diffattn(qk, dv, lambda_q1, lambda_k1, lambda_q2, lambda_k2, diff_norm_w, tm, 2)
    g_out = _gla(gq, gk, gv, gl, gr, gla_norm_w, min(512, S))

    w_o = w_out[0].astype(BF16)
    dw = d_lo.shape[2]
    wd = w_down[0].astype(BF16).reshape(nf, tf, D)
    return _mlp(x, d_lo, d_hi, g_out, ln_g, ln_b, gt_a, w_o[:dw], w_o[dw:], ln_attn_g, ln_attn_b,
                sc_f, sh_f, gt_f, w_up[0].astype(BF16), conv_w[0], conv_b, wd, ln_ffn_g, ln_ffn_b, tm)
```

```python
import functools
import math

import numpy as np
import jax
import jax.numpy as jnp
from jax import lax
from jax.experimental import pallas as pl
from jax.experimental.pallas import tpu as pltpu

F32 = jnp.float32
BF16 = jnp.bfloat16

D_MODEL = 1024
DIFF_DH = 64
DIFF_HEADS = 4
HEAD_W = 2 * DIFF_DH
GLA_HEADS = 4
GLA_DK = 64
GLA_DV = 128
GLA_KW = GLA_HEADS * GLA_DK
GLA_VW = GLA_HEADS * GLA_DV
GLA_RANK = 16
GLA_TAU = 16.0
GLA_CHUNK = 64
GLA_LEVELS = 6
GLA_MILD_DECAY = 30.0
D_FF = 2816
CONV_W = 3
ROPE_THETA = 10000.0
LN_EPS = 1e-5
DEPTH = 1
DN_ALPHA = (2.0 * DEPTH) ** 0.25
LAMBDA_INIT = 0.8 - 0.6 * math.exp(-0.3 * 0)

N_MAIN = 3072
N_PROJ = N_MAIN + 128
LANES = 128
NEG_BIG = -1e30
LOG2_E = math.log2(math.e)

VMEM_LIMIT = 56 * 1024 * 1024


def _layer_norm(x, g, b):
    mu = jnp.mean(x, axis=-1, keepdims=True)
    xc = x - mu
    var = jnp.mean(xc * xc, axis=-1, keepdims=True)
    return xc * lax.rsqrt(var + LN_EPS) * g + b


def _silu(x):
    return x * jax.nn.sigmoid(x)


def _ada_kernel(c_ref, w_ref, b_ref, o_ref):
    ca = _silu(c_ref[...])
    o_ref[...] = jnp.dot(ca.astype(BF16), w_ref[...].astype(BF16),
                         preferred_element_type=F32) + b_ref[...]


def _ada(c_pad, w_ada, b_ada):
    rows, d = c_pad.shape
    n = w_ada.shape[1]
    tn = 1024
    return pl.pallas_call(
        _ada_kernel,
        grid=(n // tn,),
        in_specs=[pl.BlockSpec((rows, d), lambda j: (0, 0)),
                  pl.BlockSpec((d, tn), lambda j: (0, j)),
                  pl.BlockSpec((1, tn), lambda j: (0, j))],
        out_specs=pl.BlockSpec((rows, tn), lambda j: (0, j)),
        out_shape=jax.ShapeDtypeStruct((rows, n), F32),
        name="ada",
    )(c_pad, w_ada, b_ada)


def _proj_kernel(x_ref, pos_ref, inv_ref, lg_ref, lb_ref, sc_ref, sh_ref, w_ref, wg_ref, bg_ref,
                 qk_ref, dv_ref, gq_ref, gk_ref, gv_ref, gr_ref, gl_ref, *, tm):
    ang_t = inv_ref[...] * pos_ref[0].astype(F32)
    reps = LANES // ang_t.shape[0]
    cos = jnp.concatenate([jnp.cos(ang_t)] * reps, axis=0).T
    sin = jnp.concatenate([jnp.sin(ang_t)] * reps, axis=0).T
    lane = lax.broadcasted_iota(jnp.int32, (1, LANES), 1)
    first_half = (lane & 32) == 0
    sin_signed = jnp.where(first_half, -sin, sin)
    scale = LOG2_E / math.sqrt(DIFF_DH)

    hm = tm // 2
    for part in range(2):
        rows = slice(part * hm, (part + 1) * hm)
        h = _layer_norm(x_ref[0, rows, :], lg_ref[...], lb_ref[...])
        u = (h * (1.0 + sc_ref[0]) + sh_ref[0]).astype(BF16)
        proj = jnp.dot(u, w_ref[...], preferred_element_type=F32)
        for j in range(8):
            xg = proj[:, j * LANES:(j + 1) * LANES]
            partner = jnp.where(first_half, pltpu.roll(xg, LANES - 32, axis=1), pltpu.roll(xg, 32, axis=1))
            r = xg * cos[rows] + partner * sin_signed[rows]
            if j < 4:
                r = r * scale
            qk_ref[0, rows, j * LANES:(j + 1) * LANES] = r.astype(BF16)
        dv_ref[0, rows, :] = proj[:, 1024:1536].astype(BF16)
        gq_ref[0, rows, :] = (proj[:, 1536:1792] * (GLA_DK ** -0.5)).astype(BF16)
        gk_ref[0, rows, :] = proj[:, 1792:2048].astype(BF16)
        gv_ref[0, rows, :] = proj[:, 2048:2560].astype(BF16)
        gr_ref[0, rows, :] = proj[:, 2560:3072].astype(BF16)
        gg = proj[:, N_MAIN:N_PROJ].astype(BF16)
        z = jnp.dot(gg, wg_ref[...], preferred_element_type=F32) + bg_ref[...]
        log_sig = jnp.minimum(z, 0.0) - jnp.log1p(jnp.exp(-jnp.abs(z)))
        gl_ref[0, rows, :] = log_sig * (1.0 / GLA_TAU)


def _proj(x, pos_row, inv_col, ln_g, ln_b, sc, sh, w, wg, bg, tm):
    B, S, D = x.shape
    grid = (B, S // tm)
    row = lambda b, s: (b, s, 0)
    const2 = lambda b, s: (0, 0)
    per_b = lambda b, s: (b, 0, 0)
    widths = (1024, 512, GLA_KW, GLA_KW, GLA_VW, GLA_VW)
    out_shape = [jax.ShapeDtypeStruct((B, S, n), BF16) for n in widths]
    out_shape.append(jax.ShapeDtypeStruct((B, S, GLA_KW), F32))
    out_specs = [pl.BlockSpec((1, tm, n), row) for n in widths]
    out_specs.append(pl.BlockSpec((1, tm, GLA_KW), row))
    return pl.pallas_call(
        functools.partial(_proj_kernel, tm=tm),
        grid=grid,
        in_specs=[pl.BlockSpec((1, tm, D), row),
                  pl.BlockSpec((1, 1, tm), lambda b, s: (b, 0, s)),
                  pl.BlockSpec(inv_col.shape, const2),
                  pl.BlockSpec((1, D), const2),
                  pl.BlockSpec((1, D), const2),
                  pl.BlockSpec((1, 1, D), per_b),
                  pl.BlockSpec((1, 1, D), per_b),
                  pl.BlockSpec((D, N_PROJ), const2),
                  pl.BlockSpec((LANES, GLA_KW), const2),
                  pl.BlockSpec((1, GLA_KW), const2)],
        out_specs=out_specs,
        out_shape=out_shape,
        compiler_params=pltpu.CompilerParams(
            dimension_semantics=("arbitrary", "arbitrary"), vmem_limit_bytes=VMEM_LIMIT),
        name="proj",
    )(x, pos_row, inv_col, ln_g, ln_b, sc, sh, w, wg, bg)


def _diffattn_kernel(qa_ref, qb_ref, k_ref, v_ref, lq1_ref, lk1_ref, lq2_ref, lk2_ref, nw_ref,
                     oa_ref, ob_ref, qz_ref, m_ref, acc_ref, *, tq, nt, hps):
    i = pl.program_id(2)
    lane = lax.broadcasted_iota(jnp.int32, (1, HEAD_W), 1)
    for hh in range(hps):
        cols = slice(hh * HEAD_W, (hh + 1) * HEAD_W)
        for t, q_ref in enumerate((qa_ref, qb_ref)):
            q = q_ref[0, :, cols]
            zero = jnp.zeros_like(q)
            qz_ref[hh, t] = jnp.concatenate([jnp.where(lane < DIFF_DH, q, zero),
                                             jnp.where(lane >= DIFF_DH, q, zero)], axis=0)
    m_ref[...] = jnp.full(m_ref.shape, NEG_BIG, F32)
    acc_ref[...] = jnp.zeros(acc_ref.shape, F32)
    ones = jnp.ones((tq, LANES), BF16)

    def block_step(hh, own, blk, q_lo=0, k_lo=0, k_len=None, masked=False):
        k_len = tq if k_len is None else k_len
        nq = tq - q_lo
        cols = slice(hh * HEAD_W, (hh + 1) * HEAD_W)
        rows = pl.ds(pl.multiple_of(blk * tq + k_lo, k_len), k_len)
        spans = (slice(q_lo, tq), slice(tq + q_lo, 2 * tq))

        def load(ref):
            return jnp.concatenate([ref[hh, own, sp, :] for sp in spans], axis=0)

        kb = k_ref[0, rows, cols]
        vb = jnp.concatenate([v_ref[0, rows, cols], ones[0:k_len]], axis=1)
        s = lax.dot_general(load(qz_ref), kb, (((1,), (1,)), ((), ())), preferred_element_type=F32)
        if masked:
            r = lax.broadcasted_iota(jnp.int32, (2 * nq, k_len), 0)
            c = lax.broadcasted_iota(jnp.int32, (2 * nq, k_len), 1)
            r = jnp.where(r >= nq, r - nq, r)
            s = jnp.where(c + k_lo <= r + q_lo, s, NEG_BIG)
        s_fold = s[:, 0:LANES]
        for t in range(1, k_len // LANES):
            s_fold = jnp.maximum(s_fold, s[:, t * LANES:(t + 1) * LANES])
        m_prev = load(m_ref)
        m_new = jnp.maximum(m_prev, jnp.max(s_fold, axis=1, keepdims=True))
        alpha = jnp.exp2(m_prev - m_new)
        p = jnp.exp2(s - jnp.concatenate([m_new] * (k_len // LANES), axis=1))
        acc = (jnp.concatenate([alpha, alpha], axis=1) * load(acc_ref)
               + jnp.dot(p.astype(BF16), vb, preferred_element_type=F32))
        for n, sp in enumerate(spans):
            acc_ref[hh, own, sp, :] = acc[n * nq:(n + 1) * nq]
            m_ref[hh, own, sp, :] = m_new[n * nq:(n + 1) * nq]

    def diagonal(hh, own, blk):
        hk = tq // 2
        block_step(hh, own, blk, k_len=hk, masked=True)
        block_step(hh, own, blk, q_lo=hk, k_lo=hk, k_len=hk, masked=True)

    for hh in range(hps):
        diagonal(hh, 0, i)
    for t in range(1, nt):
        is_a = t <= i
        for hh in range(hps):
            block_step(hh, jnp.where(is_a, 0, 1), jnp.where(is_a, i - t, t - i - 1))
    for hh in range(hps):
        diagonal(hh, 1, nt - 1 - i)

    lam = (jnp.exp(jnp.sum(lq1_ref[...] * lk1_ref[...], axis=1, keepdims=True))
           - jnp.exp(jnp.sum(lq2_ref[...] * lk2_ref[...], axis=1, keepdims=True)) + LAMBDA_INIT)
    for hh in range(hps):
        cols = slice(hh * HEAD_W, (hh + 1) * HEAD_W)
        for t, o_ref in enumerate((oa_ref, ob_ref)):
            acc = acc_ref[hh, t]
            o = acc[:, 0:HEAD_W] / acc[:, HEAD_W:2 * HEAD_W]
            d = o[0:tq] - lam * o[tq:2 * tq]
            ms = jnp.mean(d * d, axis=1, keepdims=True)
            o_ref[0, :, cols] = (d * lax.rsqrt(ms + LN_EPS) * nw_ref[...] * (1.0 - LAMBDA_INIT)).astype(BF16)


def _diffattn(qk, dv, lq1, lk1, lq2, lk2, nw, tq, hps):
    B, S, _ = qk.shape
    nt = S // tq
    assert nt % 2 == 0, "query tiles are processed in pairs (i, nt-1-i)"
    assert DIFF_HEADS % hps == 0
    half = nt // 2
    groups = DIFF_HEADS // hps
    gw = hps * HEAD_W
    const2 = lambda b, h, i: (0, 0)
    out = jax.ShapeDtypeStruct((B, S // 2, DIFF_HEADS * HEAD_W), BF16)
    lo, hi = pl.pallas_call(
        functools.partial(_diffattn_kernel, tq=tq, nt=nt, hps=hps),
        grid=(B, groups, half),
        in_specs=[pl.BlockSpec((1, tq, gw), lambda b, h, i: (b, i, h)),
                  pl.BlockSpec((1, tq, gw), lambda b, h, i: (b, nt - 1 - i, h)),
                  pl.BlockSpec((1, S, gw), lambda b, h, i: (b, 0, groups + h)),
                  pl.BlockSpec((1, S, gw), lambda b, h, i: (b, 0, h)),
                  pl.BlockSpec((1, DIFF_DH), const2),
                  pl.BlockSpec((1, DIFF_DH), const2),
                  pl.BlockSpec((1, DIFF_DH), const2),
                  pl.BlockSpec((1, DIFF_DH), const2),
                  pl.BlockSpec((1, HEAD_W), const2)],
        out_specs=[pl.BlockSpec((1, tq, gw), lambda b, h, i: (b, i, h)),
                   pl.BlockSpec((1, tq, gw), lambda b, h, i: (b, half - 1 - i, h))],
        out_shape=[out, out],
        scratch_shapes=[pltpu.VMEM((hps, 2, 2 * tq, HEAD_W), BF16),
                        pltpu.VMEM((hps, 2, 2 * tq, LANES), F32),
                        pltpu.VMEM((hps, 2, 2 * tq, 2 * HEAD_W), F32)],
        compiler_params=pltpu.CompilerParams(
            dimension_semantics=("arbitrary", "arbitrary", "arbitrary"), vmem_limit_bytes=VMEM_LIMIT),
        name="diffattn",
    )(qk, qk, qk, dv, lq1, lk1, lq2, lk2, nw)
    return lo, hi


def _gla_tables():
    C = GLA_CHUNK
    t = np.arange(C)
    rng = np.zeros((2 + GLA_LEVELS, C, C), np.float32)
    rng[0] = (t[None, :] <= t[:, None])
    rng[1] = (t[None, :] > t[:, None])
    lvl_mask = np.zeros((GLA_LEVELS + 1, C, C), np.float32)
    for l in range(GLA_LEVELS):
        s = C >> (l + 1)
        blk = t // (2 * s)
        mid = blk * 2 * s + s
        upper = (t % (2 * s)) >= s
        for i in range(C):
            if upper[i]:
                rng[2 + l, i, mid[i]:i + 1] = 1.0
            else:
                rng[2 + l, i, i + 1:mid[i]] = 1.0
        lvl_mask[l] = (blk[:, None] == blk[None, :]) & upper[:, None] & (~upper[None, :])
    lvl_mask[GLA_LEVELS] = np.eye(C)
    rng = rng.reshape((2 + GLA_LEVELS) * C, C)
    rng3 = np.concatenate([rng, rng, rng], axis=1)
    lvl_mask = np.tile(lvl_mask, (1, 1, GLA_HEADS))
    hk = np.kron(np.eye(GLA_HEADS), np.ones((C, GLA_DK)))
    hv = np.kron(np.eye(GLA_HEADS), np.ones((C, GLA_DV)))
    hs = np.kron(np.eye(GLA_HEADS), np.ones((GLA_DV, GLA_DK)))
    return (jnp.asarray(rng3, BF16), jnp.asarray(lvl_mask, F32), jnp.asarray(hk, BF16),
            jnp.asarray(hv, BF16), jnp.asarray(hs, F32))


def _gla_kernel(q_ref, k_ref, v_ref, g_ref, r_ref, rng_ref, lm_ref, tril_ref, hk_ref, hv_ref, hs_ref, nw_ref,
                o_ref, st_ref, *, ts):
    C = GLA_CHUNK

    @pl.when(pl.program_id(1) == 0)
    def _():
        st_ref[...] = jnp.zeros(st_ref.shape, F32)

    def gate_sums(c, mild):
        g = g_ref[0, pl.ds(c * C, C), :]
        g_hi = g.astype(BF16)
        rem = g - g_hi.astype(F32)
        g_mid = rem.astype(BF16)
        g_lo = (rem - g_mid.astype(F32)).astype(BF16)
        g3 = jnp.concatenate([g_hi, g_mid, g_lo], axis=0)
        rng = rng_ref[0:C, :] if mild else rng_ref[...]
        return jnp.dot(rng, g3, preferred_element_type=F32)

    def block_diag(kl):
        return jnp.concatenate([kl] * GLA_HEADS, axis=0) * hk_ref[...]

    def decays_mild(c, b):
        rows = pl.ds(c * C, C)
        q = q_ref[0, rows, :].astype(F32)
        k = k_ref[0, rows, :].astype(F32)
        b_last = b[C - 1:C]
        ref = b[C // 2 - 1:C // 2]
        levels = [((q * jnp.exp(b - ref)).astype(BF16), block_diag((k * jnp.exp(ref - b)).astype(BF16)))]
        return dict(q_in=(q * jnp.exp(b)).astype(BF16),
                    k_out=(k * jnp.exp(b_last - b)).astype(BF16),
                    decay=jnp.exp(b_last),
                    levels=levels)

    def decays(c, e):
        rows = pl.ds(c * C, C)
        q = q_ref[0, rows, :].astype(F32)
        k = k_ref[0, rows, :].astype(F32)
        f = jnp.exp(e)
        levels = []
        for l in range(GLA_LEVELS + 1):
            if l < GLA_LEVELS:
                fl = f[(2 + l) * C:(3 + l) * C]
                ql = (q * fl).astype(BF16)
                kl = (k * fl).astype(BF16)
            else:
                ql = q.astype(BF16)
                kl = k.astype(BF16)
            levels.append((ql, block_diag(kl)))
        return dict(q_in=(q * f[0:C]).astype(BF16),
                    k_out=(k * f[C:2 * C]).astype(BF16),
                    decay=f[C - 1:C],
                    levels=levels)

    def intra(d, mild):
        if mild:
            ql, k_bd = d["levels"][0]
            a = lax.dot_general(ql, k_bd, (((1,), (1,)), ((), ())), preferred_element_type=F32)
            return jnp.where(tril_ref[...] > 0.5, a, 0.0).astype(BF16)
        attn = jnp.zeros((C, GLA_HEADS * C), F32)
        for l, (ql, k_bd) in enumerate(d["levels"]):
            a = lax.dot_general(ql, k_bd, (((1,), (1,)), ((), ())), preferred_element_type=F32)
            attn = attn + a * lm_ref[l]
        return attn.astype(BF16)

    def output(c, d, attn):
        rows = pl.ds(c * C, C)
        v = v_ref[0, rows, :]
        v_bd = jnp.concatenate([v] * GLA_HEADS, axis=0) * hv_ref[...]
        st = st_ref[...]
        o = jnp.dot(attn, v_bd, preferred_element_type=F32)
        o = o + lax.dot_general(d["q_in"], st.astype(BF16), (((1,), (1,)), ((), ())), preferred_element_type=F32)
        upd = lax.dot_general(v, d["k_out"], (((0,), (0,)), ((), ())), preferred_element_type=F32)
        st_ref[...] = st * d["decay"] + upd * hs_ref[...]
        parts = []
        for h in range(GLA_HEADS):
            oh = o[:, h * GLA_DV:(h + 1) * GLA_DV]
            ms = jnp.mean(oh * oh, axis=1, keepdims=True)
            parts.append(oh * lax.rsqrt(ms + LN_EPS) * nw_ref[...])
        on = jnp.concatenate(parts, axis=1)
        o_ref[0, rows, :] = (on * _silu(r_ref[0, rows, :].astype(F32))).astype(BF16)

    n = ts // C

    def run(mild):
        prep = decays_mild if mild else decays
        esum, dec, att = {0: gate_sums(0, mild)}, {}, {}
        for t in range(n + 2):
            if 0 <= t - 1 < n:
                att[t - 1] = intra(dec[t - 1], mild)
            if 0 <= t - 2 < n:
                output(t - 2, dec.pop(t - 2), att.pop(t - 2))
            if t + 1 < n:
                esum[t + 1] = gate_sums(t + 1, mild)
            if t < n:
                dec[t] = prep(t, esum.pop(t))

    totals = [jnp.sum(g_ref[0, pl.ds(c * C, C), :], axis=0, keepdims=True) for c in range(n)]
    mild = jnp.min(jnp.concatenate(totals, axis=0)) >= -GLA_MILD_DECAY

    @pl.when(mild)
    def _():
        run(True)

    @pl.when(jnp.logical_not(mild))
    def _():
        run(False)


def _gla(gq, gk, gv, gl, gr, nw, ts):
    B, S, _ = gq.shape
    rng3, lvl_mask, hk, hv, hs = _gla_tables()
    tril = jnp.sum(lvl_mask, axis=0)
    row = lambda b, s: (b, s, 0)
    const2 = lambda b, s: (0, 0)
    const3 = lambda b, s: (0, 0, 0)
    return pl.pallas_call(
        functools.partial(_gla_kernel, ts=ts),
        grid=(B, S // ts),
        in_specs=[pl.BlockSpec((1, ts, GLA_KW), row),
                  pl.BlockSpec((1, ts, GLA_KW), row),
                  pl.BlockSpec((1, ts, GLA_VW), row),
                  pl.BlockSpec((1, ts, GLA_KW), row),
                  pl.BlockSpec((1, ts, GLA_VW), row),
                  pl.BlockSpec(rng3.shape, const2),
                  pl.BlockSpec(lvl_mask.shape, const3),
                  pl.BlockSpec(tril.shape, const2),
                  pl.BlockSpec(hk.shape, const2),
                  pl.BlockSpec(hv.shape, const2),
                  pl.BlockSpec(hs.shape, const2),
                  pl.BlockSpec((1, GLA_DV), const2)],
        out_specs=pl.BlockSpec((1, ts, GLA_VW), row),
        out_shape=jax.ShapeDtypeStruct((B, S, GLA_VW), BF16),
        scratch_shapes=[pltpu.VMEM((GLA_VW, GLA_KW), F32)],
        compiler_params=pltpu.CompilerParams(
            dimension_semantics=("arbitrary", "arbitrary"), vmem_limit_bytes=VMEM_LIMIT),
        name="gla",
    )(gq, gk, gv, gl, gr, rng3, lvl_mask, tril, hk, hv, hs, nw)


def _mlp_kernel(x_ref, dlo_ref, dhi_ref, g_ref, lg_ref, lb_ref, gta_ref, wod_ref, wog_ref, ag_ref, ab_ref,
                sc_ref, sh_ref, gtf_ref, wu_ref, cw_ref, cb_ref, wd_ref, fg_ref, fb_ref, o_ref,
                carry_ref, act_ref, u_ref, h_ref, *, tm, tf, nf, half_tiles):
    HALO = 8
    F = nf * tf
    s = pl.program_id(1)

    @pl.when(s == 0)
    def _():
        carry_ref[...] = jnp.zeros(carry_ref.shape, F32)

    d_out = jnp.where(s < half_tiles, dlo_ref[0], dhi_ref[0])
    hm = tm // 2
    for r in range(2):
        rows = slice(r * hm, (r + 1) * hm)
        h_in = _layer_norm(x_ref[0, rows, :], lg_ref[...], lb_ref[...])
        mix = (jnp.dot(d_out[rows], wod_ref[...], preferred_element_type=F32)
               + jnp.dot(g_ref[0, rows, :], wog_ref[...], preferred_element_type=F32))
        h = _layer_norm(DN_ALPHA * h_in + (1.0 + gta_ref[0]) * mix, ag_ref[...], ab_ref[...])
        h_ref[rows, :] = h
        u_ref[rows, :] = (h * (1.0 + sc_ref[0]) + sh_ref[0]).astype(BF16)

    def up_proj(f):
        halves = []
        for half in range(2):
            cols = slice(half * F + f * tf, half * F + (f + 1) * tf)
            up = jnp.dot(u_ref[...], wu_ref[:, cols], preferred_element_type=F32)
            halves.append(jnp.concatenate([carry_ref[half, f], up], axis=0))
            carry_ref[half, f] = up[tm - HALO:tm, :]
        return halves

    def gate(f, halves):
        def conv(half):
            cols = slice(half * F + f * tf, half * F + (f + 1) * tf)
            cw = cw_ref[:, cols]
            whole = halves[half]
            back1 = pltpu.roll(whole, 1, axis=0)[HALO:HALO + tm]
            back2 = pltpu.roll(whole, 2, axis=0)[HALO:HALO + tm]
            return cb_ref[:, cols] + cw[0:1] * back2 + cw[1:2] * back1 + cw[2:3] * whole[HALO:HALO + tm]
        act_ref[f] = (_silu(conv(0)) * conv(1)).astype(BF16)

    for f in range(nf):
        gate(f, up_proj(f))

    for r in range(2):
        rows = slice(r * hm, (r + 1) * hm)
        ff = jnp.dot(act_ref[0, rows, :], wd_ref[0], preferred_element_type=F32)
        for f in range(1, nf):
            ff = ff + jnp.dot(act_ref[f, rows, :], wd_ref[f], preferred_element_type=F32)
        y = DN_ALPHA * h_ref[rows, :] + (1.0 + gtf_ref[0]) * ff
        o_ref[0, rows, :] = _layer_norm(y, fg_ref[...], fb_ref[...])


def _mlp(x, d_lo, d_hi, g_out, ln_g, ln_b, gt_a, w_od, w_og, ag, ab, sc, sh, gt_f, wu, cw, cb, wd, fg, fb, tm):
    B, S, D = x.shape
    nf, tf, _ = wd.shape
    half_tiles = d_lo.shape[1] // tm
    row = lambda b, s: (b, s, 0)
    per_b = lambda b, s: (b, 0, 0)

    def resident(shape):
        return pl.BlockSpec(shape, lambda b, s: (0,) * len(shape), pipeline_mode=pl.Buffered(1))

    return pl.pallas_call(
        functools.partial(_mlp_kernel, tm=tm, tf=tf, nf=nf, half_tiles=half_tiles),
        grid=(B, S // tm),
        in_specs=[pl.BlockSpec((1, tm, D), row),
                  pl.BlockSpec((1, tm, d_lo.shape[2]), lambda b, s: (b, jnp.minimum(s, half_tiles - 1), 0)),
                  pl.BlockSpec((1, tm, d_hi.shape[2]), lambda b, s: (b, jnp.maximum(s - half_tiles, 0), 0)),
                  pl.BlockSpec((1, tm, g_out.shape[2]), row),
                  resident((1, D)), resident((1, D)),
                  pl.BlockSpec((1, 1, D), per_b),
                  resident(w_od.shape), resident(w_og.shape),
                  resident((1, D)), resident((1, D)),
                  pl.BlockSpec((1, 1, D), per_b),
                  pl.BlockSpec((1, 1, D), per_b),
                  pl.BlockSpec((1, 1, D), per_b),
                  resident(wu.shape), resident(cw.shape), resident(cb.shape), resident(wd.shape),
                  resident((1, D)), resident((1, D))],
        out_specs=pl.BlockSpec((1, tm, D), row),
        out_shape=jax.ShapeDtypeStruct((B, S, D), F32),
        scratch_shapes=[pltpu.VMEM((2, nf, 8, tf), F32),
                        pltpu.VMEM((nf, tm, tf), BF16),
                        pltpu.VMEM((tm, D), BF16),
                        pltpu.VMEM((tm, D), F32)],
        compiler_params=pltpu.CompilerParams(
            dimension_semantics=("arbitrary", "arbitrary"), vmem_limit_bytes=VMEM_LIMIT),
        name="mlp",
    )(x, d_lo, d_hi, g_out, ln_g, ln_b, gt_a, w_od, w_og, ag, ab, sc, sh, gt_f, wu, cw, cb, wd, fg, fb)


def kernel(x, c, positions, ln_in_g, ln_in_b, w_ada, b_ada, w_in, lambda_q1, lambda_k1, lambda_q2, lambda_k2, diff_norm_w, gla_w_gate_up, gla_b_gate, gla_norm_w, w_out, ln_attn_g, ln_attn_b, w_up, conv_w, conv_b, w_down, ln_ffn_g, ln_ffn_b):
    B, S, D = x.shape
    assert D == D_MODEL and w_ada.shape[0] == 1
    tm = min(512, S)
    tf = 256
    nf = D_FF // tf

    c_pad = jnp.pad(c, ((0, 8 - B % 8 if B % 8 else 0), (0, 0)))
    ada = _ada(c_pad, w_ada[0], b_ada)[:B]
    sh_a, sc_a, gt_a, sh_f, sc_f, gt_f = [t[:, None, :] for t in jnp.split(ada, 6, axis=-1)]

    ln_g = ln_in_g[None, :]
    ln_b = ln_in_b[None, :]

    w_main = w_in[0, :, :N_MAIN]
    w_gg = jnp.pad(w_in[0, :, N_MAIN:], ((0, 0), (0, N_PROJ - w_in.shape[2])))
    w_proj = jnp.concatenate([w_main, w_gg], axis=1).astype(BF16)
    w_gate = jnp.pad(gla_w_gate_up[0], ((0, LANES - GLA_RANK), (0, 0))).astype(BF16)
    inv = ROPE_THETA ** (-jnp.arange(0, DIFF_DH, 2, dtype=F32) / DIFF_DH)
    qk, dv, gq, gk, gv, gr, gl = _proj(x, positions[:, None, :], inv[:, None], ln_g, ln_b, sc_a, sh_a,
                                       w_proj, w_gate, gla_b_gate, tm)

    d_lo, d_hi = _diffattn(qk, dv, lambda_q1, lambda_k1, lambda_q2, lambda_k2, diff_norm_w, tm, 2)
    g_out = _gla(gq, gk, gv, gl, gr, gla_norm_w, min(512, S))

    w_o = w_out[0].astype(BF16)
    dw = d_lo.shape[2]
    wd = w_down[0].astype(BF16).reshape(nf, tf, D)
    return _mlp(x, d_lo, d_hi, g_out, ln_g, ln_b, gt_a, w_o[:dw], w_o[dw:], ln_attn_g, ln_attn_b,
                sc_f, sh_f, gt_f, w_up[0].astype(BF16), conv_w[0], conv_b, wd, ln_ffn_g, ln_ffn_b, tm)
```

```python
import functools
import math

import numpy as np
import jax
import jax.numpy as jnp
from jax import lax
from jax.experimental import pallas as pl
from jax.experimental.pallas import tpu as pltpu

F32 = jnp.float32
BF16 = jnp.bfloat16

D_MODEL = 1024
DIFF_DH = 64
DIFF_HEADS = 4
HEAD_W = 2 * DIFF_DH
GLA_HEADS = 4
GLA_DK = 64
GLA_DV = 128
GLA_KW = GLA_HEADS * GLA_DK
GLA_VW = GLA_HEADS * GLA_DV
GLA_RANK = 16
GLA_TAU = 16.0
GLA_CHUNK = 64
GLA_LEVELS = 6
GLA_MILD_DECAY = 30.0
D_FF = 2816
CONV_W = 3
ROPE_THETA = 10000.0
LN_EPS = 1e-5
DEPTH = 1
DN_ALPHA = (2.0 * DEPTH) ** 0.25
LAMBDA_INIT = 0.8 - 0.6 * math.exp(-0.3 * 0)

N_MAIN = 3072
N_PROJ = N_MAIN + 128
LANES = 128
NEG_BIG = -1e30
LOG2_E = math.log2(math.e)

VMEM_LIMIT = 56 * 1024 * 1024


def _layer_norm(x, g, b):
    mu = jnp.mean(x, axis=-1, keepdims=True)
    xc = x - mu
    var = jnp.mean(xc * xc, axis=-1, keepdims=True)
    return xc * lax.rsqrt(var + LN_EPS) * g + b


def _silu(x):
    return x * jax.nn.sigmoid(x)


def _ada_kernel(c_ref, w_ref, b_ref, o_ref):
    ca = _silu(c_ref[...])
    o_ref[...] = jnp.dot(ca.astype(BF16), w_ref[...].astype(BF16),
                         preferred_element_type=F32) + b_ref[...]


def _ada(c_pad, w_ada, b_ada):
    rows, d = c_pad.shape
    n = w_ada.shape[1]
    tn = 1024
    return pl.pallas_call(
        _ada_kernel,
        grid=(n // tn,),
        in_specs=[pl.BlockSpec((rows, d), lambda j: (0, 0)),
                  pl.BlockSpec((d, tn), lambda j: (0, j)),
                  pl.BlockSpec((1, tn), lambda j: (0, j))],
        out_specs=pl.BlockSpec((rows, tn), lambda j: (0, j)),
        out_shape=jax.ShapeDtypeStruct((rows, n), F32),
        name="ada",
    )(c_pad, w_ada, b_ada)


def _proj_kernel(x_ref, pos_ref, inv_ref, lg_ref, lb_ref, sc_ref, sh_ref, w_ref, wg_ref, bg_ref,
                 qk_ref, dv_ref, gq_ref, gk_ref, gv_ref, gr_ref, gl_ref, wb_ref, *, tm):
    @pl.when((pl.program_id(0) == 0) & (pl.program_id(1) == 0))
    def _():
        n_in = w_ref.shape[1]
        wb_ref[:, N_MAIN:N_PROJ] = jnp.zeros((w_ref.shape[0], N_PROJ - N_MAIN), BF16)
        wb_ref[:, N_MAIN:n_in] = w_ref[:, N_MAIN:n_in].astype(BF16)
        for r in range(0, w_ref.shape[0], LANES):
            wb_ref[r:r + LANES, 0:N_MAIN] = w_ref[r:r + LANES, 0:N_MAIN].astype(BF16)

    ang_t = inv_ref[...] * pos_ref[0].astype(F32)
    reps = LANES // ang_t.shape[0]
    cos = jnp.concatenate([jnp.cos(ang_t)] * reps, axis=0).T
    sin = jnp.concatenate([jnp.sin(ang_t)] * reps, axis=0).T
    lane = lax.broadcasted_iota(jnp.int32, (1, LANES), 1)
    first_half = (lane & 32) == 0
    sin_signed = jnp.where(first_half, -sin, sin)
    scale = LOG2_E / math.sqrt(DIFF_DH)

    hm = tm // 2
    for part in range(2):
        rows = slice(part * hm, (part + 1) * hm)
        h = _layer_norm(x_ref[0, rows, :], lg_ref[...], lb_ref[...])
        u = (h * (1.0 + sc_ref[0]) + sh_ref[0]).astype(BF16)
        proj = jnp.dot(u, wb_ref[...], preferred_element_type=F32)
        for j in range(8):
            xg = proj[:, j * LANES:(j + 1) * LANES]
            partner = jnp.where(first_half, pltpu.roll(xg, LANES - 32, axis=1), pltpu.roll(xg, 32, axis=1))
            r = xg * cos[rows] + partner * sin_signed[rows]
            if j < 4:
                r = r * scale
            qk_ref[0, rows, j * LANES:(j + 1) * LANES] = r.astype(BF16)
        dv_ref[0, rows, :] = proj[:, 1024:1536].astype(BF16)
        gq_ref[0, rows, :] = (proj[:, 1536:1792] * (GLA_DK ** -0.5)).astype(BF16)
        gk_ref[0, rows, :] = proj[:, 1792:2048].astype(BF16)
        gv_ref[0, rows, :] = proj[:, 2048:2560].astype(BF16)
        gr_ref[0, rows, :] = proj[:, 2560:3072].astype(BF16)
        gg = proj[:, N_MAIN:N_PROJ].astype(BF16)
        z = jnp.dot(gg, wg_ref[...], preferred_element_type=F32) + bg_ref[...]
        log_sig = jnp.minimum(z, 0.0) - jnp.log1p(jnp.exp(-jnp.abs(z)))
        gl_ref[0, rows, :] = log_sig * (1.0 / GLA_TAU)


def _proj(x, pos_row, inv_col, ln_g, ln_b, sc, sh, w, wg, bg, tm):
    B, S, D = x.shape
    grid = (B, S // tm)
    row = lambda b, s: (b, s, 0)
    const2 = lambda b, s: (0, 0)
    per_b = lambda b, s: (b, 0, 0)
    widths = (1024, 512, GLA_KW, GLA_KW, GLA_VW, GLA_VW)
    out_shape = [jax.ShapeDtypeStruct((B, S, n), BF16) for n in widths]
    out_shape.append(jax.ShapeDtypeStruct((B, S, GLA_KW), F32))
    out_specs = [pl.BlockSpec((1, tm, n), row) for n in widths]
    out_specs.append(pl.BlockSpec((1, tm, GLA_KW), row))
    return pl.pallas_call(
        functools.partial(_proj_kernel, tm=tm),
        grid=grid,
        in_specs=[pl.BlockSpec((1, tm, D), row),
                  pl.BlockSpec((1, 1, tm), lambda b, s: (b, 0, s)),
                  pl.BlockSpec(inv_col.shape, const2),
                  pl.BlockSpec((1, D), const2),
                  pl.BlockSpec((1, D), const2),
                  pl.BlockSpec((1, 1, D), per_b),
                  pl.BlockSpec((1, 1, D), per_b),
                  pl.BlockSpec(w.shape, const2, pipeline_mode=pl.Buffered(1)),
                  pl.BlockSpec((LANES, GLA_KW), const2),
                  pl.BlockSpec((1, GLA_KW), const2)],
        out_specs=out_specs,
        out_shape=out_shape,
        scratch_shapes=[pltpu.VMEM((D, N_PROJ), BF16)],
        compiler_params=pltpu.CompilerParams(
            dimension_semantics=("arbitrary", "arbitrary"), vmem_limit_bytes=VMEM_LIMIT),
        name="proj",
    )(x, pos_row, inv_col, ln_g, ln_b, sc, sh, w, wg, bg)


def _diffattn_kernel(qa_ref, qb_ref, k_ref, v_ref, lq1_ref, lk1_ref, lq2_ref, lk2_ref, nw_ref,
                     oa_ref, ob_ref, qz_ref, m_ref, acc_ref, *, tq, nt, hps):
    i = pl.program_id(2)
    lane = lax.broadcasted_iota(jnp.int32, (1, HEAD_W), 1)
    for hh in range(hps):
        cols = slice(hh * HEAD_W, (hh + 1) * HEAD_W)
        for t, q_ref in enumerate((qa_ref, qb_ref)):
            q = q_ref[0, :, cols]
            zero = jnp.zeros_like(q)
            qz_ref[hh, t] = jnp.concatenate([jnp.where(lane < DIFF_DH, q, zero),
                                             jnp.where(lane >= DIFF_DH, q, zero)], axis=0)
    m_ref[...] = jnp.full(m_ref.shape, NEG_BIG, F32)
    acc_ref[...] = jnp.zeros(acc_ref.shape, F32)
    ones = jnp.ones((tq, LANES), BF16)

    def block_step(hh, own, blk, q_lo=0, k_lo=0, k_len=None, masked=False):
        k_len = tq if k_len is None else k_len
        nq = tq - q_lo
        cols = slice(hh * HEAD_W, (hh + 1) * HEAD_W)
        rows = pl.ds(pl.multiple_of(blk * tq + k_lo, k_len), k_len)
        spans = (slice(q_lo, tq), slice(tq + q_lo, 2 * tq))

        def load(ref):
            return jnp.concatenate([ref[hh, own, sp, :] for sp in spans], axis=0)

        kb = k_ref[0, rows, cols]
        vb = jnp.concatenate([v_ref[0, rows, cols], ones[0:k_len]], axis=1)
        s = lax.dot_general(load(qz_ref), kb, (((1,), (1,)), ((), ())), preferred_element_type=F32)
        if masked:
            r = lax.broadcasted_iota(jnp.int32, (2 * nq, k_len), 0)
            c = lax.broadcasted_iota(jnp.int32, (2 * nq, k_len), 1)
            r = jnp.where(r >= nq, r - nq, r)
            s = jnp.where(c + k_lo <= r + q_lo, s, NEG_BIG)
        s_fold = s[:, 0:LANES]
        for t in range(1, k_len // LANES):
            s_fold = jnp.maximum(s_fold, s[:, t * LANES:(t + 1) * LANES])
        m_prev = load(m_ref)
        m_new = jnp.maximum(m_prev, jnp.max(s_fold, axis=1, keepdims=True))
        alpha = jnp.exp2(m_prev - m_new)
        p = jnp.exp2(s - jnp.concatenate([m_new] * (k_len // LANES), axis=1))
        acc = (jnp.concatenate([alpha, alpha], axis=1) * load(acc_ref)
               + jnp.dot(p.astype(BF16), vb, preferred_element_type=F32))
        for n, sp in enumerate(spans):
            acc_ref[hh, own, sp, :] = acc[n * nq:(n + 1) * nq]
            m_ref[hh, own, sp, :] = m_new[n * nq:(n + 1) * nq]

    def diagonal(hh, own, blk):
        hk = tq // 2
        block_step(hh, own, blk, k_len=hk, masked=True)
        block_step(hh, own, blk, q_lo=hk, k_lo=hk, k_len=hk, masked=True)

    for hh in range(hps):
        diagonal(hh, 0, i)
    for t in range(1, nt):
        is_a = t <= i
        for hh in range(hps):
            block_step(hh, jnp.where(is_a, 0, 1), jnp.where(is_a, i - t, t - i - 1))
    for hh in range(hps):
        diagonal(hh, 1, nt - 1 - i)

    lam = (jnp.exp(jnp.sum(lq1_ref[...] * lk1_ref[...], axis=1, keepdims=True))
           - jnp.exp(jnp.sum(lq2_ref[...] * lk2_ref[...], axis=1, keepdims=True)) + LAMBDA_INIT)
    for hh in range(hps):
        cols = slice(hh * HEAD_W, (hh + 1) * HEAD_W)
        for t, o_ref in enumerate((oa_ref, ob_ref)):
            acc = acc_ref[hh, t]
            o = acc[:, 0:HEAD_W] / acc[:, HEAD_W:2 * HEAD_W]
            d = o[0:tq] - lam * o[tq:2 * tq]
            ms = jnp.mean(d * d, axis=1, keepdims=True)
            o_ref[0, :, cols] = (d * lax.rsqrt(ms + LN_EPS) * nw_ref[...] * (1.0 - LAMBDA_INIT)).astype(BF16)


def _diffattn(qk, dv, lq1, lk1, lq2, lk2, nw, tq, hps):
    B, S, _ = qk.shape
    nt = S // tq
    assert nt % 2 == 0, "query tiles are processed in pairs (i, nt-1-i)"
    assert DIFF_HEADS % hps == 0
    half = nt // 2
    groups = DIFF_HEADS // hps
    gw = hps * HEAD_W
    const2 = lambda b, h, i: (0, 0)
    out = jax.ShapeDtypeStruct((B, S // 2, DIFF_HEADS * HEAD_W), BF16)
    lo, hi = pl.pallas_call(
        functools.partial(_diffattn_kernel, tq=tq, nt=nt, hps=hps),
        grid=(B, groups, half),
        in_specs=[pl.BlockSpec((1, tq, gw), lambda b, h, i: (b, i, h)),
                  pl.BlockSpec((1, tq, gw), lambda b, h, i: (b, nt - 1 - i, h)),
                  pl.BlockSpec((1, S, gw), lambda b, h, i: (b, 0, groups + h)),
                  pl.BlockSpec((1, S, gw), lambda b, h, i: (b, 0, h)),
                  pl.BlockSpec((1, DIFF_DH), const2),
                  pl.BlockSpec((1, DIFF_DH), const2),
                  pl.BlockSpec((1, DIFF_DH), const2),
                  pl.BlockSpec((1, DIFF_DH), const2),
                  pl.BlockSpec((1, HEAD_W), const2)],
        out_specs=[pl.BlockSpec((1, tq, gw), lambda b, h, i: (b, i, h)),
                   pl.BlockSpec((1, tq, gw), lambda b, h, i: (b, half - 1 - i, h))],
        out_shape=[out, out],
        scratch_shapes=[pltpu.VMEM((hps, 2, 2 * tq, HEAD_W), BF16),
                        pltpu.VMEM((hps, 2, 2 * tq, LANES), F32),
                        pltpu.VMEM((hps, 2, 2 * tq, 2 * HEAD_W), F32)],
        compiler_params=pltpu.CompilerParams(
            dimension_semantics=("arbitrary", "arbitrary", "arbitrary"), vmem_limit_bytes=VMEM_LIMIT),
        name="diffattn",
    )(qk, qk, qk, dv, lq1, lk1, lq2, lk2, nw)
    return lo, hi


def _gla_tables():
    C = GLA_CHUNK
    t = np.arange(C)
    rng = np.zeros((2 + GLA_LEVELS, C, C), np.float32)
    rng[0] = (t[None, :] <= t[:, None])
    rng[1] = (t[None, :] > t[:, None])
    lvl_mask = np.zeros((GLA_LEVELS + 1, C, C), np.float32)
    for l in range(GLA_LEVELS):
        s = C >> (l + 1)
        blk = t // (2 * s)
        mid = blk * 2 * s + s
        upper = (t % (2 * s)) >= s
        for i in range(C):
            if upper[i]:
                rng[2 + l, i, mid[i]:i + 1] = 1.0
            else:
                rng[2 + l, i, i + 1:mid[i]] = 1.0
        lvl_mask[l] = (blk[:, None] == blk[None, :]) & upper[:, None] & (~upper[None, :])
    lvl_mask[GLA_LEVELS] = np.eye(C)
    rng = rng.reshape((2 + GLA_LEVELS) * C, C)
    rng3 = np.concatenate([rng, rng, rng], axis=1)
    lvl_mask = np.tile(lvl_mask, (1, 1, GLA_HEADS))
    hk = np.kron(np.eye(GLA_HEADS), np.ones((C, GLA_DK)))
    hv = np.kron(np.eye(GLA_HEADS), np.ones((C, GLA_DV)))
    hs = np.kron(np.eye(GLA_HEADS), np.ones((GLA_DV, GLA_DK)))
    return (jnp.asarray(rng3, BF16), jnp.asarray(lvl_mask, F32), jnp.asarray(hk, BF16),
            jnp.asarray(hv, BF16), jnp.asarray(hs, F32))


def _gla_kernel(q_ref, k_ref, v_ref, g_ref, r_ref, rng_ref, lm_ref, tril_ref, hk_ref, hv_ref, hs_ref, nw_ref,
                o_ref, st_ref, *, ts):
    C = GLA_CHUNK

    @pl.when(pl.program_id(1) == 0)
    def _():
        st_ref[...] = jnp.zeros(st_ref.shape, F32)

    def gate_sums(c, mild):
        g = g_ref[0, pl.ds(c * C, C), :]
        g_hi = g.astype(BF16)
        rem = g - g_hi.astype(F32)
        g_mid = rem.astype(BF16)
        g_lo = (rem - g_mid.astype(F32)).astype(BF16)
        g3 = jnp.concatenate([g_hi, g_mid, g_lo], axis=0)
        rng = rng_ref[0:C, :] if mild else rng_ref[...]
        return jnp.dot(rng, g3, preferred_element_type=F32)

    def block_diag(kl):
        return jnp.concatenate([kl] * GLA_HEADS, axis=0) * hk_ref[...]

    def decays_mild(c, b):
        rows = pl.ds(c * C, C)
        q = q_ref[0, rows, :].astype(F32)
        k = k_ref[0, rows, :].astype(F32)
        b_last = b[C - 1:C]
        ref = b[C // 2 - 1:C // 2]
        levels = [((q * jnp.exp(b - ref)).astype(BF16), block_diag((k * jnp.exp(ref - b)).astype(BF16)))]
        return dict(q_in=(q * jnp.exp(b)).astype(BF16),
                    k_out=(k * jnp.exp(b_last - b)).astype(BF16),
                    decay=jnp.exp(b_last),
                    levels=levels)

    def decays(c, e):
        rows = pl.ds(c * C, C)
        q = q_ref[0, rows, :].astype(F32)
        k = k_ref[0, rows, :].astype(F32)
        f = jnp.exp(e)
        levels = []
        for l in range(GLA_LEVELS + 1):
            if l < GLA_LEVELS:
                fl = f[(2 + l) * C:(3 + l) * C]
                ql = (q * fl).astype(BF16)
                kl = (k * fl).astype(BF16)
            else:
                ql = q.astype(BF16)
                kl = k.astype(BF16)
            levels.append((ql, block_diag(kl)))
        return dict(q_in=(q * f[0:C]).astype(BF16),
                    k_out=(k * f[C:2 * C]).astype(BF16),
                    decay=f[C - 1:C],
                    levels=levels)

    def intra(d, mild):
        if mild:
            ql, k_bd = d["levels"][0]
            a = lax.dot_general(ql, k_bd, (((1,), (1,)), ((), ())), preferred_element_type=F32)
            return jnp.where(tril_ref[...] > 0.5, a, 0.0).astype(BF16)
        attn = jnp.zeros((C, GLA_HEADS * C), F32)
        for l, (ql, k_bd) in enumerate(d["levels"]):
            a = lax.dot_general(ql, k_bd, (((1,), (1,)), ((), ())), preferred_element_type=F32)
            attn = attn + a * lm_ref[l]
        return attn.astype(BF16)

    def output(c, d, attn):
        rows = pl.ds(c * C, C)
        v = v_ref[0, rows, :]
        v_bd = jnp.concatenate([v] * GLA_HEADS, axis=0) * hv_ref[...]
        st = st_ref[...]
        o = jnp.dot(attn, v_bd, preferred_element_type=F32)
        o = o + lax.dot_general(d["q_in"], st.astype(BF16), (((1,), (1,)), ((), ())), preferred_element_type=F32)
        upd = lax.dot_general(v, d["k_out"], (((0,), (0,)), ((), ())), preferred_element_type=F32)
        st_ref[...] = st * d["decay"] + upd * hs_ref[...]
        parts = []
        for h in range(GLA_HEADS):
            oh = o[:, h * GLA_DV:(h + 1) * GLA_DV]
            ms = jnp.mean(oh * oh, axis=1, keepdims=True)
            parts.append(oh * lax.rsqrt(ms + LN_EPS) * nw_ref[...])
        on = jnp.concatenate(parts, axis=1)
        o_ref[0, rows, :] = (on * _silu(r_ref[0, rows, :].astype(F32))).astype(BF16)

    n = ts // C

    def run(mild):
        prep = decays_mild if mild else decays
        esum, dec, att = {0: gate_sums(0, mild)}, {}, {}
        for t in range(n + 2):
            if 0 <= t - 1 < n:
                att[t - 1] = intra(dec[t - 1], mild)
            if 0 <= t - 2 < n:
                output(t - 2, dec.pop(t - 2), att.pop(t - 2))
            if t + 1 < n:
                esum[t + 1] = gate_sums(t + 1, mild)
            if t < n:
                dec[t] = prep(t, esum.pop(t))

    totals = [jnp.sum(g_ref[0, pl.ds(c * C, C), :], axis=0, keepdims=True) for c in range(n)]
    mild = jnp.min(jnp.concatenate(totals, axis=0)) >= -GLA_MILD_DECAY

    @pl.when(mild)
    def _():
        run(True)

    @pl.when(jnp.logical_not(mild))
    def _():
        run(False)


def _gla(gq, gk, gv, gl, gr, nw, ts):
    B, S, _ = gq.shape
    rng3, lvl_mask, hk, hv, hs = _gla_tables()
    tril = jnp.sum(lvl_mask, axis=0)
    row = lambda b, s: (b, s, 0)
    const2 = lambda b, s: (0, 0)
    const3 = lambda b, s: (0, 0, 0)
    return pl.pallas_call(
        functools.partial(_gla_kernel, ts=ts),
        grid=(B, S // ts),
        in_specs=[pl.BlockSpec((1, ts, GLA_KW), row),
                  pl.BlockSpec((1, ts, GLA_KW), row),
                  pl.BlockSpec((1, ts, GLA_VW), row),
                  pl.BlockSpec((1, ts, GLA_KW), row),
                  pl.BlockSpec((1, ts, GLA_VW), row),
                  pl.BlockSpec(rng3.shape, const2),
                  pl.BlockSpec(lvl_mask.shape, const3),
                  pl.BlockSpec(tril.shape, const2),
                  pl.BlockSpec(hk.shape, const2),
                  pl.BlockSpec(hv.shape, const2),
                  pl.BlockSpec(hs.shape, const2),
                  pl.BlockSpec((1, GLA_DV), const2)],
        out_specs=pl.BlockSpec((1, ts, GLA_VW), row),
        out_shape=jax.ShapeDtypeStruct((B, S, GLA_VW), BF16),
        scratch_shapes=[pltpu.VMEM((GLA_VW, GLA_KW), F32)],
        compiler_params=pltpu.CompilerParams(
            dimension_semantics=("arbitrary", "arbitrary"), vmem_limit_bytes=VMEM_LIMIT),
        name="gla",
    )(gq, gk, gv, gl, gr, rng3, lvl_mask, tril, hk, hv, hs, nw)


def _mlp_kernel(x_ref, dlo_ref, dhi_ref, g_ref, lg_ref, lb_ref, gta_ref, wod_ref, wog_ref, ag_ref, ab_ref,
                sc_ref, sh_ref, gtf_ref, wu_ref, cw_ref, cb_ref, wd_ref, fg_ref, fb_ref, o_ref,
                carry_ref, act_ref, u_ref, h_ref, *, tm, tf, nf, half_tiles):
    HALO = 8
    F = nf * tf
    s = pl.program_id(1)

    @pl.when(s == 0)
    def _():
        carry_ref[...] = jnp.zeros(carry_ref.shape, F32)

    d_out = jnp.where(s < half_tiles, dlo_ref[0], dhi_ref[0])
    hm = tm // 2
    for r in range(2):
        rows = slice(r * hm, (r + 1) * hm)
        h_in = _layer_norm(x_ref[0, rows, :], lg_ref[...], lb_ref[...])
        mix = (jnp.dot(d_out[rows], wod_ref[...], preferred_element_type=F32)
               + jnp.dot(g_ref[0, rows, :], wog_ref[...], preferred_element_type=F32))
        h = _layer_norm(DN_ALPHA * h_in + (1.0 + gta_ref[0]) * mix, ag_ref[...], ab_ref[...])
        h_ref[rows, :] = h
        u_ref[rows, :] = (h * (1.0 + sc_ref[0]) + sh_ref[0]).astype(BF16)

    def up_proj(f):
        halves = []
        for half in range(2):
            cols = slice(half * F + f * tf, half * F + (f + 1) * tf)
            up = jnp.dot(u_ref[...], wu_ref[:, cols], preferred_element_type=F32)
            halves.append(jnp.concatenate([carry_ref[half, f], up], axis=0))
            carry_ref[half, f] = up[tm - HALO:tm, :]
        return halves

    def gate(f, halves):
        def conv(half):
            cols = slice(half * F + f * tf, half * F + (f + 1) * tf)
            cw = cw_ref[:, cols]
            whole = halves[half]
            back1 = pltpu.roll(whole, 1, axis=0)[HALO:HALO + tm]
            back2 = pltpu.roll(whole, 2, axis=0)[HALO:HALO + tm]
            return cb_ref[:, cols] + cw[0:1] * back2 + cw[1:2] * back1 + cw[2:3] * whole[HALO:HALO + tm]
        act_ref[f] = (_silu(conv(0)) * conv(1)).astype(BF16)

    for f in range(nf):
        gate(f, up_proj(f))

    for r in range(2):
        rows = slice(r * hm, (r + 1) * hm)
        ff = jnp.dot(act_ref[0, rows, :], wd_ref[0], preferred_element_type=F32)
        for f in range(1, nf):
            ff = ff + jnp.dot(act_ref[f, rows, :], wd_ref[f], preferred_element_type=F32)
        y = DN_ALPHA * h_ref[rows, :] + (1.0 + gtf_ref[0]) * ff
        o_ref[0, rows, :] = _layer_norm(y, fg_ref[...], fb_ref[...])


def _mlp(x, d_lo, d_hi, g_out, ln_g, ln_b, gt_a, w_od, w_og, ag, ab, sc, sh, gt_f, wu, cw, cb, wd, fg, fb, tm):
    B, S, D = x.shape
    nf, tf, _ = wd.shape
    half_tiles = d_lo.shape[1] // tm
    row = lambda b, s: (b, s, 0)
    per_b = lambda b, s: (b, 0, 0)

    def resident(shape):
        return pl.BlockSpec(shape, lambda b, s: (0,) * len(shape), pipeline_mode=pl.Buffered(1))

    return pl.pallas_call(
        functools.partial(_mlp_kernel, tm=tm, tf=tf, nf=nf, half_tiles=half_tiles),
        grid=(B, S // tm),
        in_specs=[pl.BlockSpec((1, tm, D), row),
                  pl.BlockSpec((1, tm, d_lo.shape[2]), lambda b, s: (b, jnp.minimum(s, half_tiles - 1), 0)),
                  pl.BlockSpec((1, tm, d_hi.shape[2]), lambda b, s: (b, jnp.maximum(s - half_tiles, 0), 0)),
                  pl.BlockSpec((1, tm, g_out.shape[2]), row),
                  resident((1, D)), resident((1, D)),
                  pl.BlockSpec((1, 1, D), per_b),
                  resident(w_od.shape), resident(w_og.shape),
                  resident((1, D)), resident((1, D)),
                  pl.BlockSpec((1, 1, D), per_b),
                  pl.BlockSpec((1, 1, D), per_b),
                  pl.BlockSpec((1, 1, D), per_b),
                  resident(wu.shape), resident(cw.shape), resident(cb.shape), resident(wd.shape),
                  resident((1, D)), resident((1, D))],
        out_specs=pl.BlockSpec((1, tm, D), row),
        out_shape=jax.ShapeDtypeStruct((B, S, D), F32),
        scratch_shapes=[pltpu.VMEM((2, nf, 8, tf), F32),
                        pltpu.VMEM((nf, tm, tf), BF16),
                        pltpu.VMEM((tm, D), BF16),
                        pltpu.VMEM((tm, D), F32)],
        compiler_params=pltpu.CompilerParams(
            dimension_semantics=("arbitrary", "arbitrary"), vmem_limit_bytes=VMEM_LIMIT),
        name="mlp",
    )(x, d_lo, d_hi, g_out, ln_g, ln_b, gt_a, w_od, w_og, ag, ab, sc, sh, gt_f, wu, cw, cb, wd, fg, fb)


def kernel(x, c, positions, ln_in_g, ln_in_b, w_ada, b_ada, w_in, lambda_q1, lambda_k1, lambda_q2, lambda_k2, diff_norm_w, gla_w_gate_up, gla_b_gate, gla_norm_w, w_out, ln_attn_g, ln_attn_b, w_up, conv_w, conv_b, w_down, ln_ffn_g, ln_ffn_b):
    B, S, D = x.shape
    assert D == D_MODEL and w_ada.shape[0] == 1
    tm = min(512, S)
    tf = 256
    nf = D_FF // tf

    c_pad = jnp.pad(c, ((0, 8 - B % 8 if B % 8 else 0), (0, 0)))
    ada = _ada(c_pad, w_ada[0], b_ada)[:B]
    sh_a, sc_a, gt_a, sh_f, sc_f, gt_f = [t[:, None, :] for t in jnp.split(ada, 6, axis=-1)]

    ln_g = ln_in_g[None, :]
    ln_b = ln_in_b[None, :]

    w_gate = jnp.pad(gla_w_gate_up[0], ((0, LANES - GLA_RANK), (0, 0))).astype(BF16)
    inv = ROPE_THETA ** (-jnp.arange(0, DIFF_DH, 2, dtype=F32) / DIFF_DH)
    qk, dv, gq, gk, gv, gr, gl = _proj(x, positions[:, None, :], inv[:, None], ln_g, ln_b, sc_a, sh_a,
                                       w_in[0], w_gate, gla_b_gate, tm)

    d_lo, d_hi = _diffattn(qk, dv, lambda_q1, lambda_k1, lambda_q2, lambda_k2, diff_norm_w, tm, 2)
    g_out = _gla(gq, gk, gv, gl, gr, gla_norm_w, min(512, S))

    w_o = w_out[0].astype(BF16)
    dw = d_lo.shape[2]
    wd = w_down[0].astype(BF16).reshape(nf, tf, D)
    return _mlp(x, d_lo, d_hi, g_out, ln_g, ln_b, gt_a, w_o[:dw], w_o[dw:], ln_attn_g, ln_attn_b,
                sc_f, sh_f, gt_f, w_up[0].astype(BF16), conv_w[0], conv_b, wd, ln_ffn_g, ln_ffn_b, tm)
```

```python
import functools
import math

import numpy as np
import jax
import jax.numpy as jnp
from jax import lax
from jax.experimental import pallas as pl
from jax.experimental.pallas import tpu as pltpu

F32 = jnp.float32
BF16 = jnp.bfloat16

D_MODEL = 1024
DIFF_DH = 64
DIFF_HEADS = 4
HEAD_W = 2 * DIFF_DH
GLA_HEADS = 4
GLA_DK = 64
GLA_DV = 128
GLA_KW = GLA_HEADS * GLA_DK
GLA_VW = GLA_HEADS * GLA_DV
GLA_RANK = 16
GLA_TAU = 16.0
GLA_CHUNK = 64
GLA_LEVELS = 6
GLA_MILD_DECAY = 30.0
D_FF = 2816
CONV_W = 3
ROPE_THETA = 10000.0
LN_EPS = 1e-5
DEPTH = 1
DN_ALPHA = (2.0 * DEPTH) ** 0.25
LAMBDA_INIT = 0.8 - 0.6 * math.exp(-0.3 * 0)

N_MAIN = 3072
N_PROJ = N_MAIN + 128
LANES = 128
NEG_BIG = -1e30
LOG2_E = math.log2(math.e)

VMEM_LIMIT = 56 * 1024 * 1024


def _layer_norm(x, g, b):
    mu = jnp.mean(x, axis=-1, keepdims=True)
    xc = x - mu
    var = jnp.mean(xc * xc, axis=-1, keepdims=True)
    return xc * lax.rsqrt(var + LN_EPS) * g + b


def _silu(x):
    return x * jax.nn.sigmoid(x)


def _ada_kernel(c_ref, w_ref, b_ref, o_ref):
    ca = _silu(c_ref[...])
    o_ref[...] = jnp.dot(ca.astype(BF16), w_ref[...].astype(BF16),
                         preferred_element_type=F32) + b_ref[...]


def _ada(c_pad, w_ada, b_ada):
    rows, d = c_pad.shape
    n = w_ada.shape[1]
    tn = 1024
    return pl.pallas_call(
        _ada_kernel,
        grid=(n // tn,),
        in_specs=[pl.BlockSpec((rows, d), lambda j: (0, 0)),
                  pl.BlockSpec((d, tn), lambda j: (0, j)),
                  pl.BlockSpec((1, tn), lambda j: (0, j))],
        out_specs=pl.BlockSpec((rows, tn), lambda j: (0, j)),
        out_shape=jax.ShapeDtypeStruct((rows, n), F32),
        name="ada",
    )(c_pad, w_ada, b_ada)


def _proj_kernel(x_ref, pos_ref, inv_ref, lg_ref, lb_ref, sc_ref, sh_ref, w_ref, wg_ref, bg_ref,
                 qk_ref, dv_ref, gq_ref, gk_ref, gv_ref, gr_ref, gl_ref, *, tm):
    ang_t = inv_ref[...] * pos_ref[0].astype(F32)
    reps = LANES // ang_t.shape[0]
    cos = jnp.concatenate([jnp.cos(ang_t)] * reps, axis=0).T
    sin = jnp.concatenate([jnp.sin(ang_t)] * reps, axis=0).T
    lane = lax.broadcasted_iota(jnp.int32, (1, LANES), 1)
    first_half = (lane & 32) == 0
    sin_signed = jnp.where(first_half, -sin, sin)
    scale = LOG2_E / math.sqrt(DIFF_DH)

    hm = tm // 2
    for part in range(2):
        rows = slice(part * hm, (part + 1) * hm)
        h = _layer_norm(x_ref[0, rows, :], lg_ref[...], lb_ref[...])
        u = (h * (1.0 + sc_ref[0]) + sh_ref[0]).astype(BF16)
        proj = jnp.dot(u, w_ref[...], preferred_element_type=F32)
        for j in range(8):
            xg = proj[:, j * LANES:(j + 1) * LANES]
            partner = jnp.where(first_half, pltpu.roll(xg, LANES - 32, axis=1), pltpu.roll(xg, 32, axis=1))
            r = xg * cos[rows] + partner * sin_signed[rows]
            if j < 4:
                r = r * scale
            qk_ref[0, rows, j * LANES:(j + 1) * LANES] = r.astype(BF16)
        dv_ref[0, rows, :] = proj[:, 1024:1536].astype(BF16)
        gq_ref[0, rows, :] = (proj[:, 1536:1792] * (GLA_DK ** -0.5)).astype(BF16)
        gk_ref[0, rows, :] = proj[:, 1792:2048].astype(BF16)
        gv_ref[0, rows, :] = proj[:, 2048:2560].astype(BF16)
        gr_ref[0, rows, :] = proj[:, 2560:3072].astype(BF16)
        gg = proj[:, N_MAIN:N_PROJ].astype(BF16)
        z = jnp.dot(gg, wg_ref[...], preferred_element_type=F32) + bg_ref[...]
        log_sig = jnp.minimum(z, 0.0) - jnp.log1p(jnp.exp(-jnp.abs(z)))
        gl_ref[0, rows, :] = log_sig * (1.0 / GLA_TAU)


def _proj(x, pos_row, inv_col, ln_g, ln_b, sc, sh, w, wg, bg, tm):
    B, S, D = x.shape
    grid = (B, S // tm)
    row = lambda b, s: (b, s, 0)
    const2 = lambda b, s: (0, 0)
    per_b = lambda b, s: (b, 0, 0)
    widths = (1024, 512, GLA_KW, GLA_KW, GLA_VW, GLA_VW)
    out_shape = [jax.ShapeDtypeStruct((B, S, n), BF16) for n in widths]
    out_shape.append(jax.ShapeDtypeStruct((B, S, GLA_KW), F32))
    out_specs = [pl.BlockSpec((1, tm, n), row) for n in widths]
    out_specs.append(pl.BlockSpec((1, tm, GLA_KW), row))
    return pl.pallas_call(
        functools.partial(_proj_kernel, tm=tm),
        grid=grid,
        in_specs=[pl.BlockSpec((1, tm, D), row),
                  pl.BlockSpec((1, 1, tm), lambda b, s: (b, 0, s)),
                  pl.BlockSpec(inv_col.shape, const2),
                  pl.BlockSpec((1, D), const2),
                  pl.BlockSpec((1, D), const2),
                  pl.BlockSpec((1, 1, D), per_b),
                  pl.BlockSpec((1, 1, D), per_b),
                  pl.BlockSpec((D, N_PROJ), const2),
                  pl.BlockSpec((LANES, GLA_KW), const2),
                  pl.BlockSpec((1, GLA_KW), const2)],
        out_specs=out_specs,
        out_shape=out_shape,
        compiler_params=pltpu.CompilerParams(
            dimension_semantics=("arbitrary", "arbitrary"), vmem_limit_bytes=VMEM_LIMIT),
        name="proj",
    )(x, pos_row, inv_col, ln_g, ln_b, sc, sh, w, wg, bg)


def _diffattn_kernel(qa_ref, qb_ref, k_ref, v_ref, lq1_ref, lk1_ref, lq2_ref, lk2_ref, nw_ref,
                     oa_ref, ob_ref, qz_ref, m_ref, acc_ref, *, tq, nt, hps):
    i = pl.program_id(2)
    lane = lax.broadcasted_iota(jnp.int32, (1, HEAD_W), 1)
    for hh in range(hps):
        cols = slice(hh * HEAD_W, (hh + 1) * HEAD_W)
        for t, q_ref in enumerate((qa_ref, qb_ref)):
            q = q_ref[0, :, cols]
            zero = jnp.zeros_like(q)
            qz_ref[hh, t] = jnp.concatenate([jnp.where(lane < DIFF_DH, q, zero),
                                             jnp.where(lane >= DIFF_DH, q, zero)], axis=0)
    m_ref[...] = jnp.full(m_ref.shape, NEG_BIG, F32)
    acc_ref[...] = jnp.zeros(acc_ref.shape, F32)
    ones = jnp.ones((tq, LANES), BF16)

    def block_step(hh, own, blk, q_lo=0, k_lo=0, k_len=None, masked=False):
        k_len = tq if k_len is None else k_len
        nq = tq - q_lo
        cols = slice(hh * HEAD_W, (hh + 1) * HEAD_W)
        rows = pl.ds(pl.multiple_of(blk * tq + k_lo, k_len), k_len)
        spans = (slice(q_lo, tq), slice(tq + q_lo, 2 * tq))

        def load(ref):
            return jnp.concatenate([ref[hh, own, sp, :] for sp in spans], axis=0)

        kb = k_ref[0, rows, cols]
        vb = jnp.concatenate([v_ref[0, rows, cols], ones[0:k_len]], axis=1)
        s = lax.dot_general(load(qz_ref), kb, (((1,), (1,)), ((), ())), preferred_element_type=F32)
        if masked:
            r = lax.broadcasted_iota(jnp.int32, (2 * nq, k_len), 0)
            c = lax.broadcasted_iota(jnp.int32, (2 * nq, k_len), 1)
            r = jnp.where(r >= nq, r - nq, r)
            s = jnp.where(c + k_lo <= r + q_lo, s, NEG_BIG)
        s_fold = s[:, 0:LANES]
        for t in range(1, k_len // LANES):
            s_fold = jnp.maximum(s_fold, s[:, t * LANES:(t + 1) * LANES])
        m_prev = load(m_ref)
        m_new = jnp.maximum(m_prev, jnp.max(s_fold, axis=1, keepdims=True))
        alpha = jnp.exp2(m_prev - m_new)
        p = jnp.exp2(s - jnp.concatenate([m_new] * (k_len // LANES), axis=1))
        acc = (jnp.concatenate([alpha, alpha], axis=1) * load(acc_ref)
               + jnp.dot(p.astype(BF16), vb, preferred_element_type=F32))
        for n, sp in enumerate(spans):
            acc_ref[hh, own, sp, :] = acc[n * nq:(n + 1) * nq]
            m_ref[hh, own, sp, :] = m_new[n * nq:(n + 1) * nq]

    def diagonal(hh, own, blk):
        hk = tq // 2
        block_step(hh, own, blk, k_len=hk, masked=True)
        block_step(hh, own, blk, q_lo=hk, k_lo=hk, k_len=hk, masked=True)

    for hh in range(hps):
        diagonal(hh, 0, i)
    for t in range(1, nt):
        is_a = t <= i
        for hh in range(hps):
            block_step(hh, jnp.where(is_a, 0, 1), jnp.where(is_a, i - t, t - i - 1))
    for hh in range(hps):
        diagonal(hh, 1, nt - 1 - i)

    lam = (jnp.exp(jnp.sum(lq1_ref[...] * lk1_ref[...], axis=1, keepdims=True))
           - jnp.exp(jnp.sum(lq2_ref[...] * lk2_ref[...], axis=1, keepdims=True)) + LAMBDA_INIT)
    for hh in range(hps):
        cols = slice(hh * HEAD_W, (hh + 1) * HEAD_W)
        for t, o_ref in enumerate((oa_ref, ob_ref)):
            acc = acc_ref[hh, t]
            o = acc[:, 0:HEAD_W] / acc[:, HEAD_W:2 * HEAD_W]
            d = o[0:tq] - lam * o[tq:2 * tq]
            ms = jnp.mean(d * d, axis=1, keepdims=True)
            o_ref[0, :, cols] = (d * lax.rsqrt(ms + LN_EPS) * nw_ref[...] * (1.0 - LAMBDA_INIT)).astype(BF16)


def _diffattn(qk, dv, lq1, lk1, lq2, lk2, nw, tq, hps):
    B, S, _ = qk.shape
    nt = S // tq
    assert nt % 2 == 0, "query tiles are processed in pairs (i, nt-1-i)"
    assert DIFF_HEADS % hps == 0
    half = nt // 2
    groups = DIFF_HEADS // hps
    gw = hps * HEAD_W
    const2 = lambda b, h, i: (0, 0)
    out = jax.ShapeDtypeStruct((B, S // 2, DIFF_HEADS * HEAD_W), BF16)
    lo, hi = pl.pallas_call(
        functools.partial(_diffattn_kernel, tq=tq, nt=nt, hps=hps),
        grid=(B, groups, half),
        in_specs=[pl.BlockSpec((1, tq, gw), lambda b, h, i: (b, i, h)),
                  pl.BlockSpec((1, tq, gw), lambda b, h, i: (b, nt - 1 - i, h)),
                  pl.BlockSpec((1, S, gw), lambda b, h, i: (b, 0, groups + h)),
                  pl.BlockSpec((1, S, gw), lambda b, h, i: (b, 0, h)),
                  pl.BlockSpec((1, DIFF_DH), const2),
                  pl.BlockSpec((1, DIFF_DH), const2),
                  pl.BlockSpec((1, DIFF_DH), const2),
                  pl.BlockSpec((1, DIFF_DH), const2),
                  pl.BlockSpec((1, HEAD_W), const2)],
        out_specs=[pl.BlockSpec((1, tq, gw), lambda b, h, i: (b, i, h)),
                   pl.BlockSpec((1, tq, gw), lambda b, h, i: (b, half - 1 - i, h))],
        out_shape=[out, out],
        scratch_shapes=[pltpu.VMEM((hps, 2, 2 * tq, HEAD_W), BF16),
                        pltpu.VMEM((hps, 2, 2 * tq, LANES), F32),
                        pltpu.VMEM((hps, 2, 2 * tq, 2 * HEAD_W), F32)],
        compiler_params=pltpu.CompilerParams(
            dimension_semantics=("arbitrary", "arbitrary", "arbitrary"), vmem_limit_bytes=VMEM_LIMIT),
        name="diffattn",
    )(qk, qk, qk, dv, lq1, lk1, lq2, lk2, nw)
    return lo, hi


def _gla_tables():
    C = GLA_CHUNK
    t = np.arange(C)
    rng = np.zeros((2 + GLA_LEVELS, C, C), np.float32)
    rng[0] = (t[None, :] <= t[:, None])
    rng[1] = (t[None, :] > t[:, None])
    lvl_mask = np.zeros((GLA_LEVELS + 1, C, C), np.float32)
    for l in range(GLA_LEVELS):
        s = C >> (l + 1)
        blk = t // (2 * s)
        mid = blk * 2 * s + s
        upper = (t % (2 * s)) >= s
        for i in range(C):
            if upper[i]:
                rng[2 + l, i, mid[i]:i + 1] = 1.0
            else:
                rng[2 + l, i, i + 1:mid[i]] = 1.0
        lvl_mask[l] = (blk[:, None] == blk[None, :]) & upper[:, None] & (~upper[None, :])
    lvl_mask[GLA_LEVELS] = np.eye(C)
    rng = rng.reshape((2 + GLA_LEVELS) * C, C)
    rng3 = np.concatenate([rng, rng, rng], axis=1)
    lvl_mask = np.tile(lvl_mask, (1, 1, GLA_HEADS))
    hk = np.kron(np.eye(GLA_HEADS), np.ones((C, GLA_DK)))
    hv = np.kron(np.eye(GLA_HEADS), np.ones((C, GLA_DV)))
    hs = np.kron(np.eye(GLA_HEADS), np.ones((GLA_DV, GLA_DK)))
    return (jnp.asarray(rng3, BF16), jnp.asarray(lvl_mask, F32), jnp.asarray(hk, BF16),
            jnp.asarray(hv, BF16), jnp.asarray(hs, F32))


def _gla_kernel(q_ref, k_ref, v_ref, g_ref, r_ref, rng_ref, lm_ref, tril_ref, hk_ref, hv_ref, hs_ref, nw_ref,
                o_ref, st_ref, *, ts):
    C = GLA_CHUNK

    @pl.when(pl.program_id(1) == 0)
    def _():
        st_ref[...] = jnp.zeros(st_ref.shape, F32)

    def gate_sums(c, mild):
        g = g_ref[0, pl.ds(c * C, C), :]
        g_hi = g.astype(BF16)
        rem = g - g_hi.astype(F32)
        g_mid = rem.astype(BF16)
        g_lo = (rem - g_mid.astype(F32)).astype(BF16)
        g3 = jnp.concatenate([g_hi, g_mid, g_lo], axis=0)
        rng = rng_ref[0:C, :] if mild else rng_ref[...]
        return jnp.dot(rng, g3, preferred_element_type=F32)

    def block_diag(kl):
        return jnp.concatenate([kl] * GLA_HEADS, axis=0) * hk_ref[...]

    def decays_mild(c, b):
        rows = pl.ds(c * C, C)
        q = q_ref[0, rows, :].astype(F32)
        k = k_ref[0, rows, :].astype(F32)
        b_last = b[C - 1:C]
        ref = b[C // 2 - 1:C // 2]
        levels = [((q * jnp.exp(b - ref)).astype(BF16), block_diag((k * jnp.exp(ref - b)).astype(BF16)))]
        return dict(q_in=(q * jnp.exp(b)).astype(BF16),
                    k_out=(k * jnp.exp(b_last - b)).astype(BF16),
                    decay=jnp.exp(b_last),
                    levels=levels)

    def decays(c, e):
        rows = pl.ds(c * C, C)
        q = q_ref[0, rows, :].astype(F32)
        k = k_ref[0, rows, :].astype(F32)
        f = jnp.exp(e)
        levels = []
        for l in range(GLA_LEVELS + 1):
            if l < GLA_LEVELS:
                fl = f[(2 + l) * C:(3 + l) * C]
                ql = (q * fl).astype(BF16)
                kl = (k * fl).astype(BF16)
            else:
                ql = q.astype(BF16)
                kl = k.astype(BF16)
            levels.append((ql, block_diag(kl)))
        return dict(q_in=(q * f[0:C]).astype(BF16),
                    k_out=(k * f[C:2 * C]).astype(BF16),
                    decay=f[C - 1:C],
                    levels=levels)

    def intra(d, mild):
        if mild:
            ql, k_bd = d["levels"][0]
            a = lax.dot_general(ql, k_bd, (((1,), (1,)), ((), ())), preferred_element_type=F32)
            return jnp.where(tril_ref[...] > 0.5, a, 0.0).astype(BF16)
        attn = jnp.zeros((C, GLA_HEADS * C), F32)
        for l, (ql, k_bd) in enumerate(d["levels"]):
            a = lax.dot_general(ql, k_bd, (((1,), (1,)), ((), ())), preferred_element_type=F32)
            attn = attn + a * lm_ref[l]
        return attn.astype(BF16)

    def output(c, d, attn):
        rows = pl.ds(c * C, C)
        v = v_ref[0, rows, :]
        v_bd = jnp.concatenate([v] * GLA_HEADS, axis=0) * hv_ref[...]
        st = st_ref[...]
        o = jnp.dot(attn, v_bd, preferred_element_type=F32)
        o = o + lax.dot_general(d["q_in"], st.astype(BF16), (((1,), (1,)), ((), ())), preferred_element_type=F32)
        upd = lax.dot_general(v, d["k_out"], (((0,), (0,)), ((), ())), preferred_element_type=F32)
        st_ref[...] = st * d["decay"] + upd * hs_ref[...]
        parts = []
        for h in range(GLA_HEADS):
            oh = o[:, h * GLA_DV:(h + 1) * GLA_DV]
            ms = jnp.mean(oh * oh, axis=1, keepdims=True)
            parts.append(oh * lax.rsqrt(ms + LN_EPS) * nw_ref[...])
        on = jnp.concatenate(parts, axis=1)
        o_ref[0, rows, :] = (on * _silu(r_ref[0, rows, :].astype(F32))).astype(BF16)

    n = ts // C

    def run(mild):
        prep = decays_mild if mild else decays
        esum, dec, att = {0: gate_sums(0, mild)}, {}, {}
        for t in range(n + 2):
            if 0 <= t - 1 < n:
                att[t - 1] = intra(dec[t - 1], mild)
            if 0 <= t - 2 < n:
                output(t - 2, dec.pop(t - 2), att.pop(t - 2))
            if t + 1 < n:
                esum[t + 1] = gate_sums(t + 1, mild)
            if t < n:
                dec[t] = prep(t, esum.pop(t))

    totals = [jnp.sum(g_ref[0, pl.ds(c * C, C), :], axis=0, keepdims=True) for c in range(n)]
    mild = jnp.min(jnp.concatenate(totals, axis=0)) >= -GLA_MILD_DECAY

    @pl.when(mild)
    def _():
        run(True)

    @pl.when(jnp.logical_not(mild))
    def _():
        run(False)


def _gla(gq, gk, gv, gl, gr, nw, ts):
    B, S, _ = gq.shape
    rng3, lvl_mask, hk, hv, hs = _gla_tables()
    tril = jnp.sum(lvl_mask, axis=0)
    row = lambda b, s: (b, s, 0)
    const2 = lambda b, s: (0, 0)
    const3 = lambda b, s: (0, 0, 0)
    return pl.pallas_call(
        functools.partial(_gla_kernel, ts=ts),
        grid=(B, S // ts),
        in_specs=[pl.BlockSpec((1, ts, GLA_KW), row),
                  pl.BlockSpec((1, ts, GLA_KW), row),
                  pl.BlockSpec((1, ts, GLA_VW), row),
                  pl.BlockSpec((1, ts, GLA_KW), row),
                  pl.BlockSpec((1, ts, GLA_VW), row),
                  pl.BlockSpec(rng3.shape, const2),
                  pl.BlockSpec(lvl_mask.shape, const3),
                  pl.BlockSpec(tril.shape, const2),
                  pl.BlockSpec(hk.shape, const2),
                  pl.BlockSpec(hv.shape, const2),
                  pl.BlockSpec(hs.shape, const2),
                  pl.BlockSpec((1, GLA_DV), const2)],
        out_specs=pl.BlockSpec((1, ts, GLA_VW), row),
        out_shape=jax.ShapeDtypeStruct((B, S, GLA_VW), BF16),
        scratch_shapes=[pltpu.VMEM((GLA_VW, GLA_KW), F32)],
        compiler_params=pltpu.CompilerParams(
            dimension_semantics=("arbitrary", "arbitrary"), vmem_limit_bytes=VMEM_LIMIT),
        name="gla",
    )(gq, gk, gv, gl, gr, rng3, lvl_mask, tril, hk, hv, hs, nw)


def _mlp_kernel(x_ref, dlo_ref, dhi_ref, g_ref, lg_ref, lb_ref, gta_ref, wod_ref, wog_ref, ag_ref, ab_ref,
                sc_ref, sh_ref, gtf_ref, wu_ref, cw_ref, cb_ref, wd_ref, fg_ref, fb_ref, o_ref,
                carry_ref, act_ref, u_ref, h_ref, *, tm, tf, nf, half_tiles):
    HALO = 8
    F = nf * tf
    s = pl.program_id(1)

    @pl.when(s == 0)
    def _():
        carry_ref[...] = jnp.zeros(carry_ref.shape, F32)

    d_out = jnp.where(s < half_tiles, dlo_ref[0], dhi_ref[0])
    hm = tm // 2
    for r in range(2):
        rows = slice(r * hm, (r + 1) * hm)
        h_in = _layer_norm(x_ref[0, rows, :], lg_ref[...], lb_ref[...])
        mix = (jnp.dot(d_out[rows], wod_ref[...], preferred_element_type=F32)
               + jnp.dot(g_ref[0, rows, :], wog_ref[...], preferred_element_type=F32))
        h = _layer_norm(DN_ALPHA * h_in + (1.0 + gta_ref[0]) * mix, ag_ref[...], ab_ref[...])
        h_ref[rows, :] = h
        u_ref[rows, :] = (h * (1.0 + sc_ref[0]) + sh_ref[0]).astype(BF16)

    def up_proj(f):
        halves = []
        for half in range(2):
            cols = slice(half * F + f * tf, half * F + (f + 1) * tf)
            up = jnp.dot(u_ref[...], wu_ref[:, cols], preferred_element_type=F32)
            halves.append(jnp.concatenate([carry_ref[half, f], up], axis=0))
            carry_ref[half, f] = up[tm - HALO:tm, :]
        return halves

    def gate(f, halves):
        def conv(half):
            cols = slice(half * F + f * tf, half * F + (f + 1) * tf)
            cw = cw_ref[:, cols]
            whole = halves[half]
            back1 = pltpu.roll(whole, 1, axis=0)[HALO:HALO + tm]
            back2 = pltpu.roll(whole, 2, axis=0)[HALO:HALO + tm]
            return cb_ref[:, cols] + cw[0:1] * back2 + cw[1:2] * back1 + cw[2:3] * whole[HALO:HALO + tm]
        act_ref[f] = (_silu(conv(0)) * conv(1)).astype(BF16)

    for f in range(nf):
        gate(f, up_proj(f))

    for r in range(2):
        rows = slice(r * hm, (r + 1) * hm)
        ff = jnp.dot(act_ref[0, rows, :], wd_ref[0], preferred_element_type=F32)
        for f in range(1, nf):
            ff = ff + jnp.dot(act_ref[f, rows, :], wd_ref[f], preferred_element_type=F32)
        y = DN_ALPHA * h_ref[rows, :] + (1.0 + gtf_ref[0]) * ff
        o_ref[0, rows, :] = _layer_norm(y, fg_ref[...], fb_ref[...])


def _mlp(x, d_lo, d_hi, g_out, ln_g, ln_b, gt_a, w_od, w_og, ag, ab, sc, sh, gt_f, wu, cw, cb, wd, fg, fb, tm):
    B, S, D = x.shape
    nf, tf, _ = wd.shape
    half_tiles = d_lo.shape[1] // tm
    row = lambda b, s: (b, s, 0)
    per_b = lambda b, s: (b, 0, 0)

    def resident(shape):
        return pl.BlockSpec(shape, lambda b, s: (0,) * len(shape), pipeline_mode=pl.Buffered(1))

    return pl.pallas_call(
        functools.partial(_mlp_kernel, tm=tm, tf=tf, nf=nf, half_tiles=half_tiles),
        grid=(B, S // tm),
        in_specs=[pl.BlockSpec((1, tm, D), row),
                  pl.BlockSpec((1, tm, d_lo.shape[2]), lambda b, s: (b, jnp.minimum(s, half_tiles - 1), 0)),
                  pl.BlockSpec((1, tm, d_hi.shape[2]), lambda b, s: (b, jnp.maximum(s - half_tiles, 0), 0)),
                  pl.BlockSpec((1, tm, g_out.shape[2]), row),
                  resident((1, D)), resident((1, D)),
                  pl.BlockSpec((1, 1, D), per_b),
                  resident(w_od.shape), resident(w_og.shape),
                  resident((1, D)), resident((1, D)),
                  pl.BlockSpec((1, 1, D), per_b),
                  pl.BlockSpec((1, 1, D), per_b),
                  pl.BlockSpec((1, 1, D), per_b),
                  resident(wu.shape), resident(cw.shape), resident(cb.shape), resident(wd.shape),
                  resident((1, D)), resident((1, D))],
        out_specs=pl.BlockSpec((1, tm, D), row),
        out_shape=jax.ShapeDtypeStruct((B, S, D), F32),
        scratch_shapes=[pltpu.VMEM((2, nf, 8, tf), F32),
                        pltpu.VMEM((nf, tm, tf), BF16),
                        pltpu.VMEM((tm, D), BF16),
                        pltpu.VMEM((tm, D), F32)],
        compiler_params=pltpu.CompilerParams(
            dimension_semantics=("arbitrary", "arbitrary"), vmem_limit_bytes=VMEM_LIMIT),
        name="mlp",
    )(x, d_lo, d_hi, g_out, ln_g, ln_b, gt_a, w_od, w_og, ag, ab, sc, sh, gt_f, wu, cw, cb, wd, fg, fb)


def kernel(x, c, positions, ln_in_g, ln_in_b, w_ada, b_ada, w_in, lambda_q1, lambda_k1, lambda_q2, lambda_k2, diff_norm_w, gla_w_gate_up, gla_b_gate, gla_norm_w, w_out, ln_attn_g, ln_attn_b, w_up, conv_w, conv_b, w_down, ln_ffn_g, ln_ffn_b):
    B, S, D = x.shape
    assert D == D_MODEL and w_ada.shape[0] == 1
    tm = min(512, S)
    tf = 256
    nf = D_FF // tf

    c_pad = jnp.pad(c, ((0, 8 - B % 8 if B % 8 else 0), (0, 0)))
    ada = _ada(c_pad, w_ada[0], b_ada)[:B]
    sh_a, sc_a, gt_a, sh_f, sc_f, gt_f = [t[:, None, :] for t in jnp.split(ada, 6, axis=-1)]

    ln_g = ln_in_g[None, :]
    ln_b = ln_in_b[None, :]

    w_b = w_in[0].astype(BF16)
    w_proj = jnp.concatenate([w_b[:, :N_MAIN], jnp.pad(w_b[:, N_MAIN:], ((0, 0), (0, N_PROJ - w_b.shape[1])))], axis=1)
    w_gate = jnp.pad(gla_w_gate_up[0], ((0, LANES - GLA_RANK), (0, 0))).astype(BF16)
    inv = ROPE_THETA ** (-jnp.arange(0, DIFF_DH, 2, dtype=F32) / DIFF_DH)
    qk, dv, gq, gk, gv, gr, gl = _proj(x, positions[:, None, :], inv[:, None], ln_g, ln_b, sc_a, sh_a,
                                       w_proj, w_gate, gla_b_gate, tm)

    d_lo, d_hi = _diffattn(qk, dv, lambda_q1, lambda_k1, lambda_q2, lambda_k2, diff_norm_w, tm, DIFF_HEADS)
    g_out = _gla(gq, gk, gv, gl, gr, gla_norm_w, min(512, S))

    w_o = w_out[0].astype(BF16)
    dw = d_lo.shape[2]
    wd = w_down[0].astype(BF16).reshape(nf, tf, D)
    return _mlp(x, d_lo, d_hi, g_out, ln_g, ln_b, gt_a, w_o[:dw], w_o[dw:], ln_attn_g, ln_attn_b,
                sc_f, sh_f, gt_f, w_up[0].astype(BF16), conv_w[0], conv_b, wd, ln_ffn_g, ln_ffn_b, tm)
```

```python
import functools
import math

import numpy as np
import jax
import jax.numpy as jnp
from jax import lax
from jax.experimental import pallas as pl
from jax.experimental.pallas import tpu as pltpu

F32 = jnp.float32
BF16 = jnp.bfloat16

D_MODEL = 1024
DIFF_DH = 64
DIFF_HEADS = 4
HEAD_W = 2 * DIFF_DH
GLA_HEADS = 4
GLA_DK = 64
GLA_DV = 128
GLA_KW = GLA_HEADS * GLA_DK
GLA_VW = GLA_HEADS * GLA_DV
GLA_RANK = 16
GLA_TAU = 16.0
GLA_CHUNK = 64
GLA_LEVELS = 6
GLA_MILD_DECAY = 30.0
D_FF = 2816
CONV_W = 3
ROPE_THETA = 10000.0
LN_EPS = 1e-5
DEPTH = 1
DN_ALPHA = (2.0 * DEPTH) ** 0.25
LAMBDA_INIT = 0.8 - 0.6 * math.exp(-0.3 * 0)

N_MAIN = 3072
N_PROJ = N_MAIN + 128
LANES = 128
NEG_BIG = -1e30
LOG2_E = math.log2(math.e)

VMEM_LIMIT = 56 * 1024 * 1024


def _layer_norm(x, g, b):
    mu = jnp.mean(x, axis=-1, keepdims=True)
    xc = x - mu
    var = jnp.mean(xc * xc, axis=-1, keepdims=True)
    return xc * lax.rsqrt(var + LN_EPS) * g + b


def _silu(x):
    return x * jax.nn.sigmoid(x)


def _ada_kernel(c_ref, w_ref, b_ref, o_ref):
    ca = _silu(c_ref[...])
    o_ref[...] = jnp.dot(ca.astype(BF16), w_ref[...].astype(BF16),
                         preferred_element_type=F32) + b_ref[...]


def _ada(c_pad, w_ada, b_ada):
    rows, d = c_pad.shape
    n = w_ada.shape[1]
    tn = 1024
    return pl.pallas_call(
        _ada_kernel,
        grid=(n // tn,),
        in_specs=[pl.BlockSpec((rows, d), lambda j: (0, 0)),
                  pl.BlockSpec((d, tn), lambda j: (0, j)),
                  pl.BlockSpec((1, tn), lambda j: (0, j))],
        out_specs=pl.BlockSpec((rows, tn), lambda j: (0, j)),
        out_shape=jax.ShapeDtypeStruct((rows, n), F32),
        name="ada",
    )(c_pad, w_ada, b_ada)


def _proj_kernel(x_ref, pos_ref, inv_ref, lg_ref, lb_ref, sc_ref, sh_ref, w_ref, wg_ref, bg_ref,
                 qk_ref, dv_ref, gq_ref, gk_ref, gv_ref, gr_ref, gl_ref, *, tm):
    ang_t = inv_ref[...] * pos_ref[0].astype(F32)
    reps = LANES // ang_t.shape[0]
    cos = jnp.concatenate([jnp.cos(ang_t)] * reps, axis=0).T
    sin = jnp.concatenate([jnp.sin(ang_t)] * reps, axis=0).T
    lane = lax.broadcasted_iota(jnp.int32, (1, LANES), 1)
    first_half = (lane & 32) == 0
    sin_signed = jnp.where(first_half, -sin, sin)
    scale = LOG2_E / math.sqrt(DIFF_DH)

    hm = tm // 2
    for part in range(2):
        rows = slice(part * hm, (part + 1) * hm)
        h = _layer_norm(x_ref[0, rows, :], lg_ref[...], lb_ref[...])
        u = (h * (1.0 + sc_ref[0]) + sh_ref[0]).astype(BF16)
        proj = jnp.dot(u, w_ref[...], preferred_element_type=F32)
        for j in range(8):
            xg = proj[:, j * LANES:(j + 1) * LANES]
            partner = jnp.where(first_half, pltpu.roll(xg, LANES - 32, axis=1), pltpu.roll(xg, 32, axis=1))
            r = xg * cos[rows] + partner * sin_signed[rows]
            if j < 4:
                r = r * scale
            qk_ref[0, rows, j * LANES:(j + 1) * LANES] = r.astype(BF16)
        dv_ref[0, rows, :] = proj[:, 1024:1536].astype(BF16)
        gq_ref[0, rows, :] = (proj[:, 1536:1792] * (GLA_DK ** -0.5)).astype(BF16)
        gk_ref[0, rows, :] = proj[:, 1792:2048].astype(BF16)
        gv_ref[0, rows, :] = proj[:, 2048:2560].astype(BF16)
        gr_ref[0, rows, :] = proj[:, 2560:3072].astype(BF16)
        gg = proj[:, N_MAIN:N_PROJ].astype(BF16)
        z = jnp.dot(gg, wg_ref[...], preferred_element_type=F32) + bg_ref[...]
        log_sig = jnp.minimum(z, 0.0) - jnp.log1p(jnp.exp(-jnp.abs(z)))
        gl_ref[0, rows, :] = log_sig * (1.0 / GLA_TAU)


def _proj(x, pos_row, inv_col, ln_g, ln_b, sc, sh, w, wg, bg, tm):
    B, S, D = x.shape
    grid = (B, S // tm)
    row = lambda b, s: (b, s, 0)
    const2 = lambda b, s: (0, 0)
    per_b = lambda b, s: (b, 0, 0)
    widths = (1024, 512, GLA_KW, GLA_KW, GLA_VW, GLA_VW)
    out_shape = [jax.ShapeDtypeStruct((B, S, n), BF16) for n in widths]
    out_shape.append(jax.ShapeDtypeStruct((B, S, GLA_KW), F32))
    out_specs = [pl.BlockSpec((1, tm, n), row) for n in widths]
    out_specs.append(pl.BlockSpec((1, tm, GLA_KW), row))
    return pl.pallas_call(
        functools.partial(_proj_kernel, tm=tm),
        grid=grid,
        in_specs=[pl.BlockSpec((1, tm, D), row),
                  pl.BlockSpec((1, 1, tm), lambda b, s: (b, 0, s)),
                  pl.BlockSpec(inv_col.shape, const2),
                  pl.BlockSpec((1, D), const2),
                  pl.BlockSpec((1, D), const2),
                  pl.BlockSpec((1, 1, D), per_b),
                  pl.BlockSpec((1, 1, D), per_b),
                  pl.BlockSpec((D, N_PROJ), const2),
                  pl.BlockSpec((LANES, GLA_KW), const2),
                  pl.BlockSpec((1, GLA_KW), const2)],
        out_specs=out_specs,
        out_shape=out_shape,
        compiler_params=pltpu.CompilerParams(
            dimension_semantics=("arbitrary", "arbitrary"), vmem_limit_bytes=VMEM_LIMIT),
        name="proj",
    )(x, pos_row, inv_col, ln_g, ln_b, sc, sh, w, wg, bg)


def _diffattn_kernel(qa_ref, qb_ref, k_ref, v_ref, lq1_ref, lk1_ref, lq2_ref, lk2_ref, nw_ref,
                     oa_ref, ob_ref, qz_ref, m_ref, acc_ref, *, tq, nt, hps):
    i = pl.program_id(2)
    lane = lax.broadcasted_iota(jnp.int32, (1, HEAD_W), 1)
    for hh in range(hps):
        cols = slice(hh * HEAD_W, (hh + 1) * HEAD_W)
        for t, q_ref in enumerate((qa_ref, qb_ref)):
            q = q_ref[0, :, cols]
            zero = jnp.zeros_like(q)
            qz_ref[hh, t] = jnp.concatenate([jnp.where(lane < DIFF_DH, q, zero),
                                             jnp.where(lane >= DIFF_DH, q, zero)], axis=0)
    m_ref[...] = jnp.full(m_ref.shape, NEG_BIG, F32)
    acc_ref[...] = jnp.zeros(acc_ref.shape, F32)
    ones = jnp.ones((tq, LANES), BF16)

    def block_step(hh, own, blk, q_lo=0, k_lo=0, k_len=None, masked=False):
        k_len = tq if k_len is None else k_len
        nq = tq - q_lo
        cols = slice(hh * HEAD_W, (hh + 1) * HEAD_W)
        rows = pl.ds(pl.multiple_of(blk * tq + k_lo, k_len), k_len)
        spans = (slice(q_lo, tq), slice(tq + q_lo, 2 * tq))

        def load(ref):
            return jnp.concatenate([ref[hh, own, sp, :] for sp in spans], axis=0)

        kb = k_ref[0, rows, cols]
        vb = jnp.concatenate([v_ref[0, rows, cols], ones[0:k_len]], axis=1)
        s = lax.dot_general(load(qz_ref), kb, (((1,), (1,)), ((), ())), preferred_element_type=F32)
        if masked:
            r = lax.broadcasted_iota(jnp.int32, (2 * nq, k_len), 0)
            c = lax.broadcasted_iota(jnp.int32, (2 * nq, k_len), 1)
            r = jnp.where(r >= nq, r - nq, r)
            s = jnp.where(c + k_lo <= r + q_lo, s, NEG_BIG)
        s_fold = s[:, 0:LANES]
        for t in range(1, k_len // LANES):
            s_fold = jnp.maximum(s_fold, s[:, t * LANES:(t + 1) * LANES])
        m_prev = load(m_ref)
        m_new = jnp.maximum(m_prev, jnp.max(s_fold, axis=1, keepdims=True))
        alpha = jnp.exp2(m_prev - m_new)
        p = jnp.exp2(s - jnp.concatenate([m_new] * (k_len // LANES), axis=1))
        acc = (jnp.concatenate([alpha, alpha], axis=1) * load(acc_ref)
               + jnp.dot(p.astype(BF16), vb, preferred_element_type=F32))
        for n, sp in enumerate(spans):
            acc_ref[hh, own, sp, :] = acc[n * nq:(n + 1) * nq]
            m_ref[hh, own, sp, :] = m_new[n * nq:(n + 1) * nq]

    def diagonal(hh, own, blk):
        hk = tq // 2
        block_step(hh, own, blk, k_len=hk, masked=True)
        block_step(hh, own, blk, q_lo=hk, k_lo=hk, k_len=hk, masked=True)

    for hh in range(hps):
        diagonal(hh, 0, i)
    for t in range(1, nt):
        is_a = t <= i
        for hh in range(hps):
            block_step(hh, jnp.where(is_a, 0, 1), jnp.where(is_a, i - t, t - i - 1))
    for hh in range(hps):
        diagonal(hh, 1, nt - 1 - i)

    lam = (jnp.exp(jnp.sum(lq1_ref[...] * lk1_ref[...], axis=1, keepdims=True))
           - jnp.exp(jnp.sum(lq2_ref[...] * lk2_ref[...], axis=1, keepdims=True)) + LAMBDA_INIT)
    for hh in range(hps):
        cols = slice(hh * HEAD_W, (hh + 1) * HEAD_W)
        for t, o_ref in enumerate((oa_ref, ob_ref)):
            acc = acc_ref[hh, t]
            o = acc[:, 0:HEAD_W] / acc[:, HEAD_W:2 * HEAD_W]
            d = o[0:tq] - lam * o[tq:2 * tq]
            ms = jnp.mean(d * d, axis=1, keepdims=True)
            o_ref[0, :, cols] = (d * lax.rsqrt(ms + LN_EPS) * nw_ref[...] * (1.0 - LAMBDA_INIT)).astype(BF16)


def _diffattn(qk, dv, lq1, lk1, lq2, lk2, nw, tq, hps):
    B, S, _ = qk.shape
    nt = S // tq
    assert nt % 2 == 0, "query tiles are processed in pairs (i, nt-1-i)"
    assert DIFF_HEADS % hps == 0
    half = nt // 2
    groups = DIFF_HEADS // hps
    gw = hps * HEAD_W
    const2 = lambda b, h, i: (0, 0)
    out = jax.ShapeDtypeStruct((B, S // 2, DIFF_HEADS * HEAD_W), BF16)
    lo, hi = pl.pallas_call(
        functools.partial(_diffattn_kernel, tq=tq, nt=nt, hps=hps),
        grid=(B, groups, half),
        in_specs=[pl.BlockSpec((1, tq, gw), lambda b, h, i: (b, i, h)),
                  pl.BlockSpec((1, tq, gw), lambda b, h, i: (b, nt - 1 - i, h)),
                  pl.BlockSpec((1, S, gw), lambda b, h, i: (b, 0, groups + h)),
                  pl.BlockSpec((1, S, gw), lambda b, h, i: (b, 0, h)),
                  pl.BlockSpec((1, DIFF_DH), const2),
                  pl.BlockSpec((1, DIFF_DH), const2),
                  pl.BlockSpec((1, DIFF_DH), const2),
                  pl.BlockSpec((1, DIFF_DH), const2),
                  pl.BlockSpec((1, HEAD_W), const2)],
        out_specs=[pl.BlockSpec((1, tq, gw), lambda b, h, i: (b, i, h)),
                   pl.BlockSpec((1, tq, gw), lambda b, h, i: (b, half - 1 - i, h))],
        out_shape=[out, out],
        scratch_shapes=[pltpu.VMEM((hps, 2, 2 * tq, HEAD_W), BF16),
                        pltpu.VMEM((hps, 2, 2 * tq, LANES), F32),
                        pltpu.VMEM((hps, 2, 2 * tq, 2 * HEAD_W), F32)],
        compiler_params=pltpu.CompilerParams(
            dimension_semantics=("arbitrary", "arbitrary", "arbitrary"), vmem_limit_bytes=VMEM_LIMIT),
        name="diffattn",
    )(qk, qk, qk, dv, lq1, lk1, lq2, lk2, nw)
    return lo, hi


def _gla_tables():
    C = GLA_CHUNK
    t = np.arange(C)
    rng = np.zeros((2 + GLA_LEVELS, C, C), np.float32)
    rng[0] = (t[None, :] <= t[:, None])
    rng[1] = (t[None, :] > t[:, None])
    lvl_mask = np.zeros((GLA_LEVELS + 1, C, C), np.float32)
    for l in range(GLA_LEVELS):
        s = C >> (l + 1)
        blk = t // (2 * s)
        mid = blk * 2 * s + s
        upper = (t % (2 * s)) >= s
        for i in range(C):
            if upper[i]:
                rng[2 + l, i, mid[i]:i + 1] = 1.0
            else:
                rng[2 + l, i, i + 1:mid[i]] = 1.0
        lvl_mask[l] = (blk[:, None] == blk[None, :]) & upper[:, None] & (~upper[None, :])
    lvl_mask[GLA_LEVELS] = np.eye(C)
    rng = rng.reshape((2 + GLA_LEVELS) * C, C)
    rng3 = np.concatenate([rng, rng, rng], axis=1)
    lvl_mask = np.tile(lvl_mask, (1, 1, GLA_HEADS))
    hk = np.kron(np.eye(GLA_HEADS), np.ones((C, GLA_DK)))
    hv = np.kron(np.eye(GLA_HEADS), np.ones((C, GLA_DV)))
    hs = np.kron(np.eye(GLA_HEADS), np.ones((GLA_DV, GLA_DK)))
    return (jnp.asarray(rng3, BF16), jnp.asarray(lvl_mask, F32), jnp.asarray(hk, BF16),
            jnp.asarray(hv, BF16), jnp.asarray(hs, F32))


def _gla_kernel(q_ref, k_ref, v_ref, g_ref, r_ref, rng_ref, lm_ref, tril_ref, hk_ref, hv_ref, hs_ref, nw_ref,
                o_ref, st_ref, *, ts):
    C = GLA_CHUNK

    @pl.when(pl.program_id(1) == 0)
    def _():
        st_ref[...] = jnp.zeros(st_ref.shape, F32)

    def gate_sums(c, mild):
        g = g_ref[0, pl.ds(c * C, C), :]
        g_hi = g.astype(BF16)
        rem = g - g_hi.astype(F32)
        g_mid = rem.astype(BF16)
        g_lo = (rem - g_mid.astype(F32)).astype(BF16)
        g3 = jnp.concatenate([g_hi, g_mid, g_lo], axis=0)
        rng = rng_ref[0:C, :] if mild else rng_ref[...]
        return jnp.dot(rng, g3, preferred_element_type=F32)

    def block_diag(kl):
        return jnp.concatenate([kl] * GLA_HEADS, axis=0) * hk_ref[...]

    def decays_mild(c, b):
        rows = pl.ds(c * C, C)
        q = q_ref[0, rows, :].astype(F32)
        k = k_ref[0, rows, :].astype(F32)
        b_last = b[C - 1:C]
        ref = b[C // 2 - 1:C // 2]
        levels = [((q * jnp.exp(b - ref)).astype(BF16), block_diag((k * jnp.exp(ref - b)).astype(BF16)))]
        return dict(q_in=(q * jnp.exp(b)).astype(BF16),
                    k_out=(k * jnp.exp(b_last - b)).astype(BF16),
                    decay=jnp.exp(b_last),
                    levels=levels)

    def decays(c, e):
        rows = pl.ds(c * C, C)
        q = q_ref[0, rows, :].astype(F32)
        k = k_ref[0, rows, :].astype(F32)
        f = jnp.exp(e)
        levels = []
        for l in range(GLA_LEVELS + 1):
            if l < GLA_LEVELS:
                fl = f[(2 + l) * C:(3 + l) * C]
                ql = (q * fl).astype(BF16)
                kl = (k * fl).astype(BF16)
            else:
                ql = q.astype(BF16)
                kl = k.astype(BF16)
            levels.append((ql, block_diag(kl)))
        return dict(q_in=(q * f[0:C]).astype(BF16),
                    k_out=(k * f[C:2 * C]).astype(BF16),
                    decay=f[C - 1:C],
                    levels=levels)

    def intra(d, mild):
        if mild:
            ql, k_bd = d["levels"][0]
            a = lax.dot_general(ql, k_bd, (((1,), (1,)), ((), ())), preferred_element_type=F32)
            return jnp.where(tril_ref[...] > 0.5, a, 0.0).astype(BF16)
        attn = jnp.zeros((C, GLA_HEADS * C), F32)
        for l, (ql, k_bd) in enumerate(d["levels"]):
            a = lax.dot_general(ql, k_bd, (((1,), (1,)), ((), ())), preferred_element_type=F32)
            attn = attn + a * lm_ref[l]
        return attn.astype(BF16)

    def output(c, d, attn):
        rows = pl.ds(c * C, C)
        v = v_ref[0, rows, :]
        v_bd = jnp.concatenate([v] * GLA_HEADS, axis=0) * hv_ref[...]
        st = st_ref[...]
        o = jnp.dot(attn, v_bd, preferred_element_type=F32)
        o = o + lax.dot_general(d["q_in"], st.astype(BF16), (((1,), (1,)), ((), ())), preferred_element_type=F32)
        upd = lax.dot_general(v, d["k_out"], (((0,), (0,)), ((), ())), preferred_element_type=F32)
        st_ref[...] = st * d["decay"] + upd * hs_ref[...]
        parts = []
        for h in range(GLA_HEADS):
            oh = o[:, h * GLA_DV:(h + 1) * GLA_DV]
            ms = jnp.mean(oh * oh, axis=1, keepdims=True)
            parts.append(oh * lax.rsqrt(ms + LN_EPS) * nw_ref[...])
        on = jnp.concatenate(parts, axis=1)
        o_ref[0, rows, :] = (on * _silu(r_ref[0, rows, :].astype(F32))).astype(BF16)

    n = ts // C

    def run(mild):
        prep = decays_mild if mild else decays
        esum, dec, att = {0: gate_sums(0, mild)}, {}, {}
        for t in range(n + 2):
            if 0 <= t - 1 < n:
                att[t - 1] = intra(dec[t - 1], mild)
            if 0 <= t - 2 < n:
                output(t - 2, dec.pop(t - 2), att.pop(t - 2))
            if t + 1 < n:
                esum[t + 1] = gate_sums(t + 1, mild)
            if t < n:
                dec[t] = prep(t, esum.pop(t))

    totals = [jnp.sum(g_ref[0, pl.ds(c * C, C), :], axis=0, keepdims=True) for c in range(n)]
    mild = jnp.min(jnp.concatenate(totals, axis=0)) >= -GLA_MILD_DECAY

    @pl.when(mild)
    def _():
        run(True)

    @pl.when(jnp.logical_not(mild))
    def _():
        run(False)


def _gla(gq, gk, gv, gl, gr, nw, ts):
    B, S, _ = gq.shape
    rng3, lvl_mask, hk, hv, hs = _gla_tables()
    tril = jnp.sum(lvl_mask, axis=0)
    row = lambda b, s: (b, s, 0)
    const2 = lambda b, s: (0, 0)
    const3 = lambda b, s: (0, 0, 0)
    return pl.pallas_call(
        functools.partial(_gla_kernel, ts=ts),
        grid=(B, S // ts),
        in_specs=[pl.BlockSpec((1, ts, GLA_KW), row),
                  pl.BlockSpec((1, ts, GLA_KW), row),
                  pl.BlockSpec((1, ts, GLA_VW), row),
                  pl.BlockSpec((1, ts, GLA_KW), row),
                  pl.BlockSpec((1, ts, GLA_VW), row),
                  pl.BlockSpec(rng3.shape, const2),
                  pl.BlockSpec(lvl_mask.shape, const3),
                  pl.BlockSpec(tril.shape, const2),
                  pl.BlockSpec(hk.shape, const2),
                  pl.BlockSpec(hv.shape, const2),
                  pl.BlockSpec(hs.shape, const2),
                  pl.BlockSpec((1, GLA_DV), const2)],
        out_specs=pl.BlockSpec((1, ts, GLA_VW), row),
        out_shape=jax.ShapeDtypeStruct((B, S, GLA_VW), BF16),
        scratch_shapes=[pltpu.VMEM((GLA_VW, GLA_KW), F32)],
        compiler_params=pltpu.CompilerParams(
            dimension_semantics=("arbitrary", "arbitrary"), vmem_limit_bytes=VMEM_LIMIT),
        name="gla",
    )(gq, gk, gv, gl, gr, rng3, lvl_mask, tril, hk, hv, hs, nw)


def _mlp_kernel(x_ref, dlo_ref, dhi_ref, g_ref, lg_ref, lb_ref, gta_ref, wod_ref, wog_ref, ag_ref, ab_ref,
                sc_ref, sh_ref, gtf_ref, wu_ref, cw_ref, cb_ref, wd_ref, fg_ref, fb_ref, o_ref,
                carry_ref, act_ref, u_ref, h_ref, y_ref, hp_ref, *, tm, tf, nf, half_tiles):
    SUB = 8
    blk = tm // SUB
    F = nf * tf
    s = pl.program_id(1)

    @pl.when(s == 0)
    def _():
        carry_ref[...] = jnp.zeros(carry_ref.shape, F32)

    nl = h_ref.shape[0]

    def gather_rows(ref, start, size, stride):
        return jnp.concatenate([ref[c, pl.ds(start, size, stride=stride), :] for c in range(nl)], axis=1)

    def put_rows(ref, rows, val):
        for c in range(nl):
            ref[c, rows, :] = val[:, c * LANES:(c + 1) * LANES]

    d_out = jnp.where(s < half_tiles, dlo_ref[0], dhi_ref[0])
    hm = tm // 2
    for r in range(2):
        rows = slice(r * hm, (r + 1) * hm)
        h_in = _layer_norm(x_ref[0, rows, :], lg_ref[...], lb_ref[...])
        mix = (jnp.dot(d_out[rows], wod_ref[...], preferred_element_type=F32)
               + jnp.dot(g_ref[0, rows, :], wog_ref[...], preferred_element_type=F32))
        put_rows(h_ref, rows, _layer_norm(DN_ALPHA * h_in + (1.0 + gta_ref[0]) * mix, ag_ref[...], ab_ref[...]))

    for b in range(SUB):
        hb = gather_rows(h_ref, b, blk, SUB)
        hp_ref[b * blk:(b + 1) * blk, :] = hb
        u_ref[b * blk:(b + 1) * blk, :] = (hb * (1.0 + sc_ref[0]) + sh_ref[0]).astype(BF16)

    def shift_rows(block, before):
        return pltpu.roll(jnp.concatenate([before, block], axis=0), 1, axis=0)[SUB:SUB + blk]

    def conv_half(half, f):
        cols = slice(half * F + f * tf, half * F + (f + 1) * tf)
        up = jnp.dot(u_ref[...], wu_ref[:, cols], preferred_element_type=F32)
        b6 = shift_rows(up[6 * blk:7 * blk], carry_ref[half, f, 0:SUB])
        b7 = shift_rows(up[7 * blk:8 * blk], carry_ref[half, f, SUB:2 * SUB])
        carry_ref[half, f, 0:SUB] = up[7 * blk - SUB:7 * blk]
        carry_ref[half, f, SUB:2 * SUB] = up[8 * blk - SUB:8 * blk]
        back1 = jnp.concatenate([b7, up[0:7 * blk]], axis=0)
        back2 = jnp.concatenate([b6, b7, up[0:6 * blk]], axis=0)
        cw = cw_ref[:, cols]
        return cb_ref[:, cols] + cw[0:1] * back2 + cw[1:2] * back1 + cw[2:3] * up

    for f in range(nf):
        act_ref[f] = (_silu(conv_half(0, f)) * conv_half(1, f)).astype(BF16)

    for r in range(2):
        rows = slice(r * hm, (r + 1) * hm)
        ff = jnp.dot(act_ref[0, rows, :], wd_ref[0], preferred_element_type=F32)
        for f in range(1, nf):
            ff = ff + jnp.dot(act_ref[f, rows, :], wd_ref[f], preferred_element_type=F32)
        for j in range(hm // blk):
            b = r * (hm // blk) + j
            y = DN_ALPHA * hp_ref[b * blk:(b + 1) * blk, :] + (1.0 + gtf_ref[0]) * ff[j * blk:(j + 1) * blk]
            put_rows(y_ref, pl.ds(b, blk, stride=SUB), _layer_norm(y, fg_ref[...], fb_ref[...]))

    o_ref[0] = jnp.concatenate([y_ref[c] for c in range(nl)], axis=1)


def _mlp(x, d_lo, d_hi, g_out, ln_g, ln_b, gt_a, w_od, w_og, ag, ab, sc, sh, gt_f, wu, cw, cb, wd, fg, fb, tm):
    B, S, D = x.shape
    nf, tf, _ = wd.shape
    half_tiles = d_lo.shape[1] // tm
    row = lambda b, s: (b, s, 0)
    per_b = lambda b, s: (b, 0, 0)

    def resident(shape):
        return pl.BlockSpec(shape, lambda b, s: (0,) * len(shape), pipeline_mode=pl.Buffered(1))

    return pl.pallas_call(
        functools.partial(_mlp_kernel, tm=tm, tf=tf, nf=nf, half_tiles=half_tiles),
        grid=(B, S // tm),
        in_specs=[pl.BlockSpec((1, tm, D), row),
                  pl.BlockSpec((1, tm, d_lo.shape[2]), lambda b, s: (b, jnp.minimum(s, half_tiles - 1), 0)),
                  pl.BlockSpec((1, tm, d_hi.shape[2]), lambda b, s: (b, jnp.maximum(s - half_tiles, 0), 0)),
                  pl.BlockSpec((1, tm, g_out.shape[2]), row),
                  resident((1, D)), resident((1, D)),
                  pl.BlockSpec((1, 1, D), per_b),
                  resident(w_od.shape), resident(w_og.shape),
                  resident((1, D)), resident((1, D)),
                  pl.BlockSpec((1, 1, D), per_b),
                  pl.BlockSpec((1, 1, D), per_b),
                  pl.BlockSpec((1, 1, D), per_b),
                  resident(wu.shape), resident(cw.shape), resident(cb.shape), resident(wd.shape),
                  resident((1, D)), resident((1, D))],
        out_specs=pl.BlockSpec((1, tm, D), row),
        out_shape=jax.ShapeDtypeStruct((B, S, D), F32),
        scratch_shapes=[pltpu.VMEM((2, nf, 16, tf), F32),
                        pltpu.VMEM((nf, tm, tf), BF16),
                        pltpu.VMEM((tm, D), BF16),
                        pltpu.VMEM((D // LANES, tm, LANES), F32),
                        pltpu.VMEM((D // LANES, tm, LANES), F32),
                        pltpu.VMEM((tm, D), F32)],
        compiler_params=pltpu.CompilerParams(
            dimension_semantics=("arbitrary", "arbitrary"), vmem_limit_bytes=VMEM_LIMIT),
        name="mlp",
    )(x, d_lo, d_hi, g_out, ln_g, ln_b, gt_a, w_od, w_og, ag, ab, sc, sh, gt_f, wu, cw, cb, wd, fg, fb)


def kernel(x, c, positions, ln_in_g, ln_in_b, w_ada, b_ada, w_in, lambda_q1, lambda_k1, lambda_q2, lambda_k2, diff_norm_w, gla_w_gate_up, gla_b_gate, gla_norm_w, w_out, ln_attn_g, ln_attn_b, w_up, conv_w, conv_b, w_down, ln_ffn_g, ln_ffn_b):
    B, S, D = x.shape
    assert D == D_MODEL and w_ada.shape[0] == 1
    tm = min(512, S)
    tf = 256
    nf = D_FF // tf

    c_pad = jnp.pad(c, ((0, 8 - B % 8 if B % 8 else 0), (0, 0)))
    ada = _ada(c_pad, w_ada[0], b_ada)[:B]
    sh_a, sc_a, gt_a, sh_f, sc_f, gt_f = [t[:, None, :] for t in jnp.split(ada, 6, axis=-1)]

    ln_g = ln_in_g[None, :]
    ln_b = ln_in_b[None, :]

    w_b = w_in[0].astype(BF16)
    w_proj = jnp.concatenate([w_b[:, :N_MAIN], jnp.pad(w_b[:, N_MAIN:], ((0, 0), (0, N_PROJ - w_b.shape[1])))], axis=1)
    w_gate = jnp.pad(gla_w_gate_up[0], ((0, LANES - GLA_RANK), (0, 0))).astype(BF16)
    inv = ROPE_THETA ** (-jnp.arange(0, DIFF_DH, 2, dtype=F32) / DIFF_DH)
    qk, dv, gq, gk, gv, gr, gl = _proj(x, positions[:, None, :], inv[:, None], ln_g, ln_b, sc_a, sh_a,
                                       w_proj, w_gate, gla_b_gate, tm)

    d_lo, d_hi = _diffattn(qk, dv, lambda_q1, lambda_k1, lambda_q2, lambda_k2, diff_norm_w, tm, 2)
    g_out = _gla(gq, gk, gv, gl, gr, gla_norm_w, min(512, S))

    w_o = w_out[0].astype(BF16)
    dw = d_lo.shape[2]
    wd = w_down[0].astype(BF16).reshape(nf, tf, D)
    return _mlp(x, d_lo, d_hi, g_out, ln_g, ln_b, gt_a, w_o[:dw], w_o[dw:], ln_attn_g, ln_attn_b,
                sc_f, sh_f, gt_f, w_up[0].astype(BF16), conv_w[0], conv_b, wd, ln_ffn_g, ln_ffn_b, tm)
```

```python
import functools
import math

import numpy as np
import jax
import jax.numpy as jnp
from jax import lax
from jax.experimental import pallas as pl
from jax.experimental.pallas import tpu as pltpu

F32 = jnp.float32
BF16 = jnp.bfloat16

D_MODEL = 1024
DIFF_DH = 64
DIFF_HEADS = 4
HEAD_W = 2 * DIFF_DH
GLA_HEADS = 4
GLA_DK = 64
GLA_DV = 128
GLA_KW = GLA_HEADS * GLA_DK
GLA_VW = GLA_HEADS * GLA_DV
GLA_RANK = 16
GLA_TAU = 16.0
GLA_CHUNK = 64
GLA_LEVELS = 6
GLA_MILD_DECAY = 30.0
D_FF = 2816
CONV_W = 3
ROPE_THETA = 10000.0
LN_EPS = 1e-5
DEPTH = 1
DN_ALPHA = (2.0 * DEPTH) ** 0.25
LAMBDA_INIT = 0.8 - 0.6 * math.exp(-0.3 * 0)

N_MAIN = 3072
N_PROJ = N_MAIN + 128
LANES = 128
NEG_BIG = -1e30
LOG2_E = math.log2(math.e)

VMEM_LIMIT = 56 * 1024 * 1024


def _layer_norm(x, g, b):
    mu = jnp.mean(x, axis=-1, keepdims=True)
    xc = x - mu
    var = jnp.mean(xc * xc, axis=-1, keepdims=True)
    return xc * lax.rsqrt(var + LN_EPS) * g + b


def _silu(x):
    return x * jax.nn.sigmoid(x)


def _ada_kernel(c_ref, w_ref, b_ref, o_ref):
    ca = _silu(c_ref[...])
    o_ref[...] = jnp.dot(ca.astype(BF16), w_ref[...].astype(BF16),
                         preferred_element_type=F32) + b_ref[...]


def _ada(c_pad, w_ada, b_ada):
    rows, d = c_pad.shape
    n = w_ada.shape[1]
    tn = 1024
    return pl.pallas_call(
        _ada_kernel,
        grid=(n // tn,),
        in_specs=[pl.BlockSpec((rows, d), lambda j: (0, 0)),
                  pl.BlockSpec((d, tn), lambda j: (0, j)),
                  pl.BlockSpec((1, tn), lambda j: (0, j))],
        out_specs=pl.BlockSpec((rows, tn), lambda j: (0, j)),
        out_shape=jax.ShapeDtypeStruct((rows, n), F32),
        name="ada",
    )(c_pad, w_ada, b_ada)


def _proj_kernel(x_ref, pos_ref, inv_ref, lg_ref, lb_ref, sc_ref, sh_ref, w_ref, wg_ref, bg_ref,
                 qk_ref, dv_ref, gq_ref, gk_ref, gv_ref, gr_ref, gl_ref, *, tm):
    ang_t = inv_ref[...] * pos_ref[0].astype(F32)
    reps = LANES // ang_t.shape[0]
    cos = jnp.concatenate([jnp.cos(ang_t)] * reps, axis=0).T
    sin = jnp.concatenate([jnp.sin(ang_t)] * reps, axis=0).T
    lane = lax.broadcasted_iota(jnp.int32, (1, LANES), 1)
    first_half = (lane & 32) == 0
    sin_signed = jnp.where(first_half, -sin, sin)
    scale = LOG2_E / math.sqrt(DIFF_DH)

    hm = tm // 2
    for part in range(2):
        rows = slice(part * hm, (part + 1) * hm)
        h = _layer_norm(x_ref[0, rows, :], lg_ref[...], lb_ref[...])
        u = (h * (1.0 + sc_ref[0]) + sh_ref[0]).astype(BF16)
        proj = jnp.dot(u, w_ref[...], preferred_element_type=F32)
        for j in range(8):
            xg = proj[:, j * LANES:(j + 1) * LANES]
            partner = jnp.where(first_half, pltpu.roll(xg, LANES - 32, axis=1), pltpu.roll(xg, 32, axis=1))
            r = xg * cos[rows] + partner * sin_signed[rows]
            if j < 4:
                r = r * scale
            qk_ref[0, rows, j * LANES:(j + 1) * LANES] = r.astype(BF16)
        dv_ref[0, rows, :] = proj[:, 1024:1536].astype(BF16)
        gq_ref[0, rows, :] = (proj[:, 1536:1792] * (GLA_DK ** -0.5)).astype(BF16)
        gk_ref[0, rows, :] = proj[:, 1792:2048].astype(BF16)
        gv_ref[0, rows, :] = proj[:, 2048:2560].astype(BF16)
        gr_ref[0, rows, :] = proj[:, 2560:3072].astype(BF16)
        gg = proj[:, N_MAIN:N_PROJ].astype(BF16)
        z = jnp.dot(gg, wg_ref[...], preferred_element_type=F32) + bg_ref[...]
        log_sig = jnp.minimum(z, 0.0) - jnp.log1p(jnp.exp(-jnp.abs(z)))
        gl_ref[0, rows, :] = log_sig * (1.0 / GLA_TAU)


def _proj(x, pos_row, inv_col, ln_g, ln_b, sc, sh, w, wg, bg, tm):
    B, S, D = x.shape
    grid = (B, S // tm)
    row = lambda b, s: (b, s, 0)
    const2 = lambda b, s: (0, 0)
    per_b = lambda b, s: (b, 0, 0)
    widths = (1024, 512, GLA_KW, GLA_KW, GLA_VW, GLA_VW)
    out_shape = [jax.ShapeDtypeStruct((B, S, n), BF16) for n in widths]
    out_shape.append(jax.ShapeDtypeStruct((B, S, GLA_KW), F32))
    out_specs = [pl.BlockSpec((1, tm, n), row) for n in widths]
    out_specs.append(pl.BlockSpec((1, tm, GLA_KW), row))
    return pl.pallas_call(
        functools.partial(_proj_kernel, tm=tm),
        grid=grid,
        in_specs=[pl.BlockSpec((1, tm, D), row),
                  pl.BlockSpec((1, 1, tm), lambda b, s: (b, 0, s)),
                  pl.BlockSpec(inv_col.shape, const2),
                  pl.BlockSpec((1, D), const2),
                  pl.BlockSpec((1, D), const2),
                  pl.BlockSpec((1, 1, D), per_b),
                  pl.BlockSpec((1, 1, D), per_b),
                  pl.BlockSpec((D, N_PROJ), const2),
                  pl.BlockSpec((LANES, GLA_KW), const2),
                  pl.BlockSpec((1, GLA_KW), const2)],
        out_specs=out_specs,
        out_shape=out_shape,
        compiler_params=pltpu.CompilerParams(
            dimension_semantics=("arbitrary", "arbitrary"), vmem_limit_bytes=VMEM_LIMIT),
        name="proj",
    )(x, pos_row, inv_col, ln_g, ln_b, sc, sh, w, wg, bg)


def _diffattn_kernel(qa_ref, qb_ref, k_ref, v_ref, lq1_ref, lk1_ref, lq2_ref, lk2_ref, nw_ref,
                     oa_ref, ob_ref, qz_ref, m_ref, acc_ref, *, tq, nt, hps):
    i = pl.program_id(2)
    lane = lax.broadcasted_iota(jnp.int32, (1, HEAD_W), 1)
    for hh in range(hps):
        cols = slice(hh * HEAD_W, (hh + 1) * HEAD_W)
        for t, q_ref in enumerate((qa_ref, qb_ref)):
            q = q_ref[0, :, cols]
            zero = jnp.zeros_like(q)
            qz_ref[hh, t] = jnp.concatenate([jnp.where(lane < DIFF_DH, q, zero),
                                             jnp.where(lane >= DIFF_DH, q, zero)], axis=0)
    m_ref[...] = jnp.full(m_ref.shape, NEG_BIG, F32)
    acc_ref[...] = jnp.zeros(acc_ref.shape, F32)
    ones = jnp.ones((tq, LANES), BF16)

    def block_step(hh, own, blk, q_lo=0, k_lo=0, k_len=None, masked=False):
        k_len = tq if k_len is None else k_len
        nq = tq - q_lo
        cols = slice(hh * HEAD_W, (hh + 1) * HEAD_W)
        rows = pl.ds(pl.multiple_of(blk * tq + k_lo, k_len), k_len)
        spans = (slice(q_lo, tq), slice(tq + q_lo, 2 * tq))

        def load(ref):
            return jnp.concatenate([ref[hh, own, sp, :] for sp in spans], axis=0)

        kb = k_ref[0, rows, cols]
        vb = jnp.concatenate([v_ref[0, rows, cols], ones[0:k_len]], axis=1)
        s = lax.dot_general(load(qz_ref), kb, (((1,), (1,)), ((), ())), preferred_element_type=F32)
        if masked:
            r = lax.broadcasted_iota(jnp.int32, (2 * nq, k_len), 0)
            c = lax.broadcasted_iota(jnp.int32, (2 * nq, k_len), 1)
            r = jnp.where(r >= nq, r - nq, r)
            s = jnp.where(c + k_lo <= r + q_lo, s, NEG_BIG)
        s_fold = s[:, 0:LANES]
        for t in range(1, k_len // LANES):
            s_fold = jnp.maximum(s_fold, s[:, t * LANES:(t + 1) * LANES])
        m_prev = load(m_ref)
        m_new = jnp.maximum(m_prev, jnp.max(s_fold, axis=1, keepdims=True))
        alpha = jnp.exp2(m_prev - m_new)
        p = jnp.exp2(s - jnp.concatenate([m_new] * (k_len // LANES), axis=1))
        acc = (jnp.concatenate([alpha, alpha], axis=1) * load(acc_ref)
               + jnp.dot(p.astype(BF16), vb, preferred_element_type=F32))
        for n, sp in enumerate(spans):
            acc_ref[hh, own, sp, :] = acc[n * nq:(n + 1) * nq]
            m_ref[hh, own, sp, :] = m_new[n * nq:(n + 1) * nq]

    def diagonal(hh, own, blk):
        hk = tq // 2
        block_step(hh, own, blk, k_len=hk, masked=True)
        block_step(hh, own, blk, q_lo=hk, k_lo=hk, k_len=hk, masked=True)

    for hh in range(hps):
        diagonal(hh, 0, i)
    for t in range(1, nt):
        is_a = t <= i
        for hh in range(hps):
            block_step(hh, jnp.where(is_a, 0, 1), jnp.where(is_a, i - t, t - i - 1))
    for hh in range(hps):
        diagonal(hh, 1, nt - 1 - i)

    lam = (jnp.exp(jnp.sum(lq1_ref[...] * lk1_ref[...], axis=1, keepdims=True))
           - jnp.exp(jnp.sum(lq2_ref[...] * lk2_ref[...], axis=1, keepdims=True)) + LAMBDA_INIT)
    for hh in range(hps):
        cols = slice(hh * HEAD_W, (hh + 1) * HEAD_W)
        for t, o_ref in enumerate((oa_ref, ob_ref)):
            acc = acc_ref[hh, t]
            o = acc[:, 0:HEAD_W] / acc[:, HEAD_W:2 * HEAD_W]
            d = o[0:tq] - lam * o[tq:2 * tq]
            ms = jnp.mean(d * d, axis=1, keepdims=True)
            o_ref[0, :, cols] = (d * lax.rsqrt(ms + LN_EPS) * nw_ref[...] * (1.0 - LAMBDA_INIT)).astype(BF16)


def _diffattn(qk, dv, lq1, lk1, lq2, lk2, nw, tq, hps):
    B, S, _ = qk.shape
    nt = S // tq
    assert nt % 2 == 0, "query tiles are processed in pairs (i, nt-1-i)"
    assert DIFF_HEADS % hps == 0
    half = nt // 2
    groups = DIFF_HEADS // hps
    gw = hps * HEAD_W
    const2 = lambda b, h, i: (0, 0)
    out = jax.ShapeDtypeStruct((B, S // 2, DIFF_HEADS * HEAD_W), BF16)
    lo, hi = pl.pallas_call(
        functools.partial(_diffattn_kernel, tq=tq, nt=nt, hps=hps),
        grid=(B, groups, half),
        in_specs=[pl.BlockSpec((1, tq, gw), lambda b, h, i: (b, i, h)),
                  pl.BlockSpec((1, tq, gw), lambda b, h, i: (b, nt - 1 - i, h)),
                  pl.BlockSpec((1, S, gw), lambda b, h, i: (b, 0, groups + h)),
                  pl.BlockSpec((1, S, gw), lambda b, h, i: (b, 0, h)),
                  pl.BlockSpec((1, DIFF_DH), const2),
                  pl.BlockSpec((1, DIFF_DH), const2),
                  pl.BlockSpec((1, DIFF_DH), const2),
                  pl.BlockSpec((1, DIFF_DH), const2),
                  pl.BlockSpec((1, HEAD_W), const2)],
        out_specs=[pl.BlockSpec((1, tq, gw), lambda b, h, i: (b, i, h)),
                   pl.BlockSpec((1, tq, gw), lambda b, h, i: (b, half - 1 - i, h))],
        out_shape=[out, out],
        scratch_shapes=[pltpu.VMEM((hps, 2, 2 * tq, HEAD_W), BF16),
                        pltpu.VMEM((hps, 2, 2 * tq, LANES), F32),
                        pltpu.VMEM((hps, 2, 2 * tq, 2 * HEAD_W), F32)],
        compiler_params=pltpu.CompilerParams(
            dimension_semantics=("arbitrary", "arbitrary", "arbitrary"), vmem_limit_bytes=VMEM_LIMIT),
        name="diffattn",
    )(qk, qk, qk, dv, lq1, lk1, lq2, lk2, nw)
    return lo, hi


def _gla_tables():
    C = GLA_CHUNK
    t = np.arange(C)
    rng = np.zeros((2 + GLA_LEVELS, C, C), np.float32)
    rng[0] = (t[None, :] <= t[:, None])
    rng[1] = (t[None, :] > t[:, None])
    lvl_mask = np.zeros((GLA_LEVELS + 1, C, C), np.float32)
    for l in range(GLA_LEVELS):
        s = C >> (l + 1)
        blk = t // (2 * s)
        mid = blk * 2 * s + s
        upper = (t % (2 * s)) >= s
        for i in range(C):
            if upper[i]:
                rng[2 + l, i, mid[i]:i + 1] = 1.0
            else:
                rng[2 + l, i, i + 1:mid[i]] = 1.0
        lvl_mask[l] = (blk[:, None] == blk[None, :]) & upper[:, None] & (~upper[None, :])
    lvl_mask[GLA_LEVELS] = np.eye(C)
    rng = rng.reshape((2 + GLA_LEVELS) * C, C)
    rng3 = np.concatenate([rng, rng, rng], axis=1)
    lvl_mask = np.tile(lvl_mask, (1, 1, GLA_HEADS))
    hk = np.kron(np.eye(GLA_HEADS), np.ones((C, GLA_DK)))
    hv = np.kron(np.eye(GLA_HEADS), np.ones((C, GLA_DV)))
    hs = np.kron(np.eye(GLA_HEADS), np.ones((GLA_DV, GLA_DK)))
    return (jnp.asarray(rng3, BF16), jnp.asarray(lvl_mask, F32), jnp.asarray(hk, BF16),
            jnp.asarray(hv, BF16), jnp.asarray(hs, F32))


def _gla_kernel(q_ref, k_ref, v_ref, g_ref, r_ref, rng_ref, lm_ref, tril_ref, hk_ref, hv_ref, hs_ref, nw_ref,
                o_ref, st_ref, *, ts):
    C = GLA_CHUNK

    @pl.when(pl.program_id(1) == 0)
    def _():
        st_ref[...] = jnp.zeros(st_ref.shape, F32)

    def gate_sums(c, mild):
        g = g_ref[0, pl.ds(c * C, C), :]
        g_hi = g.astype(BF16)
        rem = g - g_hi.astype(F32)
        g_mid = rem.astype(BF16)
        g_lo = (rem - g_mid.astype(F32)).astype(BF16)
        g3 = jnp.concatenate([g_hi, g_mid, g_lo], axis=0)
        rng = rng_ref[0:C, :] if mild else rng_ref[...]
        return jnp.dot(rng, g3, preferred_element_type=F32)

    def block_diag(kl):
        return jnp.concatenate([kl] * GLA_HEADS, axis=0) * hk_ref[...]

    def decays_mild(c, b):
        rows = pl.ds(c * C, C)
        q = q_ref[0, rows, :].astype(F32)
        k = k_ref[0, rows, :].astype(F32)
        b_last = b[C - 1:C]
        ref = b[C // 2 - 1:C // 2]
        levels = [((q * jnp.exp(b - ref)).astype(BF16), block_diag((k * jnp.exp(ref - b)).astype(BF16)))]
        return dict(q_in=(q * jnp.exp(b)).astype(BF16),
                    k_out=(k * jnp.exp(b_last - b)).astype(BF16),
                    decay=jnp.exp(b_last),
                    levels=levels)

    def decays(c, e):
        rows = pl.ds(c * C, C)
        q = q_ref[0, rows, :].astype(F32)
        k = k_ref[0, rows, :].astype(F32)
        f = jnp.exp(e)
        levels = []
        for l in range(GLA_LEVELS + 1):
            if l < GLA_LEVELS:
                fl = f[(2 + l) * C:(3 + l) * C]
                ql = (q * fl).astype(BF16)
                kl = (k * fl).astype(BF16)
            else:
                ql = q.astype(BF16)
                kl = k.astype(BF16)
            levels.append((ql, block_diag(kl)))
        return dict(q_in=(q * f[0:C]).astype(BF16),
                    k_out=(k * f[C:2 * C]).astype(BF16),
                    decay=f[C - 1:C],
                    levels=levels)

    def intra(d, mild):
        if mild:
            ql, k_bd = d["levels"][0]
            a = lax.dot_general(ql, k_bd, (((1,), (1,)), ((), ())), preferred_element_type=F32)
            return jnp.where(tril_ref[...] > 0.5, a, 0.0).astype(BF16)
        attn = jnp.zeros((C, GLA_HEADS * C), F32)
        for l, (ql, k_bd) in enumerate(d["levels"]):
            a = lax.dot_general(ql, k_bd, (((1,), (1,)), ((), ())), preferred_element_type=F32)
            attn = attn + a * lm_ref[l]
        return attn.astype(BF16)

    def output(c, d, attn):
        rows = pl.ds(c * C, C)
        v = v_ref[0, rows, :]
        v_bd = jnp.concatenate([v] * GLA_HEADS, axis=0) * hv_ref[...]
        st = st_ref[...]
        o = jnp.dot(attn, v_bd, preferred_element_type=F32)
        o = o + lax.dot_general(d["q_in"], st.astype(BF16), (((1,), (1,)), ((), ())), preferred_element_type=F32)
        upd = lax.dot_general(v, d["k_out"], (((0,), (0,)), ((), ())), preferred_element_type=F32)
        st_ref[...] = st * d["decay"] + upd * hs_ref[...]
        parts = []
        for h in range(GLA_HEADS):
            oh = o[:, h * GLA_DV:(h + 1) * GLA_DV]
            ms = jnp.mean(oh * oh, axis=1, keepdims=True)
            parts.append(oh * lax.rsqrt(ms + LN_EPS) * nw_ref[...])
        on = jnp.concatenate(parts, axis=1)
        o_ref[0, rows, :] = (on * _silu(r_ref[0, rows, :].astype(F32))).astype(BF16)

    n = ts // C

    def run(mild):
        prep = decays_mild if mild else decays
        esum, dec, att = {0: gate_sums(0, mild)}, {}, {}
        for t in range(n + 2):
            if 0 <= t - 1 < n:
                att[t - 1] = intra(dec[t - 1], mild)
            if 0 <= t - 2 < n:
                output(t - 2, dec.pop(t - 2), att.pop(t - 2))
            if t + 1 < n:
                esum[t + 1] = gate_sums(t + 1, mild)
            if t < n:
                dec[t] = prep(t, esum.pop(t))

    totals = [jnp.sum(g_ref[0, pl.ds(c * C, C), :], axis=0, keepdims=True) for c in range(n)]
    mild = jnp.min(jnp.concatenate(totals, axis=0)) >= -GLA_MILD_DECAY

    @pl.when(mild)
    def _():
        run(True)

    @pl.when(jnp.logical_not(mild))
    def _():
        run(False)


def _gla(gq, gk, gv, gl, gr, nw, ts):
    B, S, _ = gq.shape
    rng3, lvl_mask, hk, hv, hs = _gla_tables()
    tril = jnp.sum(lvl_mask, axis=0)
    row = lambda b, s: (b, s, 0)
    const2 = lambda b, s: (0, 0)
    const3 = lambda b, s: (0, 0, 0)
    return pl.pallas_call(
        functools.partial(_gla_kernel, ts=ts),
        grid=(B, S // ts),
        in_specs=[pl.BlockSpec((1, ts, GLA_KW), row),
                  pl.BlockSpec((1, ts, GLA_KW), row),
                  pl.BlockSpec((1, ts, GLA_VW), row),
                  pl.BlockSpec((1, ts, GLA_KW), row),
                  pl.BlockSpec((1, ts, GLA_VW), row),
                  pl.BlockSpec(rng3.shape, const2),
                  pl.BlockSpec(lvl_mask.shape, const3),
                  pl.BlockSpec(tril.shape, const2),
                  pl.BlockSpec(hk.shape, const2),
                  pl.BlockSpec(hv.shape, const2),
                  pl.BlockSpec(hs.shape, const2),
                  pl.BlockSpec((1, GLA_DV), const2)],
        out_specs=pl.BlockSpec((1, ts, GLA_VW), row),
        out_shape=jax.ShapeDtypeStruct((B, S, GLA_VW), BF16),
        scratch_shapes=[pltpu.VMEM((GLA_VW, GLA_KW), F32)],
        compiler_params=pltpu.CompilerParams(
            dimension_semantics=("arbitrary", "arbitrary"), vmem_limit_bytes=VMEM_LIMIT),
        name="gla",
    )(gq, gk, gv, gl, gr, rng3, lvl_mask, tril, hk, hv, hs, nw)


def _mlp_kernel(x0_ref, d0_ref, g0_ref, gta0_ref, xn_ref, dlon_ref, dhin_ref, gn_ref, gtan_ref,
                lg_ref, lb_ref, wod_ref, wog_ref, ag_ref, ab_ref,
                sc_ref, sh_ref, gtf_ref, wu_ref, cw_ref, cb_ref, wd_ref, fg_ref, fb_ref, o_ref,
                carry_ref, act_ref, u_ref, h_ref, y_ref, hp_ref, *, tm, tf, nf, nt, half_tiles):
    SUB = 8
    blk = tm // SUB
    F = nf * tf
    lin = pl.program_id(0)
    s = lin % nt
    nl = h_ref.shape[0]
    hm = tm // 2

    def gather_rows(ref, start, size, stride):
        return jnp.concatenate([ref[c, pl.ds(start, size, stride=stride), :] for c in range(nl)], axis=1)

    def put_rows(ref, rows, val):
        for c in range(nl):
            ref[c, rows, :] = val[:, c * LANES:(c + 1) * LANES]

    def token_mix(x_ref, d_out, g_ref, gta_ref):
        for r in range(2):
            rows = slice(r * hm, (r + 1) * hm)
            h_in = _layer_norm(x_ref[0, rows, :], lg_ref[...], lb_ref[...])
            mix = (jnp.dot(d_out[rows], wod_ref[...], preferred_element_type=F32)
                   + jnp.dot(g_ref[0, rows, :], wog_ref[...], preferred_element_type=F32))
            put_rows(h_ref, rows, _layer_norm(DN_ALPHA * h_in + (1.0 + gta_ref[0]) * mix, ag_ref[...], ab_ref[...]))

    @pl.when(lin == 0)
    def _():
        token_mix(x0_ref, d0_ref[0], g0_ref, gta0_ref)

    @pl.when(s == 0)
    def _():
        carry_ref[...] = jnp.zeros(carry_ref.shape, F32)

    for b in range(SUB):
        hb = gather_rows(h_ref, b, blk, SUB)
        hp_ref[b * blk:(b + 1) * blk, :] = hb
        u_ref[b * blk:(b + 1) * blk, :] = (hb * (1.0 + sc_ref[0]) + sh_ref[0]).astype(BF16)

    def shift_rows(block, before):
        return pltpu.roll(jnp.concatenate([before, block], axis=0), 1, axis=0)[SUB:SUB + blk]

    def conv_half(half, f):
        cols = slice(half * F + f * tf, half * F + (f + 1) * tf)
        up = jnp.dot(u_ref[...], wu_ref[:, cols], preferred_element_type=F32)
        b6 = shift_rows(up[6 * blk:7 * blk], carry_ref[half, f, 0:SUB])
        b7 = shift_rows(up[7 * blk:8 * blk], carry_ref[half, f, SUB:2 * SUB])
        carry_ref[half, f, 0:SUB] = up[7 * blk - SUB:7 * blk]
        carry_ref[half, f, SUB:2 * SUB] = up[8 * blk - SUB:8 * blk]
        back1 = jnp.concatenate([b7, up[0:7 * blk]], axis=0)
        back2 = jnp.concatenate([b6, b7, up[0:6 * blk]], axis=0)
        cw = cw_ref[:, cols]
        return cb_ref[:, cols] + cw[0:1] * back2 + cw[1:2] * back1 + cw[2:3] * up

    for f in range(nf):
        act_ref[f] = (_silu(conv_half(0, f)) * conv_half(1, f)).astype(BF16)

    for r in range(2):
        rows = slice(r * hm, (r + 1) * hm)
        ff = jnp.dot(act_ref[0, rows, :], wd_ref[0], preferred_element_type=F32)
        for f in range(1, nf):
            ff = ff + jnp.dot(act_ref[f, rows, :], wd_ref[f], preferred_element_type=F32)
        for j in range(hm // blk):
            b = r * (hm // blk) + j
            y = DN_ALPHA * hp_ref[b * blk:(b + 1) * blk, :] + (1.0 + gtf_ref[0]) * ff[j * blk:(j + 1) * blk]
            put_rows(y_ref, pl.ds(b, blk, stride=SUB), _layer_norm(y, fg_ref[...], fb_ref[...]))

    o_ref[0] = jnp.concatenate([y_ref[c] for c in range(nl)], axis=1)

    s_next = (lin + 1) % nt
    token_mix(xn_ref, jnp.where(s_next < half_tiles, dlon_ref[0], dhin_ref[0]), gn_ref, gtan_ref)


def _mlp(x, d_lo, d_hi, g_out, ln_g, ln_b, gt_a, w_od, w_og, ag, ab, sc, sh, gt_f, wu, cw, cb, wd, fg, fb, tm):
    B, S, D = x.shape
    nf, tf, _ = wd.shape
    nt = S // tm
    half_tiles = d_lo.shape[1] // tm
    last = B * nt - 1

    def tile(lin):
        return lin // nt, lin % nt

    def nxt(lin):
        return tile(jnp.minimum(lin + 1, last))

    first = lambda lin: (0, 0, 0)

    def resident(shape):
        return pl.BlockSpec(shape, lambda lin: (0,) * len(shape), pipeline_mode=pl.Buffered(1))

    return pl.pallas_call(
        functools.partial(_mlp_kernel, tm=tm, tf=tf, nf=nf, nt=nt, half_tiles=half_tiles),
        grid=(B * nt,),
        in_specs=[pl.BlockSpec((1, tm, D), first),
                  pl.BlockSpec((1, tm, d_lo.shape[2]), first),
                  pl.BlockSpec((1, tm, g_out.shape[2]), first),
                  pl.BlockSpec((1, 1, D), first),
                  pl.BlockSpec((1, tm, D), lambda lin: (nxt(lin)[0], nxt(lin)[1], 0)),
                  pl.BlockSpec((1, tm, d_lo.shape[2]),
                               lambda lin: (nxt(lin)[0], jnp.minimum(nxt(lin)[1], half_tiles - 1), 0)),
                  pl.BlockSpec((1, tm, d_hi.shape[2]),
                               lambda lin: (nxt(lin)[0], jnp.maximum(nxt(lin)[1] - half_tiles, 0), 0)),
                  pl.BlockSpec((1, tm, g_out.shape[2]), lambda lin: (nxt(lin)[0], nxt(lin)[1], 0)),
                  pl.BlockSpec((1, 1, D), lambda lin: (nxt(lin)[0], 0, 0)),
                  resident((1, D)), resident((1, D)),
                  resident(w_od.shape), resident(w_og.shape),
                  resident((1, D)), resident((1, D)),
                  pl.BlockSpec((1, 1, D), lambda lin: (tile(lin)[0], 0, 0)),
                  pl.BlockSpec((1, 1, D), lambda lin: (tile(lin)[0], 0, 0)),
                  pl.BlockSpec((1, 1, D), lambda lin: (tile(lin)[0], 0, 0)),
                  resident(wu.shape), resident(cw.shape), resident(cb.shape), resident(wd.shape),
                  resident((1, D)), resident((1, D))],
        out_specs=pl.BlockSpec((1, tm, D), lambda lin: (tile(lin)[0], tile(lin)[1], 0)),
        out_shape=jax.ShapeDtypeStruct((B, S, D), F32),
        scratch_shapes=[pltpu.VMEM((2, nf, 16, tf), F32),
                        pltpu.VMEM((nf, tm, tf), BF16),
                        pltpu.VMEM((tm, D), BF16),
                        pltpu.VMEM((D // LANES, tm, LANES), F32),
                        pltpu.VMEM((D // LANES, tm, LANES), F32),
                        pltpu.VMEM((tm, D), F32)],
        compiler_params=pltpu.CompilerParams(
            dimension_semantics=("arbitrary",), vmem_limit_bytes=VMEM_LIMIT),
        name="mlp",
    )(x, d_lo, g_out, gt_a, x, d_lo, d_hi, g_out, gt_a, ln_g, ln_b, w_od, w_og, ag, ab, sc, sh, gt_f,
      wu, cw, cb, wd, fg, fb)


def kernel(x, c, positions, ln_in_g, ln_in_b, w_ada, b_ada, w_in, lambda_q1, lambda_k1, lambda_q2, lambda_k2, diff_norm_w, gla_w_gate_up, gla_b_gate, gla_norm_w, w_out, ln_attn_g, ln_attn_b, w_up, conv_w, conv_b, w_down, ln_ffn_g, ln_ffn_b):
    B, S, D = x.shape
    assert D == D_MODEL and w_ada.shape[0] == 1
    tm = min(512, S)
    tf = 256
    nf = D_FF // tf

    c_pad = jnp.pad(c, ((0, 8 - B % 8 if B % 8 else 0), (0, 0)))
    ada = _ada(c_pad, w_ada[0], b_ada)[:B]
    sh_a, sc_a, gt_a, sh_f, sc_f, gt_f = [t[:, None, :] for t in jnp.split(ada, 6, axis=-1)]

    ln_g = ln_in_g[None, :]
    ln_b = ln_in_b[None, :]

    w_b = w_in[0].astype(BF16)
    w_proj = jnp.concatenate([w_b[:, :N_MAIN], jnp.pad(w_b[:, N_MAIN:], ((0, 0), (0, N_PROJ - w_b.shape[1])))], axis=1)
    w_gate = jnp.pad(gla_w_gate_up[0], ((0, LANES - GLA_RANK), (0, 0))).astype(BF16)
    inv = ROPE_THETA ** (-jnp.arange(0, DIFF_DH, 2, dtype=F32) / DIFF_DH)
    qk, dv, gq, gk, gv, gr, gl = _proj(x, positions[:, None, :], inv[:, None], ln_g, ln_b, sc_a, sh_a,
                                       w_proj, w_gate, gla_b_gate, tm)

    d_lo, d_hi = _diffattn(qk, dv, lambda_q1, lambda_k1, lambda_q2, lambda_k2, diff_norm_w, tm, 2)
    g_out = _gla(gq, gk, gv, gl, gr, gla_norm_w, min(512, S))

    w_o = w_out[0].astype(BF16)
    dw = d_lo.shape[2]
    wd = w_down[0].astype(BF16).reshape(nf, tf, D)
    return _mlp(x, d_lo, d_hi, g_out, ln_g, ln_b, gt_a, w_o[:dw], w_o[dw:], ln_attn_g, ln_attn_b,
                sc_f, sh_f, gt_f, w_up[0].astype(BF16), conv_w[0], conv_b, wd, ln_ffn_g, ln_ffn_b, tm)
```

```python
import functools
import math

import numpy as np
import jax
import jax.numpy as jnp
from jax import lax
from jax.experimental import pallas as pl
from jax.experimental.pallas import tpu as pltpu

F32 = jnp.float32
BF16 = jnp.bfloat16

D_MODEL = 1024
DIFF_DH = 64
DIFF_HEADS = 4
HEAD_W = 2 * DIFF_DH
GLA_HEADS = 4
GLA_DK = 64
GLA_DV = 128
GLA_KW = GLA_HEADS * GLA_DK
GLA_VW = GLA_HEADS * GLA_DV
GLA_RANK = 16
GLA_TAU = 16.0
GLA_CHUNK = 64
GLA_LEVELS = 6
GLA_MILD_DECAY = 30.0
D_FF = 2816
CONV_W = 3
ROPE_THETA = 10000.0
LN_EPS = 1e-5
DEPTH = 1
DN_ALPHA = (2.0 * DEPTH) ** 0.25
LAMBDA_INIT = 0.8 - 0.6 * math.exp(-0.3 * 0)

N_MAIN = 3072
N_PROJ = N_MAIN + 128
LANES = 128
NEG_BIG = -1e30
LOG2_E = math.log2(math.e)

VMEM_LIMIT = 56 * 1024 * 1024


def _layer_norm(x, g, b):
    mu = jnp.mean(x, axis=-1, keepdims=True)
    xc = x - mu
    var = jnp.mean(xc * xc, axis=-1, keepdims=True)
    return xc * lax.rsqrt(var + LN_EPS) * g + b


def _silu(x):
    return x * jax.nn.sigmoid(x)


def _ada_kernel(c_ref, w_ref, b_ref, o_ref):
    ca = _silu(c_ref[...])
    o_ref[...] = jnp.dot(ca.astype(BF16), w_ref[...].astype(BF16),
                         preferred_element_type=F32) + b_ref[...]


def _ada(c_pad, w_ada, b_ada):
    rows, d = c_pad.shape
    n = w_ada.shape[1]
    tn = 1024
    return pl.pallas_call(
        _ada_kernel,
        grid=(n // tn,),
        in_specs=[pl.BlockSpec((rows, d), lambda j: (0, 0)),
                  pl.BlockSpec((d, tn), lambda j: (0, j)),
                  pl.BlockSpec((1, tn), lambda j: (0, j))],
        out_specs=pl.BlockSpec((rows, tn), lambda j: (0, j)),
        out_shape=jax.ShapeDtypeStruct((rows, n), F32),
        name="ada",
    )(c_pad, w_ada, b_ada)


def _proj_kernel(x_ref, pos_ref, inv_ref, lg_ref, lb_ref, sc_ref, sh_ref, w_ref, wg_ref, bg_ref,
                 qk_ref, dv_ref, gq_ref, gk_ref, gv_ref, gr_ref, gl_ref, mild_ref, *, tm):
    ang_t = inv_ref[...] * pos_ref[0].astype(F32)
    reps = LANES // ang_t.shape[0]
    cos = jnp.concatenate([jnp.cos(ang_t)] * reps, axis=0).T
    sin = jnp.concatenate([jnp.sin(ang_t)] * reps, axis=0).T
    lane = lax.broadcasted_iota(jnp.int32, (1, LANES), 1)
    first_half = (lane & 32) == 0
    sin_signed = jnp.where(first_half, -sin, sin)
    scale = LOG2_E / math.sqrt(DIFF_DH)

    hm = tm // 2
    worst = None
    for part in range(2):
        rows = slice(part * hm, (part + 1) * hm)
        h = _layer_norm(x_ref[0, rows, :], lg_ref[...], lb_ref[...])
        u = (h * (1.0 + sc_ref[0]) + sh_ref[0]).astype(BF16)
        proj = jnp.dot(u, w_ref[...], preferred_element_type=F32)
        for j in range(8):
            xg = proj[:, j * LANES:(j + 1) * LANES]
            partner = jnp.where(first_half, pltpu.roll(xg, LANES - 32, axis=1), pltpu.roll(xg, 32, axis=1))
            r = xg * cos[rows] + partner * sin_signed[rows]
            if j < 4:
                r = r * scale
            qk_ref[0, rows, j * LANES:(j + 1) * LANES] = r.astype(BF16)
        dv_ref[0, rows, :] = proj[:, 1024:1536].astype(BF16)
        gq_ref[0, rows, :] = (proj[:, 1536:1792] * (GLA_DK ** -0.5)).astype(BF16)
        gk_ref[0, rows, :] = proj[:, 1792:2048].astype(BF16)
        gv_ref[0, rows, :] = proj[:, 2048:2560].astype(BF16)
        gr_ref[0, rows, :] = proj[:, 2560:3072].astype(BF16)
        gg = proj[:, N_MAIN:N_PROJ].astype(BF16)
        z = jnp.dot(gg, wg_ref[...], preferred_element_type=F32) + bg_ref[...]
        log_sig = jnp.minimum(z, 0.0) - jnp.log1p(jnp.exp(-jnp.abs(z)))
        log_gate = log_sig * (1.0 / GLA_TAU)
        gl_ref[0, rows, :] = log_gate
        for c in range(hm // GLA_CHUNK):
            total = jnp.sum(log_gate[c * GLA_CHUNK:(c + 1) * GLA_CHUNK], axis=0, keepdims=True)
            worst = total if worst is None else jnp.minimum(worst, total)
    mild = jnp.min(worst, axis=1, keepdims=True) >= -GLA_MILD_DECAY
    mild_ref[0, 0] = jnp.broadcast_to(mild.astype(jnp.int32), mild_ref.shape[2:])


def _proj(x, pos_row, inv_col, ln_g, ln_b, sc, sh, w, wg, bg, tm):
    B, S, D = x.shape
    grid = (B, S // tm)
    row = lambda b, s: (b, s, 0)
    const2 = lambda b, s: (0, 0)
    per_b = lambda b, s: (b, 0, 0)
    widths = (1024, 512, GLA_KW, GLA_KW, GLA_VW, GLA_VW)
    out_shape = [jax.ShapeDtypeStruct((B, S, n), BF16) for n in widths]
    out_shape.append(jax.ShapeDtypeStruct((B, S, GLA_KW), F32))
    out_specs = [pl.BlockSpec((1, tm, n), row) for n in widths]
    out_specs.append(pl.BlockSpec((1, tm, GLA_KW), row))
    out_shape.append(jax.ShapeDtypeStruct((B, S // tm, 8, LANES), jnp.int32))
    out_specs.append(pl.BlockSpec((1, 1, 8, LANES), lambda b, s: (b, s, 0, 0)))
    return pl.pallas_call(
        functools.partial(_proj_kernel, tm=tm),
        grid=grid,
        in_specs=[pl.BlockSpec((1, tm, D), row),
                  pl.BlockSpec((1, 1, tm), lambda b, s: (b, 0, s)),
                  pl.BlockSpec(inv_col.shape, const2),
                  pl.BlockSpec((1, D), const2),
                  pl.BlockSpec((1, D), const2),
                  pl.BlockSpec((1, 1, D), per_b),
                  pl.BlockSpec((1, 1, D), per_b),
                  pl.BlockSpec((D, N_PROJ), const2),
                  pl.BlockSpec((LANES, GLA_KW), const2),
                  pl.BlockSpec((1, GLA_KW), const2)],
        out_specs=out_specs,
        out_shape=out_shape,
        compiler_params=pltpu.CompilerParams(
            dimension_semantics=("arbitrary", "arbitrary"), vmem_limit_bytes=VMEM_LIMIT),
        name="proj",
    )(x, pos_row, inv_col, ln_g, ln_b, sc, sh, w, wg, bg)


def _diffattn_kernel(qa_ref, qb_ref, k_ref, v_ref, lq1_ref, lk1_ref, lq2_ref, lk2_ref, nw_ref,
                     oa_ref, ob_ref, qz_ref, m_ref, acc_ref, *, tq, nt, hps):
    i = pl.program_id(2)
    lane = lax.broadcasted_iota(jnp.int32, (1, HEAD_W), 1)
    for hh in range(hps):
        cols = slice(hh * HEAD_W, (hh + 1) * HEAD_W)
        for t, q_ref in enumerate((qa_ref, qb_ref)):
            q = q_ref[0, :, cols]
            zero = jnp.zeros_like(q)
            qz_ref[hh, t] = jnp.concatenate([jnp.where(lane < DIFF_DH, q, zero),
                                             jnp.where(lane >= DIFF_DH, q, zero)], axis=0)
    m_ref[...] = jnp.full(m_ref.shape, NEG_BIG, F32)
    acc_ref[...] = jnp.zeros(acc_ref.shape, F32)
    ones = jnp.ones((tq, LANES), BF16)

    def block_step(hh, own, blk, q_lo=0, k_lo=0, k_len=None, masked=False):
        k_len = tq if k_len is None else k_len
        nq = tq - q_lo
        cols = slice(hh * HEAD_W, (hh + 1) * HEAD_W)
        rows = pl.ds(pl.multiple_of(blk * tq + k_lo, k_len), k_len)
        spans = (slice(q_lo, tq), slice(tq + q_lo, 2 * tq))

        def load(ref):
            return jnp.concatenate([ref[hh, own, sp, :] for sp in spans], axis=0)

        kb = k_ref[0, rows, cols]
        vb = jnp.concatenate([v_ref[0, rows, cols], ones[0:k_len]], axis=1)
        s = lax.dot_general(load(qz_ref), kb, (((1,), (1,)), ((), ())), preferred_element_type=F32)
        if masked:
            r = lax.broadcasted_iota(jnp.int32, (2 * nq, k_len), 0)
            c = lax.broadcasted_iota(jnp.int32, (2 * nq, k_len), 1)
            r = jnp.where(r >= nq, r - nq, r)
            s = jnp.where(c + k_lo <= r + q_lo, s, NEG_BIG)
        s_fold = s[:, 0:LANES]
        for t in range(1, k_len // LANES):
            s_fold = jnp.maximum(s_fold, s[:, t * LANES:(t + 1) * LANES])
        m_prev = load(m_ref)
        m_new = jnp.maximum(m_prev, jnp.max(s_fold, axis=1, keepdims=True))
        alpha = jnp.exp2(m_prev - m_new)
        p = jnp.exp2(s - jnp.concatenate([m_new] * (k_len // LANES), axis=1))
        acc = (jnp.concatenate([alpha, alpha], axis=1) * load(acc_ref)
               + jnp.dot(p.astype(BF16), vb, preferred_element_type=F32))
        for n, sp in enumerate(spans):
            acc_ref[hh, own, sp, :] = acc[n * nq:(n + 1) * nq]
            m_ref[hh, own, sp, :] = m_new[n * nq:(n + 1) * nq]

    def diagonal(hh, own, blk):
        hk = tq // 2
        block_step(hh, own, blk, k_len=hk, masked=True)
        block_step(hh, own, blk, q_lo=hk, k_lo=hk, k_len=hk, masked=True)

    for hh in range(hps):
        diagonal(hh, 0, i)
    for t in range(1, nt):
        is_a = t <= i
        for hh in range(hps):
            block_step(hh, jnp.where(is_a, 0, 1), jnp.where(is_a, i - t, t - i - 1))
    for hh in range(hps):
        diagonal(hh, 1, nt - 1 - i)

    lam = (jnp.exp(jnp.sum(lq1_ref[...] * lk1_ref[...], axis=1, keepdims=True))
           - jnp.exp(jnp.sum(lq2_ref[...] * lk2_ref[...], axis=1, keepdims=True)) + LAMBDA_INIT)
    for hh in range(hps):
        cols = slice(hh * HEAD_W, (hh + 1) * HEAD_W)
        for t, o_ref in enumerate((oa_ref, ob_ref)):
            acc = acc_ref[hh, t]
            o = acc[:, 0:HEAD_W] / acc[:, HEAD_W:2 * HEAD_W]
            d = o[0:tq] - lam * o[tq:2 * tq]
            ms = jnp.mean(d * d, axis=1, keepdims=True)
            o_ref[0, :, cols] = (d * lax.rsqrt(ms + LN_EPS) * nw_ref[...] * (1.0 - LAMBDA_INIT)).astype(BF16)


def _diffattn(qk, dv, lq1, lk1, lq2, lk2, nw, tq, hps):
    B, S, _ = qk.shape
    nt = S // tq
    assert nt % 2 == 0, "query tiles are processed in pairs (i, nt-1-i)"
    assert DIFF_HEADS % hps == 0
    half = nt // 2
    groups = DIFF_HEADS // hps
    gw = hps * HEAD_W
    const2 = lambda b, h, i: (0, 0)
    out = jax.ShapeDtypeStruct((B, S // 2, DIFF_HEADS * HEAD_W), BF16)
    lo, hi = pl.pallas_call(
        functools.partial(_diffattn_kernel, tq=tq, nt=nt, hps=hps),
        grid=(B, groups, half),
        in_specs=[pl.BlockSpec((1, tq, gw), lambda b, h, i: (b, i, h)),
                  pl.BlockSpec((1, tq, gw), lambda b, h, i: (b, nt - 1 - i, h)),
                  pl.BlockSpec((1, S, gw), lambda b, h, i: (b, 0, groups + h)),
                  pl.BlockSpec((1, S, gw), lambda b, h, i: (b, 0, h)),
                  pl.BlockSpec((1, DIFF_DH), const2),
                  pl.BlockSpec((1, DIFF_DH), const2),
                  pl.BlockSpec((1, DIFF_DH), const2),
                  pl.BlockSpec((1, DIFF_DH), const2),
                  pl.BlockSpec((1, HEAD_W), const2)],
        out_specs=[pl.BlockSpec((1, tq, gw), lambda b, h, i: (b, i, h)),
                   pl.BlockSpec((1, tq, gw), lambda b, h, i: (b, half - 1 - i, h))],
        out_shape=[out, out],
        scratch_shapes=[pltpu.VMEM((hps, 2, 2 * tq, HEAD_W), BF16),
                        pltpu.VMEM((hps, 2, 2 * tq, LANES), F32),
                        pltpu.VMEM((hps, 2, 2 * tq, 2 * HEAD_W), F32)],
        compiler_params=pltpu.CompilerParams(
            dimension_semantics=("arbitrary", "arbitrary", "arbitrary"), vmem_limit_bytes=VMEM_LIMIT),
        name="diffattn",
    )(qk, qk, qk, dv, lq1, lk1, lq2, lk2, nw)
    return lo, hi


def _gla_tables():
    C = GLA_CHUNK
    t = np.arange(C)
    rng = np.zeros((2 + GLA_LEVELS, C, C), np.float32)
    rng[0] = (t[None, :] <= t[:, None])
    rng[1] = (t[None, :] > t[:, None])
    lvl_mask = np.zeros((GLA_LEVELS + 1, C, C), np.float32)
    for l in range(GLA_LEVELS):
        s = C >> (l + 1)
        blk = t // (2 * s)
        mid = blk * 2 * s + s
        upper = (t % (2 * s)) >= s
        for i in range(C):
            if upper[i]:
                rng[2 + l, i, mid[i]:i + 1] = 1.0
            else:
                rng[2 + l, i, i + 1:mid[i]] = 1.0
        lvl_mask[l] = (blk[:, None] == blk[None, :]) & upper[:, None] & (~upper[None, :])
    lvl_mask[GLA_LEVELS] = np.eye(C)
    rng = rng.reshape((2 + GLA_LEVELS) * C, C)
    rng3 = np.concatenate([rng, rng, rng], axis=1)
    lvl_mask = np.tile(lvl_mask, (1, 1, GLA_HEADS))
    hk = np.kron(np.eye(GLA_HEADS), np.ones((C, GLA_DK)))
    hv = np.kron(np.eye(GLA_HEADS), np.ones((C, GLA_DV)))
    hs = np.kron(np.eye(GLA_HEADS), np.ones((GLA_DV, GLA_DK)))
    return (jnp.asarray(rng3, BF16), jnp.asarray(lvl_mask, F32), jnp.asarray(hk, BF16),
            jnp.asarray(hv, BF16), jnp.asarray(hs, F32))


def _gla_kernel(mild_ref, q_ref, k_ref, v_ref, g_ref, r_ref, rng_ref, lm_ref, tril_ref, hk_ref, hv_ref, hs_ref,
                nw_ref, o_ref, st_ref, *, ts):
    C = GLA_CHUNK

    @pl.when(pl.program_id(1) == 0)
    def _():
        st_ref[...] = jnp.zeros(st_ref.shape, F32)

    def gate_sums(c, mild):
        g = g_ref[0, pl.ds(c * C, C), :]
        g_hi = g.astype(BF16)
        rem = g - g_hi.astype(F32)
        g_mid = rem.astype(BF16)
        g_lo = (rem - g_mid.astype(F32)).astype(BF16)
        g3 = jnp.concatenate([g_hi, g_mid, g_lo], axis=0)
        rng = rng_ref[0:C, :] if mild else rng_ref[...]
        return jnp.dot(rng, g3, preferred_element_type=F32)

    def block_diag(kl):
        return jnp.concatenate([kl] * GLA_HEADS, axis=0) * hk_ref[...]

    def decays_mild(c, b):
        rows = pl.ds(c * C, C)
        q = q_ref[0, rows, :].astype(F32)
        k = k_ref[0, rows, :].astype(F32)
        b_last = b[C - 1:C]
        ref = b[C // 2 - 1:C // 2]
        levels = [((q * jnp.exp(b - ref)).astype(BF16), block_diag((k * jnp.exp(ref - b)).astype(BF16)))]
        return dict(q_in=(q * jnp.exp(b)).astype(BF16),
                    k_out=(k * jnp.exp(b_last - b)).astype(BF16),
                    decay=jnp.exp(b_last),
                    levels=levels)

    def decays(c, e):
        rows = pl.ds(c * C, C)
        q = q_ref[0, rows, :].astype(F32)
        k = k_ref[0, rows, :].astype(F32)
        f = jnp.exp(e)
        levels = []
        for l in range(GLA_LEVELS + 1):
            if l < GLA_LEVELS:
                fl = f[(2 + l) * C:(3 + l) * C]
                ql = (q * fl).astype(BF16)
                kl = (k * fl).astype(BF16)
            else:
                ql = q.astype(BF16)
                kl = k.astype(BF16)
            levels.append((ql, block_diag(kl)))
        return dict(q_in=(q * f[0:C]).astype(BF16),
                    k_out=(k * f[C:2 * C]).astype(BF16),
                    decay=f[C - 1:C],
                    levels=levels)

    def intra(d, mild):
        if mild:
            ql, k_bd = d["levels"][0]
            a = lax.dot_general(ql, k_bd, (((1,), (1,)), ((), ())), preferred_element_type=F32)
            return jnp.where(tril_ref[...] > 0.5, a, 0.0).astype(BF16)
        attn = jnp.zeros((C, GLA_HEADS * C), F32)
        for l, (ql, k_bd) in enumerate(d["levels"]):
            a = lax.dot_general(ql, k_bd, (((1,), (1,)), ((), ())), preferred_element_type=F32)
            attn = attn + a * lm_ref[l]
        return attn.astype(BF16)

    def output(c, d, attn):
        rows = pl.ds(c * C, C)
        v = v_ref[0, rows, :]
        v_bd = jnp.concatenate([v] * GLA_HEADS, axis=0) * hv_ref[...]
        st = st_ref[...]
        o = jnp.dot(attn, v_bd, preferred_element_type=F32)
        o = o + lax.dot_general(d["q_in"], st.astype(BF16), (((1,), (1,)), ((), ())), preferred_element_type=F32)
        upd = lax.dot_general(v, d["k_out"], (((0,), (0,)), ((), ())), preferred_element_type=F32)
        st_ref[...] = st * d["decay"] + upd * hs_ref[...]
        parts = []
        for h in range(GLA_HEADS):
            oh = o[:, h * GLA_DV:(h + 1) * GLA_DV]
            ms = jnp.mean(oh * oh, axis=1, keepdims=True)
            parts.append(oh * lax.rsqrt(ms + LN_EPS) * nw_ref[...])
        on = jnp.concatenate(parts, axis=1)
        o_ref[0, rows, :] = (on * _silu(r_ref[0, rows, :].astype(F32))).astype(BF16)

    n = ts // C

    def run(mild):
        prep = decays_mild if mild else decays
        esum, dec, att = {0: gate_sums(0, mild)}, {}, {}
        for t in range(n + 2):
            if 0 <= t - 1 < n:
                att[t - 1] = intra(dec[t - 1], mild)
            if 0 <= t - 2 < n:
                output(t - 2, dec.pop(t - 2), att.pop(t - 2))
            if t + 1 < n:
                esum[t + 1] = gate_sums(t + 1, mild)
            if t < n:
                dec[t] = prep(t, esum.pop(t))

    mild = mild_ref[pl.program_id(0), pl.program_id(1)] != 0

    @pl.when(mild)
    def _():
        run(True)

    @pl.when(jnp.logical_not(mild))
    def _():
        run(False)


def _gla(mild, gq, gk, gv, gl, gr, nw, ts):
    B, S, _ = gq.shape
    assert mild.shape == (B, S // ts)
    rng3, lvl_mask, hk, hv, hs = _gla_tables()
    tril = jnp.sum(lvl_mask, axis=0)
    row = lambda b, s, mild_ref: (b, s, 0)
    const2 = lambda b, s, mild_ref: (0, 0)
    const3 = lambda b, s, mild_ref: (0, 0, 0)
    grid_spec = pltpu.PrefetchScalarGridSpec(
        num_scalar_prefetch=1,
        grid=(B, S // ts),
        in_specs=[pl.BlockSpec((1, ts, GLA_KW), row),
                  pl.BlockSpec((1, ts, GLA_KW), row),
                  pl.BlockSpec((1, ts, GLA_VW), row),
                  pl.BlockSpec((1, ts, GLA_KW), row),
                  pl.BlockSpec((1, ts, GLA_VW), row),
                  pl.BlockSpec(rng3.shape, const2),
                  pl.BlockSpec(lvl_mask.shape, const3),
                  pl.BlockSpec(tril.shape, const2),
                  pl.BlockSpec(hk.shape, const2),
                  pl.BlockSpec(hv.shape, const2),
                  pl.BlockSpec(hs.shape, const2),
                  pl.BlockSpec((1, GLA_DV), const2)],
        out_specs=pl.BlockSpec((1, ts, GLA_VW), row),
        scratch_shapes=[pltpu.VMEM((GLA_VW, GLA_KW), F32)])
    return pl.pallas_call(
        functools.partial(_gla_kernel, ts=ts),
        grid_spec=grid_spec,
        out_shape=jax.ShapeDtypeStruct((B, S, GLA_VW), BF16),
        compiler_params=pltpu.CompilerParams(
            dimension_semantics=("arbitrary", "arbitrary"), vmem_limit_bytes=VMEM_LIMIT),
        name="gla",
    )(mild, gq, gk, gv, gl, gr, rng3, lvl_mask, tril, hk, hv, hs, nw)


def _mlp_kernel(x0_ref, d0_ref, g0_ref, gta0_ref, xn_ref, dlon_ref, dhin_ref, gn_ref, gtan_ref,
                lg_ref, lb_ref, wo_ref, ag_ref, ab_ref,
                sc_ref, sh_ref, gtf_ref, wu_ref, cw_ref, cb_ref, wd_ref, fg_ref, fb_ref, o_ref,
                carry_ref, act_ref, u_ref, h_ref, y_ref, hp_ref, *, tm, tf, nf, nt, half_tiles):
    SUB = 8
    blk = tm // SUB
    F = nf * tf
    lin = pl.program_id(0)
    s = lin % nt
    nl = h_ref.shape[0]
    hm = tm // 2

    def gather_rows(ref, start, size, stride):
        return jnp.concatenate([ref[c, pl.ds(start, size, stride=stride), :] for c in range(nl)], axis=1)

    def put_rows(ref, rows, val):
        for c in range(nl):
            ref[c, rows, :] = val[:, c * LANES:(c + 1) * LANES]

    def token_mix(x_ref, d_out, g_ref, gta_ref):
        for r in range(2):
            rows = slice(r * hm, (r + 1) * hm)
            h_in = _layer_norm(x_ref[0, rows, :], lg_ref[...], lb_ref[...])
            dw = d_out.shape[1]
            mix = (jnp.dot(d_out[rows], wo_ref[0:dw, :], preferred_element_type=F32)
                   + jnp.dot(g_ref[0, rows, :], wo_ref[dw:, :], preferred_element_type=F32))
            put_rows(h_ref, rows, _layer_norm(DN_ALPHA * h_in + (1.0 + gta_ref[0]) * mix, ag_ref[...], ab_ref[...]))

    @pl.when(lin == 0)
    def _():
        token_mix(x0_ref, d0_ref[0], g0_ref, gta0_ref)

    @pl.when(s == 0)
    def _():
        carry_ref[...] = jnp.zeros(carry_ref.shape, F32)

    for b in range(SUB):
        hb = gather_rows(h_ref, b, blk, SUB)
        hp_ref[b * blk:(b + 1) * blk, :] = hb
        u_ref[b * blk:(b + 1) * blk, :] = (hb * (1.0 + sc_ref[0]) + sh_ref[0]).astype(BF16)

    def shift_rows(block, before):
        return pltpu.roll(jnp.concatenate([before, block], axis=0), 1, axis=0)[SUB:SUB + blk]

    def conv_half(half, f):
        cols = slice(half * F + f * tf, half * F + (f + 1) * tf)
        up = jnp.dot(u_ref[...], wu_ref[:, cols], preferred_element_type=F32)
        b6 = shift_rows(up[6 * blk:7 * blk], carry_ref[half, f, 0:SUB])
        b7 = shift_rows(up[7 * blk:8 * blk], carry_ref[half, f, SUB:2 * SUB])
        carry_ref[half, f, 0:SUB] = up[7 * blk - SUB:7 * blk]
        carry_ref[half, f, SUB:2 * SUB] = up[8 * blk - SUB:8 * blk]
        back1 = jnp.concatenate([b7, up[0:7 * blk]], axis=0)
        back2 = jnp.concatenate([b6, b7, up[0:6 * blk]], axis=0)
        cw = cw_ref[:, cols]
        return cb_ref[:, cols] + cw[0:1] * back2 + cw[1:2] * back1 + cw[2:3] * up

    for f in range(nf):
        act_ref[f] = (_silu(conv_half(0, f)) * conv_half(1, f)).astype(BF16)

    for r in range(2):
        rows = slice(r * hm, (r + 1) * hm)
        ff = jnp.dot(act_ref[0, rows, :], wd_ref[0], preferred_element_type=F32)
        for f in range(1, nf):
            ff = ff + jnp.dot(act_ref[f, rows, :], wd_ref[f], preferred_element_type=F32)
        for j in range(hm // blk):
            b = r * (hm // blk) + j
            y = DN_ALPHA * hp_ref[b * blk:(b + 1) * blk, :] + (1.0 + gtf_ref[0]) * ff[j * blk:(j + 1) * blk]
            put_rows(y_ref, pl.ds(b, blk, stride=SUB), _layer_norm(y, fg_ref[...], fb_ref[...]))

    o_ref[0] = jnp.concatenate([y_ref[c] for c in range(nl)], axis=1)

    s_next = (lin + 1) % nt
    token_mix(xn_ref, jnp.where(s_next < half_tiles, dlon_ref[0], dhin_ref[0]), gn_ref, gtan_ref)


def _mlp(x, d_lo, d_hi, g_out, ln_g, ln_b, gt_a, w_o, ag, ab, sc, sh, gt_f, wu, cw, cb, wd, fg, fb, tm):
    B, S, D = x.shape
    nf, tf, _ = wd.shape
    nt = S // tm
    half_tiles = d_lo.shape[1] // tm
    last = B * nt - 1

    def tile(lin):
        return lin // nt, lin % nt

    def nxt(lin):
        return tile(jnp.minimum(lin + 1, last))

    first = lambda lin: (0, 0, 0)

    def resident(shape):
        return pl.BlockSpec(shape, lambda lin: (0,) * len(shape), pipeline_mode=pl.Buffered(1))

    return pl.pallas_call(
        functools.partial(_mlp_kernel, tm=tm, tf=tf, nf=nf, nt=nt, half_tiles=half_tiles),
        grid=(B * nt,),
        in_specs=[pl.BlockSpec((1, tm, D), first),
                  pl.BlockSpec((1, tm, d_lo.shape[2]), first),
                  pl.BlockSpec((1, tm, g_out.shape[2]), first),
                  pl.BlockSpec((1, 1, D), first),
                  pl.BlockSpec((1, tm, D), lambda lin: (nxt(lin)[0], nxt(lin)[1], 0)),
                  pl.BlockSpec((1, tm, d_lo.shape[2]),
                               lambda lin: (nxt(lin)[0], jnp.minimum(nxt(lin)[1], half_tiles - 1), 0)),
                  pl.BlockSpec((1, tm, d_hi.shape[2]),
                               lambda lin: (nxt(lin)[0], jnp.maximum(nxt(lin)[1] - half_tiles, 0), 0)),
                  pl.BlockSpec((1, tm, g_out.shape[2]), lambda lin: (nxt(lin)[0], nxt(lin)[1], 0)),
                  pl.BlockSpec((1, 1, D), lambda lin: (nxt(lin)[0], 0, 0)),
                  resident((1, D)), resident((1, D)),
                  resident(w_o.shape),
                  resident((1, D)), resident((1, D)),
                  pl.BlockSpec((1, 1, D), lambda lin: (tile(lin)[0], 0, 0)),
                  pl.BlockSpec((1, 1, D), lambda lin: (tile(lin)[0], 0, 0)),
                  pl.BlockSpec((1, 1, D), lambda lin: (tile(lin)[0], 0, 0)),
                  resident(wu.shape), resident(cw.shape), resident(cb.shape), resident(wd.shape),
                  resident((1, D)), resident((1, D))],
        out_specs=pl.BlockSpec((1, tm, D), lambda lin: (tile(lin)[0], tile(lin)[1], 0)),
        out_shape=jax.ShapeDtypeStruct((B, S, D), F32),
        scratch_shapes=[pltpu.VMEM((2, nf, 16, tf), F32),
                        pltpu.VMEM((nf, tm, tf), BF16),
                        pltpu.VMEM((tm, D), BF16),
                        pltpu.VMEM((D // LANES, tm, LANES), F32),
                        pltpu.VMEM((D // LANES, tm, LANES), F32),
                        pltpu.VMEM((tm, D), F32)],
        compiler_params=pltpu.CompilerParams(
            dimension_semantics=("arbitrary",), vmem_limit_bytes=VMEM_LIMIT),
        name="mlp",
    )(x, d_lo, g_out, gt_a, x, d_lo, d_hi, g_out, gt_a, ln_g, ln_b, w_o, ag, ab, sc, sh, gt_f,
      wu, cw, cb, wd, fg, fb)


def kernel(x, c, positions, ln_in_g, ln_in_b, w_ada, b_ada, w_in, lambda_q1, lambda_k1, lambda_q2, lambda_k2, diff_norm_w, gla_w_gate_up, gla_b_gate, gla_norm_w, w_out, ln_attn_g, ln_attn_b, w_up, conv_w, conv_b, w_down, ln_ffn_g, ln_ffn_b):
    B, S, D = x.shape
    assert D == D_MODEL and w_ada.shape[0] == 1
    tm = min(512, S)
    tf = 256
    nf = D_FF // tf

    c_pad = jnp.pad(c, ((0, 8 - B % 8 if B % 8 else 0), (0, 0)))
    ada = _ada(c_pad, w_ada[0], b_ada)[:B]
    sh_a, sc_a, gt_a, sh_f, sc_f, gt_f = [t[:, None, :] for t in jnp.split(ada, 6, axis=-1)]

    ln_g = ln_in_g[None, :]
    ln_b = ln_in_b[None, :]

    w_b = w_in[0].astype(BF16)
    w_proj = jnp.concatenate([w_b[:, :N_MAIN], jnp.pad(w_b[:, N_MAIN:], ((0, 0), (0, N_PROJ - w_b.shape[1])))], axis=1)
    w_gate = jnp.pad(gla_w_gate_up[0], ((0, LANES - GLA_RANK), (0, 0))).astype(BF16)
    inv = ROPE_THETA ** (-jnp.arange(0, DIFF_DH, 2, dtype=F32) / DIFF_DH)
    qk, dv, gq, gk, gv, gr, gl, mild = _proj(x, positions[:, None, :], inv[:, None], ln_g, ln_b, sc_a, sh_a,
                                       w_proj, w_gate, gla_b_gate, tm)

    d_lo, d_hi = _diffattn(qk, dv, lambda_q1, lambda_k1, lambda_q2, lambda_k2, diff_norm_w, tm, 2)
    g_out = _gla(mild[:, :, 0, 0], gq, gk, gv, gl, gr, gla_norm_w, tm)

    wd = w_down[0].astype(BF16).reshape(nf, tf, D)
    return _mlp(x, d_lo, d_hi, g_out, ln_g, ln_b, gt_a, w_out[0].astype(BF16), ln_attn_g, ln_attn_b,
                sc_f, sh_f, gt_f, w_up[0].astype(BF16), conv_w[0], conv_b, wd, ln_ffn_g, ln_ffn_b, tm)
```

```python
import functools
import math

import numpy as np
import jax
import jax.numpy as jnp
from jax import lax
from jax.experimental import pallas as pl
from jax.experimental.pallas import tpu as pltpu

F32 = jnp.float32
BF16 = jnp.bfloat16

D_MODEL = 1024
DIFF_DH = 64
DIFF_HEADS = 4
HEAD_W = 2 * DIFF_DH
GLA_HEADS = 4
GLA_DK = 64
GLA_DV = 128
GLA_KW = GLA_HEADS * GLA_DK
GLA_VW = GLA_HEADS * GLA_DV
GLA_RANK = 16
GLA_TAU = 16.0
GLA_CHUNK = 64
GLA_LEVELS = 6
GLA_MILD_DECAY = 30.0
D_FF = 2816
CONV_W = 3
ROPE_THETA = 10000.0
LN_EPS = 1e-5
DEPTH = 1
DN_ALPHA = (2.0 * DEPTH) ** 0.25
LAMBDA_INIT = 0.8 - 0.6 * math.exp(-0.3 * 0)

N_MAIN = 3072
N_PROJ = N_MAIN + 128
LANES = 128
SUBLANES = 8
NEG_BIG = -1e30
LOG2_E = math.log2(math.e)

VMEM_LIMIT = 56 * 1024 * 1024


def _layer_norm(x, g, b):
    mu = jnp.mean(x, axis=-1, keepdims=True)
    xc = x - mu
    var = jnp.mean(xc * xc, axis=-1, keepdims=True)
    return xc * lax.rsqrt(var + LN_EPS) * g + b


def _silu(x):
    return x * jax.nn.sigmoid(x)


def _ada_kernel(c_ref, w_ref, b_ref, o_ref):
    ca = _silu(c_ref[...])
    o_ref[...] = jnp.dot(ca.astype(BF16), w_ref[...].astype(BF16),
                         preferred_element_type=F32) + b_ref[...]


def _ada(c_pad, w_ada, b_ada):
    rows, d = c_pad.shape
    n = w_ada.shape[1]
    tn = 1024
    return pl.pallas_call(
        _ada_kernel,
        grid=(n // tn,),
        in_specs=[pl.BlockSpec((rows, d), lambda j: (0, 0)),
                  pl.BlockSpec((d, tn), lambda j: (0, j)),
                  pl.BlockSpec((1, tn), lambda j: (0, j))],
        out_specs=pl.BlockSpec((rows, tn), lambda j: (0, j)),
        out_shape=jax.ShapeDtypeStruct((rows, n), F32),
        name="ada",
    )(c_pad, w_ada, b_ada)


def _proj_kernel(x_ref, pos_ref, inv_ref, lg_ref, lb_ref, sc_ref, sh_ref, w_ref, wg_ref, bg_ref,
                 qk_ref, dv_ref, gq_ref, gk_ref, gv_ref, gr_ref, gl_ref, mild_ref, *, tm):
    ang_t = inv_ref[...] * pos_ref[0].astype(F32)
    reps = LANES // ang_t.shape[0]
    cos = jnp.concatenate([jnp.cos(ang_t)] * reps, axis=0).T
    sin = jnp.concatenate([jnp.sin(ang_t)] * reps, axis=0).T
    lane = lax.broadcasted_iota(jnp.int32, (1, LANES), 1)
    first_half = (lane & 32) == 0
    sin_signed = jnp.where(first_half, -sin, sin)
    scale = LOG2_E / math.sqrt(DIFF_DH)

    hm = tm // 2
    worst = None
    for part in range(2):
        rows = slice(part * hm, (part + 1) * hm)
        h = _layer_norm(x_ref[0, rows, :], lg_ref[...], lb_ref[...])
        u = (h * (1.0 + sc_ref[0]) + sh_ref[0]).astype(BF16)
        proj = jnp.dot(u, w_ref[...], preferred_element_type=F32)
        for j in range(8):
            xg = proj[:, j * LANES:(j + 1) * LANES]
            partner = jnp.where(first_half, pltpu.roll(xg, LANES - 32, axis=1), pltpu.roll(xg, 32, axis=1))
            r = xg * cos[rows] + partner * sin_signed[rows]
            if j < 4:
                r = r * scale
            qk_ref[0, rows, j * LANES:(j + 1) * LANES] = r.astype(BF16)
        dv_ref[0, rows, :] = proj[:, 1024:1536].astype(BF16)
        gq_ref[0, rows, :] = (proj[:, 1536:1792] * (GLA_DK ** -0.5)).astype(BF16)
        gk_ref[0, rows, :] = proj[:, 1792:2048].astype(BF16)
        gv_ref[0, rows, :] = proj[:, 2048:2560].astype(BF16)
        gr_ref[0, rows, :] = proj[:, 2560:3072].astype(BF16)
        gg = proj[:, N_MAIN:N_PROJ].astype(BF16)
        z = jnp.dot(gg, wg_ref[...], preferred_element_type=F32) + bg_ref[...]
        log_sig = jnp.minimum(z, 0.0) - jnp.log1p(jnp.exp(-jnp.abs(z)))
        log_gate = log_sig * (1.0 / GLA_TAU)
        gl_ref[0, rows, :] = log_gate
        for c in range(hm // GLA_CHUNK):
            total = jnp.sum(log_gate[c * GLA_CHUNK:(c + 1) * GLA_CHUNK], axis=0, keepdims=True)
            worst = total if worst is None else jnp.minimum(worst, total)
    mild = jnp.min(worst, axis=1, keepdims=True) >= -GLA_MILD_DECAY
    mild_ref[0, 0] = jnp.broadcast_to(mild.astype(jnp.int32), mild_ref.shape[2:])


def _proj(x, pos_row, inv_col, ln_g, ln_b, sc, sh, w, wg, bg, tm):
    B, S, D = x.shape
    grid = (B, S // tm)
    row = lambda b, s: (b, s, 0)
    const2 = lambda b, s: (0, 0)
    per_b = lambda b, s: (b, 0, 0)
    widths = (1024, 512, GLA_KW, GLA_KW, GLA_VW, GLA_VW)
    out_shape = [jax.ShapeDtypeStruct((B, S, n), BF16) for n in widths]
    out_shape.append(jax.ShapeDtypeStruct((B, S, GLA_KW), F32))
    out_specs = [pl.BlockSpec((1, tm, n), row) for n in widths]
    out_specs.append(pl.BlockSpec((1, tm, GLA_KW), row))
    out_shape.append(jax.ShapeDtypeStruct((B, S // tm, SUBLANES, LANES), jnp.int32))
    out_specs.append(pl.BlockSpec((1, 1, SUBLANES, LANES), lambda b, s: (b, s, 0, 0)))
    return pl.pallas_call(
        functools.partial(_proj_kernel, tm=tm),
        grid=grid,
        in_specs=[pl.BlockSpec((1, tm, D), row),
                  pl.BlockSpec((1, 1, tm), lambda b, s: (b, 0, s)),
                  pl.BlockSpec(inv_col.shape, const2),
                  pl.BlockSpec((1, D), const2),
                  pl.BlockSpec((1, D), const2),
                  pl.BlockSpec((1, 1, D), per_b),
                  pl.BlockSpec((1, 1, D), per_b),
                  pl.BlockSpec((D, N_PROJ), const2),
                  pl.BlockSpec((LANES, GLA_KW), const2),
                  pl.BlockSpec((1, GLA_KW), const2)],
        out_specs=out_specs,
        out_shape=out_shape,
        compiler_params=pltpu.CompilerParams(
            dimension_semantics=("arbitrary", "arbitrary"), vmem_limit_bytes=VMEM_LIMIT),
        name="proj",
    )(x, pos_row, inv_col, ln_g, ln_b, sc, sh, w, wg, bg)


def _diffattn_kernel(qa_ref, qb_ref, k_ref, v_ref, lq1_ref, lk1_ref, lq2_ref, lk2_ref, nw_ref,
                     oa_ref, ob_ref, qz_ref, m_ref, acc_ref, *, tq, nt, hps):
    i = pl.program_id(2)
    lane = lax.broadcasted_iota(jnp.int32, (1, HEAD_W), 1)
    for hh in range(hps):
        cols = slice(hh * HEAD_W, (hh + 1) * HEAD_W)
        for t, q_ref in enumerate((qa_ref, qb_ref)):
            q = q_ref[0, :, cols]
            zero = jnp.zeros_like(q)
            qz_ref[hh, t] = jnp.concatenate([jnp.where(lane < DIFF_DH, q, zero),
                                             jnp.where(lane >= DIFF_DH, q, zero)], axis=0)
    m_ref[...] = jnp.full(m_ref.shape, NEG_BIG, F32)
    acc_ref[...] = jnp.zeros(acc_ref.shape, F32)
    ones = jnp.ones((tq, LANES), BF16)

    def block_step(hh, own, blk, q_lo=0, k_lo=0, k_len=None, masked=False):
        k_len = tq if k_len is None else k_len
        nq = tq - q_lo
        cols = slice(hh * HEAD_W, (hh + 1) * HEAD_W)
        rows = pl.ds(pl.multiple_of(blk * tq + k_lo, k_len), k_len)
        spans = (slice(q_lo, tq), slice(tq + q_lo, 2 * tq))

        def load(ref):
            return jnp.concatenate([ref[hh, own, sp, :] for sp in spans], axis=0)

        kb = k_ref[0, rows, cols]
        vb = jnp.concatenate([v_ref[0, rows, cols], ones[0:k_len]], axis=1)
        s = lax.dot_general(load(qz_ref), kb, (((1,), (1,)), ((), ())), preferred_element_type=F32)
        if masked:
            r = lax.broadcasted_iota(jnp.int32, (2 * nq, k_len), 0)
            c = lax.broadcasted_iota(jnp.int32, (2 * nq, k_len), 1)
            r = jnp.where(r >= nq, r - nq, r)
            s = jnp.where(c + k_lo <= r + q_lo, s, NEG_BIG)
        s_fold = s[:, 0:LANES]
        for t in range(1, k_len // LANES):
            s_fold = jnp.maximum(s_fold, s[:, t * LANES:(t + 1) * LANES])
        m_prev = load(m_ref)
        m_new = jnp.maximum(m_prev, jnp.max(s_fold, axis=1, keepdims=True))
        alpha = jnp.exp2(m_prev - m_new)
        p = jnp.exp2(s - jnp.concatenate([m_new] * (k_len // LANES), axis=1))
        acc = (jnp.concatenate([alpha, alpha], axis=1) * load(acc_ref)
               + jnp.dot(p.astype(BF16), vb, preferred_element_type=F32))
        for n, sp in enumerate(spans):
            acc_ref[hh, own, sp, :] = acc[n * nq:(n + 1) * nq]
            m_ref[hh, own, sp, :] = m_new[n * nq:(n + 1) * nq]

    def diagonal(hh, own, blk):
        hk = tq // 2
        block_step(hh, own, blk, k_len=hk, masked=True)
        block_step(hh, own, blk, q_lo=hk, k_lo=hk, k_len=hk, masked=True)

    for hh in range(hps):
        diagonal(hh, 0, i)
    for t in range(1, nt):
        is_a = t <= i
        for hh in range(hps):
            block_step(hh, jnp.where(is_a, 0, 1), jnp.where(is_a, i - t, t - i - 1))
    for hh in range(hps):
        diagonal(hh, 1, nt - 1 - i)

    lam = (jnp.exp(jnp.sum(lq1_ref[...] * lk1_ref[...], axis=1, keepdims=True))
           - jnp.exp(jnp.sum(lq2_ref[...] * lk2_ref[...], axis=1, keepdims=True)) + LAMBDA_INIT)
    for hh in range(hps):
        cols = slice(hh * HEAD_W, (hh + 1) * HEAD_W)
        for t, o_ref in enumerate((oa_ref, ob_ref)):
            acc = acc_ref[hh, t]
            o = acc[:, 0:HEAD_W] / acc[:, HEAD_W:2 * HEAD_W]
            d = o[0:tq] - lam * o[tq:2 * tq]
            ms = jnp.mean(d * d, axis=1, keepdims=True)
            o_ref[0, :, cols] = (d * lax.rsqrt(ms + LN_EPS) * nw_ref[...] * (1.0 - LAMBDA_INIT)).astype(BF16)


def _diffattn(qk, dv, lq1, lk1, lq2, lk2, nw, tq, hps):
    B, S, _ = qk.shape
    nt = S // tq
    assert nt % 2 == 0, "query tiles are processed in pairs (i, nt-1-i)"
    assert DIFF_HEADS % hps == 0
    half = nt // 2
    groups = DIFF_HEADS // hps
    gw = hps * HEAD_W
    const2 = lambda b, h, i: (0, 0)
    out = jax.ShapeDtypeStruct((B, S // 2, DIFF_HEADS * HEAD_W), BF16)
    lo, hi = pl.pallas_call(
        functools.partial(_diffattn_kernel, tq=tq, nt=nt, hps=hps),
        grid=(B, groups, half),
        in_specs=[pl.BlockSpec((1, tq, gw), lambda b, h, i: (b, i, h)),
                  pl.BlockSpec((1, tq, gw), lambda b, h, i: (b, nt - 1 - i, h)),
                  pl.BlockSpec((1, S, gw), lambda b, h, i: (b, 0, groups + h)),
                  pl.BlockSpec((1, S, gw), lambda b, h, i: (b, 0, h)),
                  pl.BlockSpec((1, DIFF_DH), const2),
                  pl.BlockSpec((1, DIFF_DH), const2),
                  pl.BlockSpec((1, DIFF_DH), const2),
                  pl.BlockSpec((1, DIFF_DH), const2),
                  pl.BlockSpec((1, HEAD_W), const2)],
        out_specs=[pl.BlockSpec((1, tq, gw), lambda b, h, i: (b, i, h)),
                   pl.BlockSpec((1, tq, gw), lambda b, h, i: (b, half - 1 - i, h))],
        out_shape=[out, out],
        scratch_shapes=[pltpu.VMEM((hps, 2, 2 * tq, HEAD_W), BF16),
                        pltpu.VMEM((hps, 2, 2 * tq, LANES), F32),
                        pltpu.VMEM((hps, 2, 2 * tq, 2 * HEAD_W), F32)],
        compiler_params=pltpu.CompilerParams(
            dimension_semantics=("arbitrary", "arbitrary", "arbitrary"), vmem_limit_bytes=VMEM_LIMIT),
        name="diffattn",
    )(qk, qk, qk, dv, lq1, lk1, lq2, lk2, nw)
    return lo, hi


def _gla_tables():
    C = GLA_CHUNK
    t = np.arange(C)
    rng = np.zeros((2 + GLA_LEVELS, C, C), np.float32)
    rng[0] = (t[None, :] <= t[:, None])
    rng[1] = (t[None, :] > t[:, None])
    lvl_mask = np.zeros((GLA_LEVELS + 1, C, C), np.float32)
    for l in range(GLA_LEVELS):
        s = C >> (l + 1)
        blk = t // (2 * s)
        mid = blk * 2 * s + s
        upper = (t % (2 * s)) >= s
        for i in range(C):
            if upper[i]:
                rng[2 + l, i, mid[i]:i + 1] = 1.0
            else:
                rng[2 + l, i, i + 1:mid[i]] = 1.0
        lvl_mask[l] = (blk[:, None] == blk[None, :]) & upper[:, None] & (~upper[None, :])
    lvl_mask[GLA_LEVELS] = np.eye(C)
    rng = rng.reshape((2 + GLA_LEVELS) * C, C)
    rng3 = np.concatenate([rng, rng, rng], axis=1)
    lvl_mask = np.tile(lvl_mask, (1, 1, GLA_HEADS))
    hk = np.kron(np.eye(GLA_HEADS), np.ones((C, GLA_DK)))
    hv = np.kron(np.eye(GLA_HEADS), np.ones((C, GLA_DV)))
    hs = np.kron(np.eye(GLA_HEADS), np.ones((GLA_DV, GLA_DK)))
    return (jnp.asarray(rng3, BF16), jnp.asarray(lvl_mask, F32), jnp.asarray(hk, BF16),
            jnp.asarray(hv, BF16), jnp.asarray(hs, F32))


def _gla_kernel(mild_ref, q_ref, k_ref, v_ref, g_ref, r_ref, rng_ref, lm_ref, tril_ref, hk_ref, hv_ref, hs_ref,
                nw_ref, o_ref, st_ref, *, ts, flags):
    C = GLA_CHUNK

    @pl.when(pl.program_id(1) == 0)
    def _():
        st_ref[...] = jnp.zeros(st_ref.shape, F32)

    def gate_sums(c, mild):
        g = g_ref[0, pl.ds(c * C, C), :]
        g_hi = g.astype(BF16)
        rem = g - g_hi.astype(F32)
        g_mid = rem.astype(BF16)
        g_lo = (rem - g_mid.astype(F32)).astype(BF16)
        g3 = jnp.concatenate([g_hi, g_mid, g_lo], axis=0)
        rng = rng_ref[0:C, :] if mild else rng_ref[...]
        return jnp.dot(rng, g3, preferred_element_type=F32)

    def block_diag(kl):
        return jnp.concatenate([kl] * GLA_HEADS, axis=0) * hk_ref[...]

    def decays_mild(c, b):
        rows = pl.ds(c * C, C)
        q = q_ref[0, rows, :].astype(F32)
        k = k_ref[0, rows, :].astype(F32)
        b_last = b[C - 1:C]
        ref = b[C // 2 - 1:C // 2]
        levels = [((q * jnp.exp(b - ref)).astype(BF16), block_diag((k * jnp.exp(ref - b)).astype(BF16)))]
        return dict(q_in=(q * jnp.exp(b)).astype(BF16),
                    k_out=(k * jnp.exp(b_last - b)).astype(BF16),
                    decay=jnp.exp(b_last),
                    levels=levels)

    def decays(c, e):
        rows = pl.ds(c * C, C)
        q = q_ref[0, rows, :].astype(F32)
        k = k_ref[0, rows, :].astype(F32)
        f = jnp.exp(e)
        levels = []
        for l in range(GLA_LEVELS + 1):
            if l < GLA_LEVELS:
                fl = f[(2 + l) * C:(3 + l) * C]
                ql = (q * fl).astype(BF16)
                kl = (k * fl).astype(BF16)
            else:
                ql = q.astype(BF16)
                kl = k.astype(BF16)
            levels.append((ql, block_diag(kl)))
        return dict(q_in=(q * f[0:C]).astype(BF16),
                    k_out=(k * f[C:2 * C]).astype(BF16),
                    decay=f[C - 1:C],
                    levels=levels)

    def intra(d, mild):
        if mild:
            ql, k_bd = d["levels"][0]
            a = lax.dot_general(ql, k_bd, (((1,), (1,)), ((), ())), preferred_element_type=F32)
            return jnp.where(tril_ref[...] > 0.5, a, 0.0).astype(BF16)
        attn = jnp.zeros((C, GLA_HEADS * C), F32)
        for l, (ql, k_bd) in enumerate(d["levels"]):
            a = lax.dot_general(ql, k_bd, (((1,), (1,)), ((), ())), preferred_element_type=F32)
            attn = attn + a * lm_ref[l]
        return attn.astype(BF16)

    def output(c, d, attn):
        rows = pl.ds(c * C, C)
        v = v_ref[0, rows, :]
        v_bd = jnp.concatenate([v] * GLA_HEADS, axis=0) * hv_ref[...]
        st = st_ref[...]
        o = jnp.dot(attn, v_bd, preferred_element_type=F32)
        o = o + lax.dot_general(d["q_in"], st.astype(BF16), (((1,), (1,)), ((), ())), preferred_element_type=F32)
        upd = lax.dot_general(v, d["k_out"], (((0,), (0,)), ((), ())), preferred_element_type=F32)
        st_ref[...] = st * d["decay"] + upd * hs_ref[...]
        parts = []
        for h in range(GLA_HEADS):
            oh = o[:, h * GLA_DV:(h + 1) * GLA_DV]
            ms = jnp.mean(oh * oh, axis=1, keepdims=True)
            parts.append(oh * lax.rsqrt(ms + LN_EPS) * nw_ref[...])
        on = jnp.concatenate(parts, axis=1)
        o_ref[0, rows, :] = (on * _silu(r_ref[0, rows, :].astype(F32))).astype(BF16)

    n = ts // C

    def run(mild):
        prep = decays_mild if mild else decays
        esum, dec, att = {0: gate_sums(0, mild)}, {}, {}
        for t in range(n + 2):
            if 0 <= t - 1 < n:
                att[t - 1] = intra(dec[t - 1], mild)
            if 0 <= t - 2 < n:
                output(t - 2, dec.pop(t - 2), att.pop(t - 2))
            if t + 1 < n:
                esum[t + 1] = gate_sums(t + 1, mild)
            if t < n:
                dec[t] = prep(t, esum.pop(t))

    mild = mild_ref[pl.program_id(0), pl.program_id(1) * flags] != 0
    for t in range(1, flags):
        mild = jnp.logical_and(mild, mild_ref[pl.program_id(0), pl.program_id(1) * flags + t] != 0)

    @pl.when(mild)
    def _():
        run(True)

    @pl.when(jnp.logical_not(mild))
    def _():
        run(False)


def _gla(mild, gq, gk, gv, gl, gr, nw, ts):
    B, S, _ = gq.shape
    flags = mild.shape[1] // (S // ts)
    assert mild.shape == (B, flags * (S // ts))
    rng3, lvl_mask, hk, hv, hs = _gla_tables()
    tril = jnp.sum(lvl_mask, axis=0)
    row = lambda b, s, mild_ref: (b, s, 0)
    const2 = lambda b, s, mild_ref: (0, 0)
    const3 = lambda b, s, mild_ref: (0, 0, 0)
    grid_spec = pltpu.PrefetchScalarGridSpec(
        num_scalar_prefetch=1,
        grid=(B, S // ts),
        in_specs=[pl.BlockSpec((1, ts, GLA_KW), row),
                  pl.BlockSpec((1, ts, GLA_KW), row),
                  pl.BlockSpec((1, ts, GLA_VW), row),
                  pl.BlockSpec((1, ts, GLA_KW), row),
                  pl.BlockSpec((1, ts, GLA_VW), row),
                  pl.BlockSpec(rng3.shape, const2),
                  pl.BlockSpec(lvl_mask.shape, const3),
                  pl.BlockSpec(tril.shape, const2),
                  pl.BlockSpec(hk.shape, const2),
                  pl.BlockSpec(hv.shape, const2),
                  pl.BlockSpec(hs.shape, const2),
                  pl.BlockSpec((1, GLA_DV), const2)],
        out_specs=pl.BlockSpec((1, ts, GLA_VW), row),
        scratch_shapes=[pltpu.VMEM((GLA_VW, GLA_KW), F32)])
    return pl.pallas_call(
        functools.partial(_gla_kernel, ts=ts, flags=flags),
        grid_spec=grid_spec,
        out_shape=jax.ShapeDtypeStruct((B, S, GLA_VW), BF16),
        compiler_params=pltpu.CompilerParams(
            dimension_semantics=("arbitrary", "arbitrary"), vmem_limit_bytes=VMEM_LIMIT),
        name="gla",
    )(mild, gq, gk, gv, gl, gr, rng3, lvl_mask, tril, hk, hv, hs, nw)


def _mlp_kernel(x0_ref, d0_ref, g0_ref, gta0_ref, xn_ref, dlon_ref, dhin_ref, gn_ref, gtan_ref,
                lg_ref, lb_ref, wo_ref, ag_ref, ab_ref,
                sc_ref, sh_ref, gtf_ref, wu_ref, cw_ref, cb_ref, wd_ref, fg_ref, fb_ref, o_ref,
                carry_ref, act_ref, u_ref, h_ref, y_ref, hp_ref, *, tm, tf, nf, nt, half_tiles):
    SUB = SUBLANES
    blk = tm // SUB
    F = nf * tf
    lin = pl.program_id(0)
    s = lin % nt
    nl = h_ref.shape[0]
    hm = tm // 2

    def gather_rows(ref, start, size, stride):
        return jnp.concatenate([ref[c, pl.ds(start, size, stride=stride), :] for c in range(nl)], axis=1)

    def put_rows(ref, rows, val):
        for c in range(nl):
            ref[c, rows, :] = val[:, c * LANES:(c + 1) * LANES]

    def token_mix(x_ref, d_out, g_ref, gta_ref):
        for r in range(2):
            rows = slice(r * hm, (r + 1) * hm)
            h_in = _layer_norm(x_ref[0, rows, :], lg_ref[...], lb_ref[...])
            dw = d_out.shape[1]
            mix = (jnp.dot(d_out[rows], wo_ref[0:dw, :], preferred_element_type=F32)
                   + jnp.dot(g_ref[0, rows, :], wo_ref[dw:, :], preferred_element_type=F32))
            put_rows(h_ref, rows, _layer_norm(DN_ALPHA * h_in + (1.0 + gta_ref[0]) * mix, ag_ref[...], ab_ref[...]))

    @pl.when(lin == 0)
    def _():
        token_mix(x0_ref, d0_ref[0], g0_ref, gta0_ref)

    @pl.when(s == 0)
    def _():
        carry_ref[...] = jnp.zeros(carry_ref.shape, F32)

    for b in range(SUB):
        hb = gather_rows(h_ref, b, blk, SUB)
        hp_ref[b * blk:(b + 1) * blk, :] = hb
        u_ref[b * blk:(b + 1) * blk, :] = (hb * (1.0 + sc_ref[0]) + sh_ref[0]).astype(BF16)

    def shift_rows(block, before):
        return pltpu.roll(jnp.concatenate([before, block], axis=0), 1, axis=0)[SUB:SUB + blk]

    def conv_half(half, f):
        cols = slice(half * F + f * tf, half * F + (f + 1) * tf)
        up = jnp.dot(u_ref[...], wu_ref[:, cols], preferred_element_type=F32)
        b6 = shift_rows(up[6 * blk:7 * blk], carry_ref[half, f, 0:SUB])
        b7 = shift_rows(up[7 * blk:8 * blk], carry_ref[half, f, SUB:2 * SUB])
        carry_ref[half, f, 0:SUB] = up[7 * blk - SUB:7 * blk]
        carry_ref[half, f, SUB:2 * SUB] = up[8 * blk - SUB:8 * blk]
        back1 = jnp.concatenate([b7, up[0:7 * blk]], axis=0)
        back2 = jnp.concatenate([b6, b7, up[0:6 * blk]], axis=0)
        cw = cw_ref[:, cols]
        return cb_ref[:, cols] + cw[0:1] * back2 + cw[1:2] * back1 + cw[2:3] * up

    for f in range(nf):
        act_ref[f] = (_silu(conv_half(0, f)) * conv_half(1, f)).astype(BF16)

    for r in range(2):
        rows = slice(r * hm, (r + 1) * hm)
        ff = jnp.dot(act_ref[0, rows, :], wd_ref[0], preferred_element_type=F32)
        for f in range(1, nf):
            ff = ff + jnp.dot(act_ref[f, rows, :], wd_ref[f], preferred_element_type=F32)
        for j in range(hm // blk):
            b = r * (hm // blk) + j
            y = DN_ALPHA * hp_ref[b * blk:(b + 1) * blk, :] + (1.0 + gtf_ref[0]) * ff[j * blk:(j + 1) * blk]
            put_rows(y_ref, pl.ds(b, blk, stride=SUB), _layer_norm(y, fg_ref[...], fb_ref[...]))

    o_ref[0] = jnp.concatenate([y_ref[c] for c in range(nl)], axis=1)

    s_next = (lin + 1) % nt
    token_mix(xn_ref, jnp.where(s_next < half_tiles, dlon_ref[0], dhin_ref[0]), gn_ref, gtan_ref)


def _mlp(x, d_lo, d_hi, g_out, ln_g, ln_b, gt_a, w_o, ag, ab, sc, sh, gt_f, wu, cw, cb, wd, fg, fb, tm):
    B, S, D = x.shape
    nf, tf, _ = wd.shape
    nt = S // tm
    half_tiles = d_lo.shape[1] // tm
    last = B * nt - 1

    def tile(lin):
        return lin // nt, lin % nt

    def nxt(lin):
        return tile(jnp.minimum(lin + 1, last))

    first = lambda lin: (0, 0, 0)

    def resident(shape):
        return pl.BlockSpec(shape, lambda lin: (0,) * len(shape), pipeline_mode=pl.Buffered(1))

    return pl.pallas_call(
        functools.partial(_mlp_kernel, tm=tm, tf=tf, nf=nf, nt=nt, half_tiles=half_tiles),
        grid=(B * nt,),
        in_specs=[pl.BlockSpec((1, tm, D), first),
                  pl.BlockSpec((1, tm, d_lo.shape[2]), first),
                  pl.BlockSpec((1, tm, g_out.shape[2]), first),
                  pl.BlockSpec((1, 1, D), first),
                  pl.BlockSpec((1, tm, D), lambda lin: (nxt(lin)[0], nxt(lin)[1], 0)),
                  pl.BlockSpec((1, tm, d_lo.shape[2]),
                               lambda lin: (nxt(lin)[0], jnp.minimum(nxt(lin)[1], half_tiles - 1), 0)),
                  pl.BlockSpec((1, tm, d_hi.shape[2]),
                               lambda lin: (nxt(lin)[0], jnp.maximum(nxt(lin)[1] - half_tiles, 0), 0)),
                  pl.BlockSpec((1, tm, g_out.shape[2]), lambda lin: (nxt(lin)[0], nxt(lin)[1], 0)),
                  pl.BlockSpec((1, 1, D), lambda lin: (nxt(lin)[0], 0, 0)),
                  resident((1, D)), resident((1, D)),
                  resident(w_o.shape),
                  resident((1, D)), resident((1, D)),
                  pl.BlockSpec((1, 1, D), lambda lin: (tile(lin)[0], 0, 0)),
                  pl.BlockSpec((1, 1, D), lambda lin: (tile(lin)[0], 0, 0)),
                  pl.BlockSpec((1, 1, D), lambda lin: (tile(lin)[0], 0, 0)),
                  resident(wu.shape), resident(cw.shape), resident(cb.shape), resident(wd.shape),
                  resident((1, D)), resident((1, D))],
        out_specs=pl.BlockSpec((1, tm, D), lambda lin: (tile(lin)[0], tile(lin)[1], 0)),
        out_shape=jax.ShapeDtypeStruct((B, S, D), F32),
        scratch_shapes=[pltpu.VMEM((2, nf, 2 * SUBLANES, tf), F32),
                        pltpu.VMEM((nf, tm, tf), BF16),
                        pltpu.VMEM((tm, D), BF16),
                        pltpu.VMEM((D // LANES, tm, LANES), F32),
                        pltpu.VMEM((D // LANES, tm, LANES), F32),
                        pltpu.VMEM((tm, D), F32)],
        compiler_params=pltpu.CompilerParams(
            dimension_semantics=("arbitrary",), vmem_limit_bytes=VMEM_LIMIT),
        name="mlp",
    )(x, d_lo, g_out, gt_a, x, d_lo, d_hi, g_out, gt_a, ln_g, ln_b, w_o, ag, ab, sc, sh, gt_f,
      wu, cw, cb, wd, fg, fb)


def kernel(x, c, positions, ln_in_g, ln_in_b, w_ada, b_ada, w_in, lambda_q1, lambda_k1, lambda_q2, lambda_k2, diff_norm_w, gla_w_gate_up, gla_b_gate, gla_norm_w, w_out, ln_attn_g, ln_attn_b, w_up, conv_w, conv_b, w_down, ln_ffn_g, ln_ffn_b):
    B, S, D = x.shape
    assert D == D_MODEL and w_ada.shape[0] == 1
    tm = min(512, S)
    tf = 256
    nf = D_FF // tf

    c_pad = jnp.pad(c, ((0, -B % SUBLANES), (0, 0)))
    ada = _ada(c_pad, w_ada[0], b_ada)[:B]
    sh_a, sc_a, gt_a, sh_f, sc_f, gt_f = [t[:, None, :] for t in jnp.split(ada, 6, axis=-1)]

    ln_g = ln_in_g[None, :]
    ln_b = ln_in_b[None, :]

    w_b = w_in[0].astype(BF16)
    w_proj = jnp.concatenate([w_b[:, :N_MAIN], jnp.pad(w_b[:, N_MAIN:], ((0, 0), (0, N_PROJ - w_b.shape[1])))], axis=1)
    w_gate = jnp.pad(gla_w_gate_up[0], ((0, LANES - GLA_RANK), (0, 0))).astype(BF16)
    inv = ROPE_THETA ** (-jnp.arange(0, DIFF_DH, 2, dtype=F32) / DIFF_DH)
    qk, dv, gq, gk, gv, gr, gl, mild = _proj(x, positions[:, None, :], inv[:, None], ln_g, ln_b, sc_a, sh_a,
                                       w_proj, w_gate, gla_b_gate, tm)

    d_lo, d_hi = _diffattn(qk, dv, lambda_q1, lambda_k1, lambda_q2, lambda_k2, diff_norm_w, tm, 2)
    g_out = _gla(mild[:, :, 0, 0], gq, gk, gv, gl, gr, gla_norm_w, min(4 * tm, S))

    wd = w_down[0].astype(BF16).reshape(nf, tf, D)
    return _mlp(x, d_lo, d_hi, g_out, ln_g, ln_b, gt_a, w_out[0].astype(BF16), ln_attn_g, ln_attn_b,
                sc_f, sh_f, gt_f, w_up[0].astype(BF16), conv_w[0], conv_b, wd, ln_ffn_g, ln_ffn_b, tm)
```

```python
import functools
import math

import numpy as np
import jax
import jax.numpy as jnp
from jax import lax
from jax.experimental import pallas as pl
from jax.experimental.pallas import tpu as pltpu

F32 = jnp.float32
BF16 = jnp.bfloat16

D_MODEL = 1024
DIFF_DH = 64
DIFF_HEADS = 4
HEAD_W = 2 * DIFF_DH
GLA_HEADS = 4
GLA_DK = 64
GLA_DV = 128
GLA_KW = GLA_HEADS * GLA_DK
GLA_VW = GLA_HEADS * GLA_DV
GLA_RANK = 16
GLA_TAU = 16.0
GLA_CHUNK = 64
GLA_LEVELS = 6
GLA_MILD_DECAY = 30.0
D_FF = 2816
CONV_W = 3
ROPE_THETA = 10000.0
LN_EPS = 1e-5
DEPTH = 1
DN_ALPHA = (2.0 * DEPTH) ** 0.25
LAMBDA_INIT = 0.8 - 0.6 * math.exp(-0.3 * 0)

N_MAIN = 3072
N_PROJ = N_MAIN + 128
LANES = 128
SUBLANES = 8
NEG_BIG = -1e30
LOG2_E = math.log2(math.e)

VMEM_LIMIT = 56 * 1024 * 1024


def _layer_norm(x, g, b):
    mu = jnp.mean(x, axis=-1, keepdims=True)
    xc = x - mu
    var = jnp.mean(xc * xc, axis=-1, keepdims=True)
    return xc * lax.rsqrt(var + LN_EPS) * g + b


def _silu(x):
    return x * jax.nn.sigmoid(x)


def _ada_kernel(c_ref, w_ref, b_ref, o_ref):
    ca = _silu(c_ref[...])
    o_ref[...] = jnp.dot(ca.astype(BF16), w_ref[...].astype(BF16),
                         preferred_element_type=F32) + b_ref[...]


def _ada(c_pad, w_ada, b_ada):
    rows, d = c_pad.shape
    n = w_ada.shape[1]
    tn = 1024
    return pl.pallas_call(
        _ada_kernel,
        grid=(n // tn,),
        in_specs=[pl.BlockSpec((rows, d), lambda j: (0, 0)),
                  pl.BlockSpec((d, tn), lambda j: (0, j)),
                  pl.BlockSpec((1, tn), lambda j: (0, j))],
        out_specs=pl.BlockSpec((rows, tn), lambda j: (0, j)),
        out_shape=jax.ShapeDtypeStruct((rows, n), F32),
        name="ada",
    )(c_pad, w_ada, b_ada)


def _proj_kernel(x_ref, pos_ref, inv_ref, lg_ref, lb_ref, sc_ref, sh_ref, w_ref, wg_ref, bg_ref,
                 qk_ref, dv_ref, gq_ref, gk_ref, gv_ref, gr_ref, gl_ref, mild_ref, *, tm):
    ang_t = inv_ref[...] * pos_ref[0].astype(F32)
    reps = LANES // ang_t.shape[0]
    cos = jnp.concatenate([jnp.cos(ang_t)] * reps, axis=0).T
    sin = jnp.concatenate([jnp.sin(ang_t)] * reps, axis=0).T
    lane = lax.broadcasted_iota(jnp.int32, (1, LANES), 1)
    first_half = (lane & 32) == 0
    sin_signed = jnp.where(first_half, -sin, sin)
    scale = LOG2_E / math.sqrt(DIFF_DH)

    hm = tm // 2
    worst = None
    for part in range(2):
        rows = slice(part * hm, (part + 1) * hm)
        h = _layer_norm(x_ref[0, rows, :], lg_ref[...], lb_ref[...])
        u = (h * (1.0 + sc_ref[0]) + sh_ref[0]).astype(BF16)
        proj = jnp.dot(u, w_ref[...], preferred_element_type=F32)
        for j in range(8):
            xg = proj[:, j * LANES:(j + 1) * LANES]
            partner = jnp.where(first_half, pltpu.roll(xg, LANES - 32, axis=1), pltpu.roll(xg, 32, axis=1))
            r = xg * cos[rows] + partner * sin_signed[rows]
            if j < 4:
                r = r * scale
            qk_ref[0, rows, j * LANES:(j + 1) * LANES] = r.astype(BF16)
        dv_ref[0, rows, :] = proj[:, 1024:1536].astype(BF16)
        gq_ref[0, rows, :] = (proj[:, 1536:1792] * (GLA_DK ** -0.5)).astype(BF16)
        gk_ref[0, rows, :] = proj[:, 1792:2048].astype(BF16)
        gv_ref[0, rows, :] = proj[:, 2048:2560].astype(BF16)
        gr_ref[0, rows, :] = proj[:, 2560:3072].astype(BF16)
        gg = proj[:, N_MAIN:N_PROJ].astype(BF16)
        z = jnp.dot(gg, wg_ref[...], preferred_element_type=F32) + bg_ref[...]
        log_sig = jnp.minimum(z, 0.0) - jnp.log1p(jnp.exp(-jnp.abs(z)))
        log_gate = log_sig * (1.0 / GLA_TAU)
        gl_ref[0, rows, :] = log_gate
        for c in range(hm // GLA_CHUNK):
            total = jnp.sum(log_gate[c * GLA_CHUNK:(c + 1) * GLA_CHUNK], axis=0, keepdims=True)
            worst = total if worst is None else jnp.minimum(worst, total)
    mild = jnp.min(worst, axis=1, keepdims=True) >= -GLA_MILD_DECAY
    mild_ref[0, 0] = jnp.broadcast_to(mild.astype(jnp.int32), mild_ref.shape[2:])


def _proj(x, pos_row, inv_col, ln_g, ln_b, sc, sh, w, wg, bg, tm):
    B, S, D = x.shape
    grid = (B, S // tm)
    row = lambda b, s: (b, s, 0)
    const2 = lambda b, s: (0, 0)
    per_b = lambda b, s: (b, 0, 0)
    widths = (1024, 512, GLA_KW, GLA_KW, GLA_VW, GLA_VW)
    out_shape = [jax.ShapeDtypeStruct((B, S, n), BF16) for n in widths]
    out_shape.append(jax.ShapeDtypeStruct((B, S, GLA_KW), F32))
    out_specs = [pl.BlockSpec((1, tm, n), row) for n in widths]
    out_specs.append(pl.BlockSpec((1, tm, GLA_KW), row))
    out_shape.append(jax.ShapeDtypeStruct((B, S // tm, SUBLANES, LANES), jnp.int32))
    out_specs.append(pl.BlockSpec((1, 1, SUBLANES, LANES), lambda b, s: (b, s, 0, 0)))
    return pl.pallas_call(
        functools.partial(_proj_kernel, tm=tm),
        grid=grid,
        in_specs=[pl.BlockSpec((1, tm, D), row),
                  pl.BlockSpec((1, 1, tm), lambda b, s: (b, 0, s)),
                  pl.BlockSpec(inv_col.shape, const2),
                  pl.BlockSpec((1, D), const2),
                  pl.BlockSpec((1, D), const2),
                  pl.BlockSpec((1, 1, D), per_b),
                  pl.BlockSpec((1, 1, D), per_b),
                  pl.BlockSpec((D, N_PROJ), const2),
                  pl.BlockSpec((LANES, GLA_KW), const2),
                  pl.BlockSpec((1, GLA_KW), const2)],
        out_specs=out_specs,
        out_shape=out_shape,
        compiler_params=pltpu.CompilerParams(
            dimension_semantics=("arbitrary", "arbitrary"), vmem_limit_bytes=VMEM_LIMIT),
        name="proj",
    )(x, pos_row, inv_col, ln_g, ln_b, sc, sh, w, wg, bg)


def _diffattn_kernel(qa_ref, qb_ref, k_ref, v_ref, lq1_ref, lk1_ref, lq2_ref, lk2_ref, nw_ref,
                     oa_ref, ob_ref, qz_ref, m_ref, acc_ref, *, tq, nt, hps):
    i = pl.program_id(2)
    lane = lax.broadcasted_iota(jnp.int32, (1, HEAD_W), 1)
    for hh in range(hps):
        cols = slice(hh * HEAD_W, (hh + 1) * HEAD_W)
        for t, q_ref in enumerate((qa_ref, qb_ref)):
            q = q_ref[0, :, cols]
            zero = jnp.zeros_like(q)
            qz_ref[hh, t] = jnp.concatenate([jnp.where(lane < DIFF_DH, q, zero),
                                             jnp.where(lane >= DIFF_DH, q, zero)], axis=0)
    m_ref[...] = jnp.full(m_ref.shape, NEG_BIG, F32)
    acc_ref[...] = jnp.zeros(acc_ref.shape, F32)
    ones = jnp.ones((tq, LANES), BF16)

    def block_step(hh, own, blk, q_lo=0, k_lo=0, k_len=None, masked=False):
        k_len = tq if k_len is None else k_len
        nq = tq - q_lo
        cols = slice(hh * HEAD_W, (hh + 1) * HEAD_W)
        rows = pl.ds(pl.multiple_of(blk * tq + k_lo, k_len), k_len)
        spans = (slice(q_lo, tq), slice(tq + q_lo, 2 * tq))

        def load(ref):
            return jnp.concatenate([ref[hh, own, sp, :] for sp in spans], axis=0)

        kb = k_ref[0, rows, cols]
        vb = jnp.concatenate([v_ref[0, rows, cols], ones[0:k_len]], axis=1)
        s = lax.dot_general(load(qz_ref), kb, (((1,), (1,)), ((), ())), preferred_element_type=F32)
        if masked:
            r = lax.broadcasted_iota(jnp.int32, (2 * nq, k_len), 0)
            c = lax.broadcasted_iota(jnp.int32, (2 * nq, k_len), 1)
            r = jnp.where(r >= nq, r - nq, r)
            s = jnp.where(c + k_lo <= r + q_lo, s, NEG_BIG)
        s_fold = s[:, 0:LANES]
        for t in range(1, k_len // LANES):
            s_fold = jnp.maximum(s_fold, s[:, t * LANES:(t + 1) * LANES])
        m_prev = load(m_ref)
        m_new = jnp.maximum(m_prev, jnp.max(s_fold, axis=1, keepdims=True))
        alpha = jnp.exp2(m_prev - m_new)
        p = jnp.exp2(s - jnp.concatenate([m_new] * (k_len // LANES), axis=1))
        acc = (jnp.concatenate([alpha, alpha], axis=1) * load(acc_ref)
               + jnp.dot(p.astype(BF16), vb, preferred_element_type=F32))
        for n, sp in enumerate(spans):
            acc_ref[hh, own, sp, :] = acc[n * nq:(n + 1) * nq]
            m_ref[hh, own, sp, :] = m_new[n * nq:(n + 1) * nq]

    def diagonal(hh, own, blk):
        hk = tq // 2
        block_step(hh, own, blk, k_len=hk, masked=True)
        block_step(hh, own, blk, q_lo=hk, k_lo=hk, k_len=hk, masked=True)

    for hh in range(hps):
        diagonal(hh, 0, i)
    for t in range(1, nt):
        is_a = t <= i
        for hh in range(hps):
            block_step(hh, jnp.where(is_a, 0, 1), jnp.where(is_a, i - t, t - i - 1))
    for hh in range(hps):
        diagonal(hh, 1, nt - 1 - i)

    lam = (jnp.exp(jnp.sum(lq1_ref[...] * lk1_ref[...], axis=1, keepdims=True))
           - jnp.exp(jnp.sum(lq2_ref[...] * lk2_ref[...], axis=1, keepdims=True)) + LAMBDA_INIT)
    for hh in range(hps):
        cols = slice(hh * HEAD_W, (hh + 1) * HEAD_W)
        for t, o_ref in enumerate((oa_ref, ob_ref)):
            acc = acc_ref[hh, t]
            o = acc[:, 0:HEAD_W] / acc[:, HEAD_W:2 * HEAD_W]
            d = o[0:tq] - lam * o[tq:2 * tq]
            ms = jnp.mean(d * d, axis=1, keepdims=True)
            o_ref[0, :, cols] = (d * lax.rsqrt(ms + LN_EPS) * nw_ref[...] * (1.0 - LAMBDA_INIT)).astype(BF16)


def _diffattn(qk, dv, lq1, lk1, lq2, lk2, nw, tq, hps):
    B, S, _ = qk.shape
    nt = S // tq
    assert nt % 2 == 0, "query tiles are processed in pairs (i, nt-1-i)"
    assert DIFF_HEADS % hps == 0
    half = nt // 2
    groups = DIFF_HEADS // hps
    gw = hps * HEAD_W
    const2 = lambda b, h, i: (0, 0)
    out = jax.ShapeDtypeStruct((B, S // 2, DIFF_HEADS * HEAD_W), BF16)
    lo, hi = pl.pallas_call(
        functools.partial(_diffattn_kernel, tq=tq, nt=nt, hps=hps),
        grid=(B, groups, half),
        in_specs=[pl.BlockSpec((1, tq, gw), lambda b, h, i: (b, i, h)),
                  pl.BlockSpec((1, tq, gw), lambda b, h, i: (b, nt - 1 - i, h)),
                  pl.BlockSpec((1, S, gw), lambda b, h, i: (b, 0, groups + h)),
                  pl.BlockSpec((1, S, gw), lambda b, h, i: (b, 0, h)),
                  pl.BlockSpec((1, DIFF_DH), const2),
                  pl.BlockSpec((1, DIFF_DH), const2),
                  pl.BlockSpec((1, DIFF_DH), const2),
                  pl.BlockSpec((1, DIFF_DH), const2),
                  pl.BlockSpec((1, HEAD_W), const2)],
        out_specs=[pl.BlockSpec((1, tq, gw), lambda b, h, i: (b, i, h)),
                   pl.BlockSpec((1, tq, gw), lambda b, h, i: (b, half - 1 - i, h))],
        out_shape=[out, out],
        scratch_shapes=[pltpu.VMEM((hps, 2, 2 * tq, HEAD_W), BF16),
                        pltpu.VMEM((hps, 2, 2 * tq, LANES), F32),
                        pltpu.VMEM((hps, 2, 2 * tq, 2 * HEAD_W), F32)],
        compiler_params=pltpu.CompilerParams(
            dimension_semantics=("arbitrary", "arbitrary", "arbitrary"), vmem_limit_bytes=VMEM_LIMIT),
        name="diffattn",
    )(qk, qk, qk, dv, lq1, lk1, lq2, lk2, nw)
    return lo, hi


def _gla_tables():
    C = GLA_CHUNK
    t = np.arange(C)
    rng = np.zeros((2 + GLA_LEVELS, C, C), np.float32)
    rng[0] = (t[None, :] <= t[:, None])
    rng[1] = (t[None, :] > t[:, None])
    lvl_mask = np.zeros((GLA_LEVELS + 1, C, C), np.float32)
    for l in range(GLA_LEVELS):
        s = C >> (l + 1)
        blk = t // (2 * s)
        mid = blk * 2 * s + s
        upper = (t % (2 * s)) >= s
        for i in range(C):
            if upper[i]:
                rng[2 + l, i, mid[i]:i + 1] = 1.0
            else:
                rng[2 + l, i, i + 1:mid[i]] = 1.0
        lvl_mask[l] = (blk[:, None] == blk[None, :]) & upper[:, None] & (~upper[None, :])
    lvl_mask[GLA_LEVELS] = np.eye(C)
    rng = rng.reshape((2 + GLA_LEVELS) * C, C)
    rng3 = np.concatenate([rng, rng, rng], axis=1)
    lvl_mask = np.tile(lvl_mask, (1, 1, GLA_HEADS))
    hk = np.kron(np.eye(GLA_HEADS), np.ones((C, GLA_DK)))
    hv = np.kron(np.eye(GLA_HEADS), np.ones((C, GLA_DV)))
    hs = np.kron(np.eye(GLA_HEADS), np.ones((GLA_DV, GLA_DK)))
    return (jnp.asarray(rng3, BF16), jnp.asarray(lvl_mask, F32), jnp.asarray(hk, BF16),
            jnp.asarray(hv, BF16), jnp.asarray(hs, F32))


def _gla_kernel(mild_ref, q_ref, k_ref, v_ref, g_ref, r_ref, rng_ref, lm_ref, tril_ref, hk_ref, hv_ref, hs_ref,
                nw_ref, o_ref, st_ref, *, ts, flags):
    C = GLA_CHUNK

    @pl.when(pl.program_id(1) == 0)
    def _():
        st_ref[...] = jnp.zeros(st_ref.shape, F32)

    def gate_sums(c, mild):
        g = g_ref[0, pl.ds(c * C, C), :]
        g_hi = g.astype(BF16)
        rem = g - g_hi.astype(F32)
        g_mid = rem.astype(BF16)
        g_lo = (rem - g_mid.astype(F32)).astype(BF16)
        g3 = jnp.concatenate([g_hi, g_mid, g_lo], axis=0)
        rng = rng_ref[0:C, :] if mild else rng_ref[...]
        return jnp.dot(rng, g3, preferred_element_type=F32)

    def block_diag(kl):
        return jnp.concatenate([kl] * GLA_HEADS, axis=0) * hk_ref[...]

    def decays_mild(c, b):
        rows = pl.ds(c * C, C)
        q = q_ref[0, rows, :].astype(F32)
        k = k_ref[0, rows, :].astype(F32)
        b_last = b[C - 1:C]
        ref = b[C // 2 - 1:C // 2]
        levels = [((q * jnp.exp(b - ref)).astype(BF16), block_diag((k * jnp.exp(ref - b)).astype(BF16)))]
        return dict(q_in=(q * jnp.exp(b)).astype(BF16),
                    k_out=(k * jnp.exp(b_last - b)).astype(BF16),
                    decay=jnp.exp(b_last),
                    levels=levels)

    def decays(c, e):
        rows = pl.ds(c * C, C)
        q = q_ref[0, rows, :].astype(F32)
        k = k_ref[0, rows, :].astype(F32)
        f = jnp.exp(e)
        levels = []
        for l in range(GLA_LEVELS + 1):
            if l < GLA_LEVELS:
                fl = f[(2 + l) * C:(3 + l) * C]
                ql = (q * fl).astype(BF16)
                kl = (k * fl).astype(BF16)
            else:
                ql = q.astype(BF16)
                kl = k.astype(BF16)
            levels.append((ql, block_diag(kl)))
        return dict(q_in=(q * f[0:C]).astype(BF16),
                    k_out=(k * f[C:2 * C]).astype(BF16),
                    decay=f[C - 1:C],
                    levels=levels)

    def intra(d, mild):
        if mild:
            ql, k_bd = d["levels"][0]
            a = lax.dot_general(ql, k_bd, (((1,), (1,)), ((), ())), preferred_element_type=F32)
            return jnp.where(tril_ref[...] > 0.5, a, 0.0).astype(BF16)
        attn = jnp.zeros((C, GLA_HEADS * C), F32)
        for l, (ql, k_bd) in enumerate(d["levels"]):
            a = lax.dot_general(ql, k_bd, (((1,), (1,)), ((), ())), preferred_element_type=F32)
            attn = attn + a * lm_ref[l]
        return attn.astype(BF16)

    def output(c, d, attn):
        rows = pl.ds(c * C, C)
        v = v_ref[0, rows, :]
        v_bd = jnp.concatenate([v] * GLA_HEADS, axis=0) * hv_ref[...]
        st = st_ref[...]
        o = jnp.dot(attn, v_bd, preferred_element_type=F32)
        o = o + lax.dot_general(d["q_in"], st.astype(BF16), (((1,), (1,)), ((), ())), preferred_element_type=F32)
        upd = lax.dot_general(v, d["k_out"], (((0,), (0,)), ((), ())), preferred_element_type=F32)
        st_ref[...] = st * d["decay"] + upd * hs_ref[...]
        parts = []
        for h in range(GLA_HEADS):
            oh = o[:, h * GLA_DV:(h + 1) * GLA_DV]
            ms = jnp.mean(oh * oh, axis=1, keepdims=True)
            parts.append(oh * lax.rsqrt(ms + LN_EPS) * nw_ref[...])
        on = jnp.concatenate(parts, axis=1)
        o_ref[0, rows, :] = (on * _silu(r_ref[0, rows, :].astype(F32))).astype(BF16)

    n = ts // C

    def run(mild):
        prep = decays_mild if mild else decays
        esum, dec, att = {0: gate_sums(0, mild)}, {}, {}
        for t in range(n + 2):
            if 0 <= t - 1 < n:
                att[t - 1] = intra(dec[t - 1], mild)
            if 0 <= t - 2 < n:
                output(t - 2, dec.pop(t - 2), att.pop(t - 2))
            if t + 1 < n:
                esum[t + 1] = gate_sums(t + 1, mild)
            if t < n:
                dec[t] = prep(t, esum.pop(t))

    mild = mild_ref[pl.program_id(0), pl.program_id(1) * flags] != 0
    for t in range(1, flags):
        mild = jnp.logical_and(mild, mild_ref[pl.program_id(0), pl.program_id(1) * flags + t] != 0)

    @pl.when(mild)
    def _():
        run(True)

    @pl.when(jnp.logical_not(mild))
    def _():
        run(False)


def _gla(mild, gq, gk, gv, gl, gr, nw, ts):
    B, S, _ = gq.shape
    flags = mild.shape[1] // (S // ts)
    assert mild.shape == (B, flags * (S // ts))
    rng3, lvl_mask, hk, hv, hs = _gla_tables()
    tril = jnp.sum(lvl_mask, axis=0)
    row = lambda b, s, mild_ref: (b, s, 0)
    const2 = lambda b, s, mild_ref: (0, 0)
    const3 = lambda b, s, mild_ref: (0, 0, 0)
    grid_spec = pltpu.PrefetchScalarGridSpec(
        num_scalar_prefetch=1,
        grid=(B, S // ts),
        in_specs=[pl.BlockSpec((1, ts, GLA_KW), row),
                  pl.BlockSpec((1, ts, GLA_KW), row),
                  pl.BlockSpec((1, ts, GLA_VW), row),
                  pl.BlockSpec((1, ts, GLA_KW), row),
                  pl.BlockSpec((1, ts, GLA_VW), row),
                  pl.BlockSpec(rng3.shape, const2),
                  pl.BlockSpec(lvl_mask.shape, const3),
                  pl.BlockSpec(tril.shape, const2),
                  pl.BlockSpec(hk.shape, const2),
                  pl.BlockSpec(hv.shape, const2),
                  pl.BlockSpec(hs.shape, const2),
                  pl.BlockSpec((1, GLA_DV), const2)],
        out_specs=pl.BlockSpec((1, ts, GLA_VW), row),
        scratch_shapes=[pltpu.VMEM((GLA_VW, GLA_KW), F32)])
    return pl.pallas_call(
        functools.partial(_gla_kernel, ts=ts, flags=flags),
        grid_spec=grid_spec,
        out_shape=jax.ShapeDtypeStruct((B, S, GLA_VW), BF16),
        compiler_params=pltpu.CompilerParams(
            dimension_semantics=("arbitrary", "arbitrary"), vmem_limit_bytes=VMEM_LIMIT),
        name="gla",
    )(mild, gq, gk, gv, gl, gr, rng3, lvl_mask, tril, hk, hv, hs, nw)


def _mlp_kernel(x0_ref, d0_ref, g0_ref, gta0_ref, xn_ref, dlon_ref, dhin_ref, gn_ref, gtan_ref,
                lg_ref, lb_ref, wo_ref, ag_ref, ab_ref,
                sc_ref, sh_ref, gtf_ref, wu_ref, cw_ref, cb_ref, wd_ref, fg_ref, fb_ref, o_ref,
                carry_ref, act_ref, u_ref, h_ref, y_ref, hp_ref, *, tm, tf, nf, nt, half_tiles):
    SUB = SUBLANES
    blk = tm // SUB
    F = nf * tf
    lin = pl.program_id(0)
    s = lin % nt
    nl = h_ref.shape[0]
    hm = tm // 2

    def gather_rows(ref, start, size, stride):
        return jnp.concatenate([ref[c, pl.ds(start, size, stride=stride), :] for c in range(nl)], axis=1)

    def put_rows(ref, rows, val):
        for c in range(nl):
            ref[c, rows, :] = val[:, c * LANES:(c + 1) * LANES]

    def token_mix(x_ref, d_out, g_ref, gta_ref):
        for r in range(2):
            rows = slice(r * hm, (r + 1) * hm)
            h_in = _layer_norm(x_ref[0, rows, :], lg_ref[...], lb_ref[...])
            dw = d_out.shape[1]
            mix = (jnp.dot(d_out[rows], wo_ref[0:dw, :], preferred_element_type=F32)
                   + jnp.dot(g_ref[0, rows, :], wo_ref[dw:, :], preferred_element_type=F32))
            put_rows(h_ref, rows, _layer_norm(DN_ALPHA * h_in + (1.0 + gta_ref[0]) * mix, ag_ref[...], ab_ref[...]))

    @pl.when(lin == 0)
    def _():
        token_mix(x0_ref, d0_ref[0], g0_ref, gta0_ref)

    @pl.when(s == 0)
    def _():
        carry_ref[...] = jnp.zeros(carry_ref.shape, F32)

    for b in range(SUB):
        hb = gather_rows(h_ref, b, blk, SUB)
        hp_ref[b * blk:(b + 1) * blk, :] = hb
        u_ref[b * blk:(b + 1) * blk, :] = (hb * (1.0 + sc_ref[0]) + sh_ref[0]).astype(BF16)

    def shift_rows(block, before):
        return pltpu.roll(jnp.concatenate([before, block], axis=0), 1, axis=0)[SUB:SUB + blk]

    def conv_half(half, f):
        cols = slice(half * F + f * tf, half * F + (f + 1) * tf)
        up = jnp.dot(u_ref[...], wu_ref[:, cols], preferred_element_type=F32)
        lo, hi = (SUB - 2) * blk, (SUB - 1) * blk
        b6 = shift_rows(up[lo:hi], carry_ref[half, f, 0:SUB])
        b7 = shift_rows(up[hi:tm], carry_ref[half, f, SUB:2 * SUB])
        carry_ref[half, f, 0:SUB] = up[hi - SUB:hi]
        carry_ref[half, f, SUB:2 * SUB] = up[tm - SUB:tm]
        back1 = jnp.concatenate([b7, up[0:hi]], axis=0)
        back2 = jnp.concatenate([b6, b7, up[0:lo]], axis=0)
        cw = cw_ref[:, cols]
        return cb_ref[:, cols] + cw[0:1] * back2 + cw[1:2] * back1 + cw[2:3] * up

    for f in range(nf):
        act_ref[f] = (_silu(conv_half(0, f)) * conv_half(1, f)).astype(BF16)

    for r in range(2):
        rows = slice(r * hm, (r + 1) * hm)
        ff = jnp.dot(act_ref[0, rows, :], wd_ref[0], preferred_element_type=F32)
        for f in range(1, nf):
            ff = ff + jnp.dot(act_ref[f, rows, :], wd_ref[f], preferred_element_type=F32)
        for j in range(hm // blk):
            b = r * (hm // blk) + j
            y = DN_ALPHA * hp_ref[b * blk:(b + 1) * blk, :] + (1.0 + gtf_ref[0]) * ff[j * blk:(j + 1) * blk]
            put_rows(y_ref, pl.ds(b, blk, stride=SUB), _layer_norm(y, fg_ref[...], fb_ref[...]))

    o_ref[0] = jnp.concatenate([y_ref[c] for c in range(nl)], axis=1)

    s_next = (lin + 1) % nt
    token_mix(xn_ref, jnp.where(s_next < half_tiles, dlon_ref[0], dhin_ref[0]), gn_ref, gtan_ref)


def _mlp(x, d_lo, d_hi, g_out, ln_g, ln_b, gt_a, w_o, ag, ab, sc, sh, gt_f, wu, cw, cb, wd, fg, fb, tm):
    B, S, D = x.shape
    nf, tf, _ = wd.shape
    nt = S // tm
    half_tiles = d_lo.shape[1] // tm
    last = B * nt - 1

    def tile(lin):
        return lin // nt, lin % nt

    def nxt(lin):
        return tile(jnp.minimum(lin + 1, last))

    first = lambda lin: (0, 0, 0)

    def resident(shape):
        return pl.BlockSpec(shape, lambda lin: (0,) * len(shape), pipeline_mode=pl.Buffered(1))

    return pl.pallas_call(
        functools.partial(_mlp_kernel, tm=tm, tf=tf, nf=nf, nt=nt, half_tiles=half_tiles),
        grid=(B * nt,),
        in_specs=[pl.BlockSpec((1, tm, D), first),
                  pl.BlockSpec((1, tm, d_lo.shape[2]), first),
                  pl.BlockSpec((1, tm, g_out.shape[2]), first),
                  pl.BlockSpec((1, 1, D), first),
                  pl.BlockSpec((1, tm, D), lambda lin: (nxt(lin)[0], nxt(lin)[1], 0)),
                  pl.BlockSpec((1, tm, d_lo.shape[2]),
                               lambda lin: (nxt(lin)[0], jnp.minimum(nxt(lin)[1], half_tiles - 1), 0)),
                  pl.BlockSpec((1, tm, d_hi.shape[2]),
                               lambda lin: (nxt(lin)[0], jnp.maximum(nxt(lin)[1] - half_tiles, 0), 0)),
                  pl.BlockSpec((1, tm, g_out.shape[2]), lambda lin: (nxt(lin)[0], nxt(lin)[1], 0)),
                  pl.BlockSpec((1, 1, D), lambda lin: (nxt(lin)[0], 0, 0)),
                  resident((1, D)), resident((1, D)),
                  resident(w_o.shape),
                  resident((1, D)), resident((1, D)),
                  pl.BlockSpec((1, 1, D), lambda lin: (tile(lin)[0], 0, 0)),
                  pl.BlockSpec((1, 1, D), lambda lin: (tile(lin)[0], 0, 0)),
                  pl.BlockSpec((1, 1, D), lambda lin: (tile(lin)[0], 0, 0)),
                  resident(wu.shape), resident(cw.shape), resident(cb.shape), resident(wd.shape),
                  resident((1, D)), resident((1, D))],
        out_specs=pl.BlockSpec((1, tm, D), lambda lin: (tile(lin)[0], tile(lin)[1], 0)),
        out_shape=jax.ShapeDtypeStruct((B, S, D), F32),
        scratch_shapes=[pltpu.VMEM((2, nf, 2 * SUBLANES, tf), F32),
                        pltpu.VMEM((nf, tm, tf), BF16),
                        pltpu.VMEM((tm, D), BF16),
                        pltpu.VMEM((D // LANES, tm, LANES), F32),
                        pltpu.VMEM((D // LANES, tm, LANES), F32),
                        pltpu.VMEM((tm, D), F32)],
        compiler_params=pltpu.CompilerParams(
            dimension_semantics=("arbitrary",), vmem_limit_bytes=VMEM_LIMIT),
        name="mlp",
    )(x, d_lo, g_out, gt_a, x, d_lo, d_hi, g_out, gt_a, ln_g, ln_b, w_o, ag, ab, sc, sh, gt_f,
      wu, cw, cb, wd, fg, fb)


def kernel(x, c, positions, ln_in_g, ln_in_b, w_ada, b_ada, w_in, lambda_q1, lambda_k1, lambda_q2, lambda_k2, diff_norm_w, gla_w_gate_up, gla_b_gate, gla_norm_w, w_out, ln_attn_g, ln_attn_b, w_up, conv_w, conv_b, w_down, ln_ffn_g, ln_ffn_b):
    B, S, D = x.shape
    assert D == D_MODEL and w_ada.shape[0] == 1
    tm = min(512, S)
    tf = 256
    nf = D_FF // tf

    c_pad = jnp.pad(c, ((0, -B % SUBLANES), (0, 0)))
    ada = _ada(c_pad, w_ada[0], b_ada)[:B]
    sh_a, sc_a, gt_a, sh_f, sc_f, gt_f = [t[:, None, :] for t in jnp.split(ada, 6, axis=-1)]

    ln_g = ln_in_g[None, :]
    ln_b = ln_in_b[None, :]

    w_proj = jnp.pad(w_in[0].astype(BF16), ((0, 0), (0, N_PROJ - w_in.shape[2])))
    w_gate = jnp.pad(gla_w_gate_up[0], ((0, LANES - GLA_RANK), (0, 0))).astype(BF16)
    inv = ROPE_THETA ** (-jnp.arange(0, DIFF_DH, 2, dtype=F32) / DIFF_DH)
    qk, dv, gq, gk, gv, gr, gl, mild = _proj(x, positions[:, None, :], inv[:, None], ln_g, ln_b, sc_a, sh_a,
                                       w_proj, w_gate, gla_b_gate, tm)

    d_lo, d_hi = _diffattn(qk, dv, lambda_q1, lambda_k1, lambda_q2, lambda_k2, diff_norm_w, tm, 2)
    g_out = _gla(mild[:, :, 0, 0], gq, gk, gv, gl, gr, gla_norm_w, 2 * tm)

    wd = w_down[0].astype(BF16).reshape(nf, tf, D)
    return _mlp(x, d_lo, d_hi, g_out, ln_g, ln_b, gt_a, w_out[0].astype(BF16), ln_attn_g, ln_attn_b,
                sc_f, sh_f, gt_f, w_up[0].astype(BF16), conv_w[0], conv_b, wd, ln_ffn_g, ln_ffn_b, tm)
```

```python
import functools
import math

import numpy as np
import jax
import jax.numpy as jnp
from jax import lax
from jax.experimental import pallas as pl
from jax.experimental.pallas import tpu as pltpu

F32 = jnp.float32
BF16 = jnp.bfloat16

D_MODEL = 1024
DIFF_DH = 64
DIFF_HEADS = 4
HEAD_W = 2 * DIFF_DH
GLA_HEADS = 4
GLA_DK = 64
GLA_DV = 128
GLA_KW = GLA_HEADS * GLA_DK
GLA_VW = GLA_HEADS * GLA_DV
GLA_RANK = 16
GLA_TAU = 16.0
GLA_CHUNK = 64
GLA_LEVELS = 6
GLA_MILD_DECAY = 30.0
D_FF = 2816
CONV_W = 3
ROPE_THETA = 10000.0
LN_EPS = 1e-5
DEPTH = 1
DN_ALPHA = (2.0 * DEPTH) ** 0.25
LAMBDA_INIT = 0.8 - 0.6 * math.exp(-0.3 * 0)

N_MAIN = 3072
N_PROJ = N_MAIN + 128
LANES = 128
SUBLANES = 8
NEG_BIG = -1e30
LOG2_E = math.log2(math.e)

VMEM_LIMIT = 56 * 1024 * 1024


def _layer_norm(x, g, b):
    mu = jnp.mean(x, axis=-1, keepdims=True)
    xc = x - mu
    var = jnp.mean(xc * xc, axis=-1, keepdims=True)
    return xc * lax.rsqrt(var + LN_EPS) * g + b


def _silu(x):
    return x * jax.nn.sigmoid(x)


def _ada_kernel(c_ref, w_ref, b_ref, o_ref):
    ca = _silu(c_ref[...])
    o_ref[...] = jnp.dot(ca.astype(BF16), w_ref[...].astype(BF16),
                         preferred_element_type=F32) + b_ref[...]


def _ada(c_pad, w_ada, b_ada):
    rows, d = c_pad.shape
    n = w_ada.shape[1]
    tn = 1024
    return pl.pallas_call(
        _ada_kernel,
        grid=(n // tn,),
        in_specs=[pl.BlockSpec((rows, d), lambda j: (0, 0)),
                  pl.BlockSpec((d, tn), lambda j: (0, j)),
                  pl.BlockSpec((1, tn), lambda j: (0, j))],
        out_specs=pl.BlockSpec((rows, tn), lambda j: (0, j)),
        out_shape=jax.ShapeDtypeStruct((rows, n), F32),
        name="ada",
    )(c_pad, w_ada, b_ada)


def _proj_kernel(x_ref, pos_ref, inv_ref, lg_ref, lb_ref, sc_ref, sh_ref, w_ref, wg_ref, bg_ref,
                 qk_ref, dv_ref, gq_ref, gk_ref, gv_ref, gr_ref, gl_ref, mild_ref, *, tm):
    ang_t = inv_ref[...] * pos_ref[0].astype(F32)
    reps = LANES // ang_t.shape[0]
    cos = jnp.concatenate([jnp.cos(ang_t)] * reps, axis=0).T
    sin = jnp.concatenate([jnp.sin(ang_t)] * reps, axis=0).T
    lane = lax.broadcasted_iota(jnp.int32, (1, LANES), 1)
    first_half = (lane & 32) == 0
    sin_signed = jnp.where(first_half, -sin, sin)
    scale = LOG2_E / math.sqrt(DIFF_DH)

    hm = tm // 2
    worst = None
    for part in range(2):
        rows = slice(part * hm, (part + 1) * hm)
        h = _layer_norm(x_ref[0, rows, :], lg_ref[...], lb_ref[...])
        u = (h * (1.0 + sc_ref[0]) + sh_ref[0]).astype(BF16)
        proj = jnp.dot(u, w_ref[...], preferred_element_type=F32)
        for j in range(8):
            xg = proj[:, j * LANES:(j + 1) * LANES]
            partner = jnp.where(first_half, pltpu.roll(xg, LANES - 32, axis=1), pltpu.roll(xg, 32, axis=1))
            r = xg * cos[rows] + partner * sin_signed[rows]
            if j < 4:
                r = r * scale
            qk_ref[0, rows, j * LANES:(j + 1) * LANES] = r.astype(BF16)
        dv_ref[0, rows, :] = proj[:, 1024:1536].astype(BF16)
        gq_ref[0, rows, :] = (proj[:, 1536:1792] * (GLA_DK ** -0.5)).astype(BF16)
        gk_ref[0, rows, :] = proj[:, 1792:2048].astype(BF16)
        gv_ref[0, rows, :] = proj[:, 2048:2560].astype(BF16)
        gr_ref[0, rows, :] = proj[:, 2560:3072].astype(BF16)
        gg = proj[:, N_MAIN:N_PROJ].astype(BF16)
        z = jnp.dot(gg, wg_ref[...], preferred_element_type=F32) + bg_ref[...]
        log_sig = jnp.minimum(z, 0.0) - jnp.log1p(jnp.exp(-jnp.abs(z)))
        log_gate = log_sig * (1.0 / GLA_TAU)
        gl_ref[0, rows, :] = log_gate
        for c in range(hm // GLA_CHUNK):
            total = jnp.sum(log_gate[c * GLA_CHUNK:(c + 1) * GLA_CHUNK], axis=0, keepdims=True)
            worst = total if worst is None else jnp.minimum(worst, total)
    mild = jnp.min(worst, axis=1, keepdims=True) >= -GLA_MILD_DECAY
    mild_ref[0, 0] = jnp.broadcast_to(mild.astype(jnp.int32), mild_ref.shape[2:])


def _proj(x, pos_row, inv_col, ln_g, ln_b, sc, sh, w, wg, bg, tm):
    B, S, D = x.shape
    grid = (B, S // tm)
    row = lambda b, s: (b, s, 0)
    const2 = lambda b, s: (0, 0)
    per_b = lambda b, s: (b, 0, 0)
    widths = (1024, 512, GLA_KW, GLA_KW, GLA_VW, GLA_VW)
    out_shape = [jax.ShapeDtypeStruct((B, S, n), BF16) for n in widths]
    out_shape.append(jax.ShapeDtypeStruct((B, S, GLA_KW), F32))
    out_specs = [pl.BlockSpec((1, tm, n), row) for n in widths]
    out_specs.append(pl.BlockSpec((1, tm, GLA_KW), row))
    out_shape.append(jax.ShapeDtypeStruct((B, S // tm, SUBLANES, LANES), jnp.int32))
    out_specs.append(pl.BlockSpec((1, 1, SUBLANES, LANES), lambda b, s: (b, s, 0, 0)))
    return pl.pallas_call(
        functools.partial(_proj_kernel, tm=tm),
        grid=grid,
        in_specs=[pl.BlockSpec((1, tm, D), row),
                  pl.BlockSpec((1, 1, tm), lambda b, s: (b, 0, s)),
                  pl.BlockSpec(inv_col.shape, const2),
                  pl.BlockSpec((1, D), const2),
                  pl.BlockSpec((1, D), const2),
                  pl.BlockSpec((1, 1, D), per_b),
                  pl.BlockSpec((1, 1, D), per_b),
                  pl.BlockSpec((D, N_PROJ), const2),
                  pl.BlockSpec((LANES, GLA_KW), const2),
                  pl.BlockSpec((1, GLA_KW), const2)],
        out_specs=out_specs,
        out_shape=out_shape,
        compiler_params=pltpu.CompilerParams(
            dimension_semantics=("arbitrary", "arbitrary"), vmem_limit_bytes=VMEM_LIMIT),
        name="proj",
    )(x, pos_row, inv_col, ln_g, ln_b, sc, sh, w, wg, bg)


def _diffattn_kernel(qa_ref, qb_ref, k_ref, v_ref, lq1_ref, lk1_ref, lq2_ref, lk2_ref, nw_ref,
                     oa_ref, ob_ref, qz_ref, m_ref, acc_ref, *, tq, nt, hps):
    i = pl.program_id(2)
    lane = lax.broadcasted_iota(jnp.int32, (1, HEAD_W), 1)
    for hh in range(hps):
        cols = slice(hh * HEAD_W, (hh + 1) * HEAD_W)
        for t, q_ref in enumerate((qa_ref, qb_ref)):
            q = q_ref[0, :, cols]
            zero = jnp.zeros_like(q)
            qz_ref[hh, t] = jnp.concatenate([jnp.where(lane < DIFF_DH, q, zero),
                                             jnp.where(lane >= DIFF_DH, q, zero)], axis=0)
    m_ref[...] = jnp.full(m_ref.shape, NEG_BIG, F32)
    acc_ref[...] = jnp.zeros(acc_ref.shape, F32)
    ones = jnp.ones((tq, LANES), BF16)

    def block_step(hh, own, blk, q_lo=0, k_lo=0, k_len=None, masked=False):
        k_len = tq if k_len is None else k_len
        nq = tq - q_lo
        cols = slice(hh * HEAD_W, (hh + 1) * HEAD_W)
        rows = pl.ds(pl.multiple_of(blk * tq + k_lo, k_len), k_len)
        spans = (slice(q_lo, tq), slice(tq + q_lo, 2 * tq))

        def load(ref):
            return jnp.concatenate([ref[hh, own, sp, :] for sp in spans], axis=0)

        kb = k_ref[0, rows, cols]
        vb = jnp.concatenate([v_ref[0, rows, cols], ones[0:k_len]], axis=1)
        s = lax.dot_general(load(qz_ref), kb, (((1,), (1,)), ((), ())), preferred_element_type=F32)
        if masked:
            r = lax.broadcasted_iota(jnp.int32, (2 * nq, k_len), 0)
            c = lax.broadcasted_iota(jnp.int32, (2 * nq, k_len), 1)
            r = jnp.where(r >= nq, r - nq, r)
            s = jnp.where(c + k_lo <= r + q_lo, s, NEG_BIG)
        s_fold = s[:, 0:LANES]
        for t in range(1, k_len // LANES):
            s_fold = jnp.maximum(s_fold, s[:, t * LANES:(t + 1) * LANES])
        m_prev = load(m_ref)
        m_new = jnp.maximum(m_prev, jnp.max(s_fold, axis=1, keepdims=True))
        alpha = jnp.exp2(m_prev - m_new)
        p = jnp.exp2(s - jnp.concatenate([m_new] * (k_len // LANES), axis=1))
        acc = (jnp.concatenate([alpha, alpha], axis=1) * load(acc_ref)
               + jnp.dot(p.astype(BF16), vb, preferred_element_type=F32))
        for n, sp in enumerate(spans):
            acc_ref[hh, own, sp, :] = acc[n * nq:(n + 1) * nq]
            m_ref[hh, own, sp, :] = m_new[n * nq:(n + 1) * nq]

    def diagonal(hh, own, blk):
        hk = tq // 2
        block_step(hh, own, blk, k_len=hk, masked=True)
        block_step(hh, own, blk, q_lo=hk, k_lo=hk, k_len=hk, masked=True)

    for hh in range(hps):
        diagonal(hh, 0, i)
    for t in range(1, nt):
        is_a = t <= i
        for hh in range(hps):
            block_step(hh, jnp.where(is_a, 0, 1), jnp.where(is_a, i - t, t - i - 1))
    for hh in range(hps):
        diagonal(hh, 1, nt - 1 - i)

    lam = (jnp.exp(jnp.sum(lq1_ref[...] * lk1_ref[...], axis=1, keepdims=True))
           - jnp.exp(jnp.sum(lq2_ref[...] * lk2_ref[...], axis=1, keepdims=True)) + LAMBDA_INIT)
    for hh in range(hps):
        cols = slice(hh * HEAD_W, (hh + 1) * HEAD_W)
        for t, o_ref in enumerate((oa_ref, ob_ref)):
            acc = acc_ref[hh, t]
            o = acc[:, 0:HEAD_W] / acc[:, HEAD_W:2 * HEAD_W]
            d = o[0:tq] - lam * o[tq:2 * tq]
            ms = jnp.mean(d * d, axis=1, keepdims=True)
            o_ref[0, :, cols] = (d * lax.rsqrt(ms + LN_EPS) * nw_ref[...] * (1.0 - LAMBDA_INIT)).astype(BF16)


def _diffattn(qk, dv, lq1, lk1, lq2, lk2, nw, tq, hps):
    B, S, _ = qk.shape
    nt = S // tq
    assert nt % 2 == 0, "query tiles are processed in pairs (i, nt-1-i)"
    assert DIFF_HEADS % hps == 0
    half = nt // 2
    groups = DIFF_HEADS // hps
    gw = hps * HEAD_W
    const2 = lambda b, h, i: (0, 0)
    out = jax.ShapeDtypeStruct((B, S // 2, DIFF_HEADS * HEAD_W), BF16)
    lo, hi = pl.pallas_call(
        functools.partial(_diffattn_kernel, tq=tq, nt=nt, hps=hps),
        grid=(B, groups, half),
        in_specs=[pl.BlockSpec((1, tq, gw), lambda b, h, i: (b, i, h)),
                  pl.BlockSpec((1, tq, gw), lambda b, h, i: (b, nt - 1 - i, h)),
                  pl.BlockSpec((1, S, gw), lambda b, h, i: (b, 0, groups + h)),
                  pl.BlockSpec((1, S, gw), lambda b, h, i: (b, 0, h)),
                  pl.BlockSpec((1, DIFF_DH), const2),
                  pl.BlockSpec((1, DIFF_DH), const2),
                  pl.BlockSpec((1, DIFF_DH), const2),
                  pl.BlockSpec((1, DIFF_DH), const2),
                  pl.BlockSpec((1, HEAD_W), const2)],
        out_specs=[pl.BlockSpec((1, tq, gw), lambda b, h, i: (b, i, h)),
                   pl.BlockSpec((1, tq, gw), lambda b, h, i: (b, half - 1 - i, h))],
        out_shape=[out, out],
        scratch_shapes=[pltpu.VMEM((hps, 2, 2 * tq, HEAD_W), BF16),
                        pltpu.VMEM((hps, 2, 2 * tq, LANES), F32),
                        pltpu.VMEM((hps, 2, 2 * tq, 2 * HEAD_W), F32)],
        compiler_params=pltpu.CompilerParams(
            dimension_semantics=("arbitrary", "arbitrary", "arbitrary"), vmem_limit_bytes=VMEM_LIMIT),
        name="diffattn",
    )(qk, qk, qk, dv, lq1, lk1, lq2, lk2, nw)
    return lo, hi


def _gla_tables():
    C = GLA_CHUNK
    t = np.arange(C)
    rng = np.zeros((2 + GLA_LEVELS, C, C), np.float32)
    rng[0] = (t[None, :] <= t[:, None])
    rng[1] = (t[None, :] > t[:, None])
    lvl_mask = np.zeros((GLA_LEVELS + 1, C, C), np.float32)
    for l in range(GLA_LEVELS):
        s = C >> (l + 1)
        blk = t // (2 * s)
        mid = blk * 2 * s + s
        upper = (t % (2 * s)) >= s
        for i in range(C):
            if upper[i]:
                rng[2 + l, i, mid[i]:i + 1] = 1.0
            else:
                rng[2 + l, i, i + 1:mid[i]] = 1.0
        lvl_mask[l] = (blk[:, None] == blk[None, :]) & upper[:, None] & (~upper[None, :])
    lvl_mask[GLA_LEVELS] = np.eye(C)
    rng = rng.reshape((2 + GLA_LEVELS) * C, C)
    rng3 = np.concatenate([rng, rng, rng], axis=1)
    lvl_mask = np.tile(lvl_mask, (1, 1, GLA_HEADS))
    hk = np.kron(np.eye(GLA_HEADS), np.ones((C, GLA_DK)))
    hv = np.kron(np.eye(GLA_HEADS), np.ones((C, GLA_DV)))
    hs = np.kron(np.eye(GLA_HEADS), np.ones((GLA_DV, GLA_DK)))
    return (jnp.asarray(rng3, BF16), jnp.asarray(lvl_mask, F32), jnp.asarray(hk, BF16),
            jnp.asarray(hv, BF16), jnp.asarray(hs, F32))


def _gla_kernel(mild_ref, q_ref, k_ref, v_ref, g_ref, r_ref, rng_ref, lm_ref, tril_ref, hk_ref, hv_ref, hs_ref,
                nw_ref, o_ref, st_ref, *, ts, flags):
    C = GLA_CHUNK

    @pl.when(pl.program_id(1) == 0)
    def _():
        st_ref[...] = jnp.zeros(st_ref.shape, F32)

    def gate_sums(c, mild):
        g = g_ref[0, pl.ds(c * C, C), :]
        g_hi = g.astype(BF16)
        rem = g - g_hi.astype(F32)
        g_mid = rem.astype(BF16)
        g_lo = (rem - g_mid.astype(F32)).astype(BF16)
        g3 = jnp.concatenate([g_hi, g_mid, g_lo], axis=0)
        rng = rng_ref[0:C, :] if mild else rng_ref[...]
        return jnp.dot(rng, g3, preferred_element_type=F32)

    def block_diag(kl):
        return jnp.concatenate([kl] * GLA_HEADS, axis=0) * hk_ref[...]

    def decays_mild(c, b):
        rows = pl.ds(c * C, C)
        q = q_ref[0, rows, :].astype(F32)
        k = k_ref[0, rows, :].astype(F32)
        b_last = b[C - 1:C]
        ref = b[C // 2 - 1:C // 2]
        levels = [((q * jnp.exp(b - ref)).astype(BF16), block_diag((k * jnp.exp(ref - b)).astype(BF16)))]
        return dict(q_in=(q * jnp.exp(b)).astype(BF16),
                    k_out=(k * jnp.exp(b_last - b)).astype(BF16),
                    decay=jnp.exp(b_last),
                    levels=levels)

    def decays(c, e):
        rows = pl.ds(c * C, C)
        q = q_ref[0, rows, :].astype(F32)
        k = k_ref[0, rows, :].astype(F32)
        f = jnp.exp(e)
        levels = []
        for l in range(GLA_LEVELS + 1):
            if l < GLA_LEVELS:
                fl = f[(2 + l) * C:(3 + l) * C]
                ql = (q * fl).astype(BF16)
                kl = (k * fl).astype(BF16)
            else:
                ql = q.astype(BF16)
                kl = k.astype(BF16)
            levels.append((ql, block_diag(kl)))
        return dict(q_in=(q * f[0:C]).astype(BF16),
                    k_out=(k * f[C:2 * C]).astype(BF16),
                    decay=f[C - 1:C],
                    levels=levels)

    def intra(d, mild):
        if mild:
            ql, k_bd = d["levels"][0]
            a = lax.dot_general(ql, k_bd, (((1,), (1,)), ((), ())), preferred_element_type=F32)
            return jnp.where(tril_ref[...] > 0.5, a, 0.0).astype(BF16)
        attn = jnp.zeros((C, GLA_HEADS * C), F32)
        for l, (ql, k_bd) in enumerate(d["levels"]):
            a = lax.dot_general(ql, k_bd, (((1,), (1,)), ((), ())), preferred_element_type=F32)
            attn = attn + a * lm_ref[l]
        return attn.astype(BF16)

    def output(c, d, attn):
        rows = pl.ds(c * C, C)
        v = v_ref[0, rows, :]
        v_bd = jnp.concatenate([v] * GLA_HEADS, axis=0) * hv_ref[...]
        st = st_ref[...]
        o = jnp.dot(attn, v_bd, preferred_element_type=F32)
        o = o + lax.dot_general(d["q_in"], st.astype(BF16), (((1,), (1,)), ((), ())), preferred_element_type=F32)
        upd = lax.dot_general(v, d["k_out"], (((0,), (0,)), ((), ())), preferred_element_type=F32)
        st_ref[...] = st * d["decay"] + upd * hs_ref[...]
        parts = []
        for h in range(GLA_HEADS):
            oh = o[:, h * GLA_DV:(h + 1) * GLA_DV]
            ms = jnp.mean(oh * oh, axis=1, keepdims=True)
            parts.append(oh * lax.rsqrt(ms + LN_EPS) * nw_ref[...])
        on = jnp.concatenate(parts, axis=1)
        o_ref[0, rows, :] = (on * _silu(r_ref[0, rows, :].astype(F32))).astype(BF16)

    n = ts // C

    def run(mild):
        prep = decays_mild if mild else decays
        esum, dec, att = {0: gate_sums(0, mild)}, {}, {}
        for t in range(n + 2):
            if 0 <= t - 1 < n:
                att[t - 1] = intra(dec[t - 1], mild)
            if 0 <= t - 2 < n:
                output(t - 2, dec.pop(t - 2), att.pop(t - 2))
            if t + 1 < n:
                esum[t + 1] = gate_sums(t + 1, mild)
            if t < n:
                dec[t] = prep(t, esum.pop(t))

    mild = mild_ref[pl.program_id(0), pl.program_id(1) * flags] != 0
    for t in range(1, flags):
        mild = jnp.logical_and(mild, mild_ref[pl.program_id(0), pl.program_id(1) * flags + t] != 0)

    @pl.when(mild)
    def _():
        run(True)

    @pl.when(jnp.logical_not(mild))
    def _():
        run(False)


def _gla(mild, gq, gk, gv, gl, gr, nw, ts):
    B, S, _ = gq.shape
    flags = mild.shape[1] // (S // ts)
    assert mild.shape == (B, flags * (S // ts))
    rng3, lvl_mask, hk, hv, hs = _gla_tables()
    tril = jnp.sum(lvl_mask, axis=0)
    row = lambda b, s, mild_ref: (b, s, 0)
    const2 = lambda b, s, mild_ref: (0, 0)
    const3 = lambda b, s, mild_ref: (0, 0, 0)
    grid_spec = pltpu.PrefetchScalarGridSpec(
        num_scalar_prefetch=1,
        grid=(B, S // ts),
        in_specs=[pl.BlockSpec((1, ts, GLA_KW), row),
                  pl.BlockSpec((1, ts, GLA_KW), row),
                  pl.BlockSpec((1, ts, GLA_VW), row),
                  pl.BlockSpec((1, ts, GLA_KW), row),
                  pl.BlockSpec((1, ts, GLA_VW), row),
                  pl.BlockSpec(rng3.shape, const2),
                  pl.BlockSpec(lvl_mask.shape, const3),
                  pl.BlockSpec(tril.shape, const2),
                  pl.BlockSpec(hk.shape, const2),
                  pl.BlockSpec(hv.shape, const2),
                  pl.BlockSpec(hs.shape, const2),
                  pl.BlockSpec((1, GLA_DV), const2)],
        out_specs=pl.BlockSpec((1, ts, GLA_VW), row),
        scratch_shapes=[pltpu.VMEM((GLA_VW, GLA_KW), F32)])
    return pl.pallas_call(
        functools.partial(_gla_kernel, ts=ts, flags=flags),
        grid_spec=grid_spec,
        out_shape=jax.ShapeDtypeStruct((B, S, GLA_VW), BF16),
        compiler_params=pltpu.CompilerParams(
            dimension_semantics=("arbitrary", "arbitrary"), vmem_limit_bytes=VMEM_LIMIT),
        name="gla",
    )(mild, gq, gk, gv, gl, gr, rng3, lvl_mask, tril, hk, hv, hs, nw)


def _mlp_kernel(x0_ref, d0_ref, g0_ref, gta0_ref, xn_ref, dlon_ref, dhin_ref, gn_ref, gtan_ref,
                lg_ref, lb_ref, wo_ref, ag_ref, ab_ref,
                sc_ref, sh_ref, gtf_ref, wu_ref, cw_ref, cb_ref, wd_ref, fg_ref, fb_ref, o_ref,
                carry_ref, act_ref, u_ref, h_ref, y_ref, hp_ref, *, tm, tf, nf, nt, half_tiles):
    SUB = SUBLANES
    blk = tm // SUB
    F = nf * tf
    lin = pl.program_id(0)
    s = lin % nt
    nl = h_ref.shape[0]
    hm = tm // 2

    def gather_rows(ref, start, size, stride):
        return jnp.concatenate([ref[c, pl.ds(start, size, stride=stride), :] for c in range(nl)], axis=1)

    def put_rows(ref, rows, val):
        for c in range(nl):
            ref[c, rows, :] = val[:, c * LANES:(c + 1) * LANES]

    def token_mix(x_ref, d_out, g_ref, gta_ref):
        for r in range(2):
            rows = slice(r * hm, (r + 1) * hm)
            h_in = _layer_norm(x_ref[0, rows, :], lg_ref[...], lb_ref[...])
            dw = d_out.shape[1]
            mix = (jnp.dot(d_out[rows], wo_ref[0:dw, :], preferred_element_type=F32)
                   + jnp.dot(g_ref[0, rows, :], wo_ref[dw:, :], preferred_element_type=F32))
            put_rows(h_ref, rows, _layer_norm(DN_ALPHA * h_in + (1.0 + gta_ref[0]) * mix, ag_ref[...], ab_ref[...]))

    @pl.when(lin == 0)
    def _():
        token_mix(x0_ref, d0_ref[0], g0_ref, gta0_ref)

    @pl.when(s == 0)
    def _():
        carry_ref[...] = jnp.zeros(carry_ref.shape, F32)

    for b in range(SUB):
        hb = gather_rows(h_ref, b, blk, SUB)
        hp_ref[b * blk:(b + 1) * blk, :] = hb
        u_ref[b * blk:(b + 1) * blk, :] = (hb * (1.0 + sc_ref[0]) + sh_ref[0]).astype(BF16)

    def shift_rows(block, before):
        return pltpu.roll(jnp.concatenate([before, block], axis=0), 1, axis=0)[SUB:SUB + blk]

    def conv_half(half, f):
        cols = slice(half * F + f * tf, half * F + (f + 1) * tf)
        up = jnp.dot(u_ref[...], wu_ref[:, cols], preferred_element_type=F32)
        lo, hi = (SUB - 2) * blk, (SUB - 1) * blk
        b6 = shift_rows(up[lo:hi], carry_ref[half, f, 0:SUB])
        b7 = shift_rows(up[hi:tm], carry_ref[half, f, SUB:2 * SUB])
        carry_ref[half, f, 0:SUB] = up[hi - SUB:hi]
        carry_ref[half, f, SUB:2 * SUB] = up[tm - SUB:tm]
        back1 = jnp.concatenate([b7, up[0:hi]], axis=0)
        back2 = jnp.concatenate([b6, b7, up[0:lo]], axis=0)
        cw = cw_ref[:, cols]
        return cb_ref[:, cols] + cw[0:1] * back2 + cw[1:2] * back1 + cw[2:3] * up

    for f in range(nf):
        act_ref[f] = (_silu(conv_half(0, f)) * conv_half(1, f)).astype(BF16)

    for r in range(2):
        rows = slice(r * hm, (r + 1) * hm)
        ff = jnp.dot(act_ref[0, rows, :], wd_ref[0], preferred_element_type=F32)
        for f in range(1, nf):
            ff = ff + jnp.dot(act_ref[f, rows, :], wd_ref[f], preferred_element_type=F32)
        for j in range(hm // blk):
            b = r * (hm // blk) + j
            y = DN_ALPHA * hp_ref[b * blk:(b + 1) * blk, :] + (1.0 + gtf_ref[0]) * ff[j * blk:(j + 1) * blk]
            put_rows(y_ref, pl.ds(b, blk, stride=SUB), _layer_norm(y, fg_ref[...], fb_ref[...]))

    o_ref[0] = jnp.concatenate([y_ref[c] for c in range(nl)], axis=1)

    s_next = (lin + 1) % nt
    token_mix(xn_ref, jnp.where(s_next < half_tiles, dlon_ref[0], dhin_ref[0]), gn_ref, gtan_ref)


def _mlp(x, d_lo, d_hi, g_out, ln_g, ln_b, gt_a, w_o, ag, ab, sc, sh, gt_f, wu, cw, cb, wd, fg, fb, tm):
    B, S, D = x.shape
    nf, tf, _ = wd.shape
    nt = S // tm
    half_tiles = d_lo.shape[1] // tm
    last = B * nt - 1

    def tile(lin):
        return lin // nt, lin % nt

    def nxt(lin):
        return tile(jnp.minimum(lin + 1, last))

    first = lambda lin: (0, 0, 0)

    def resident(shape):
        return pl.BlockSpec(shape, lambda lin: (0,) * len(shape), pipeline_mode=pl.Buffered(1))

    return pl.pallas_call(
        functools.partial(_mlp_kernel, tm=tm, tf=tf, nf=nf, nt=nt, half_tiles=half_tiles),
        grid=(B * nt,),
        in_specs=[pl.BlockSpec((1, tm, D), first),
                  pl.BlockSpec((1, tm, d_lo.shape[2]), first),
                  pl.BlockSpec((1, tm, g_out.shape[2]), first),
                  pl.BlockSpec((1, 1, D), first),
                  pl.BlockSpec((1, tm, D), lambda lin: (nxt(lin)[0], nxt(lin)[1], 0)),
                  pl.BlockSpec((1, tm, d_lo.shape[2]),
                               lambda lin: (nxt(lin)[0], jnp.minimum(nxt(lin)[1], half_tiles - 1), 0)),
                  pl.BlockSpec((1, tm, d_hi.shape[2]),
                               lambda lin: (nxt(lin)[0], jnp.maximum(nxt(lin)[1] - half_tiles, 0), 0)),
                  pl.BlockSpec((1, tm, g_out.shape[2]), lambda lin: (nxt(lin)[0], nxt(lin)[1], 0)),
                  pl.BlockSpec((1, 1, D), lambda lin: (nxt(lin)[0], 0, 0)),
                  resident((1, D)), resident((1, D)),
                  resident(w_o.shape),
                  resident((1, D)), resident((1, D)),
                  pl.BlockSpec((1, 1, D), lambda lin: (tile(lin)[0], 0, 0)),
                  pl.BlockSpec((1, 1, D), lambda lin: (tile(lin)[0], 0, 0)),
                  pl.BlockSpec((1, 1, D), lambda lin: (tile(lin)[0], 0, 0)),
                  resident(wu.shape), resident(cw.shape), resident(cb.shape), resident(wd.shape),
                  resident((1, D)), resident((1, D))],
        out_specs=pl.BlockSpec((1, tm, D), lambda lin: (tile(lin)[0], tile(lin)[1], 0)),
        out_shape=jax.ShapeDtypeStruct((B, S, D), F32),
        scratch_shapes=[pltpu.VMEM((2, nf, 2 * SUBLANES, tf), F32),
                        pltpu.VMEM((nf, tm, tf), BF16),
                        pltpu.VMEM((tm, D), BF16),
                        pltpu.VMEM((D // LANES, tm, LANES), F32),
                        pltpu.VMEM((D // LANES, tm, LANES), F32),
                        pltpu.VMEM((tm, D), F32)],
        compiler_params=pltpu.CompilerParams(
            dimension_semantics=("arbitrary",), vmem_limit_bytes=VMEM_LIMIT),
        name="mlp",
    )(x, d_lo, g_out, gt_a, x, d_lo, d_hi, g_out, gt_a, ln_g, ln_b, w_o, ag, ab, sc, sh, gt_f,
      wu, cw, cb, wd, fg, fb)


def kernel(x, c, positions, ln_in_g, ln_in_b, w_ada, b_ada, w_in, lambda_q1, lambda_k1, lambda_q2, lambda_k2, diff_norm_w, gla_w_gate_up, gla_b_gate, gla_norm_w, w_out, ln_attn_g, ln_attn_b, w_up, conv_w, conv_b, w_down, ln_ffn_g, ln_ffn_b):
    B, S, D = x.shape
    assert D == D_MODEL and w_ada.shape[0] == 1
    assert conv_w.shape[1] == CONV_W and w_up.shape[2] == 2 * D_FF
    tm = min(512, S)
    tf = 256
    nf = D_FF // tf

    c_pad = jnp.pad(c, ((0, -B % SUBLANES), (0, 0)))
    ada = _ada(c_pad, w_ada[0], b_ada)[:B]
    sh_a, sc_a, gt_a, sh_f, sc_f, gt_f = [t[:, None, :] for t in jnp.split(ada, 6, axis=-1)]

    ln_g = ln_in_g[None, :]
    ln_b = ln_in_b[None, :]

    w_proj = jnp.pad(w_in[0].astype(BF16), ((0, 0), (0, N_PROJ - w_in.shape[2])))
    w_gate = jnp.pad(gla_w_gate_up[0], ((0, LANES - GLA_RANK), (0, 0))).astype(BF16)
    inv = ROPE_THETA ** (-jnp.arange(0, DIFF_DH, 2, dtype=F32) / DIFF_DH)
    qk, dv, gq, gk, gv, gr, gl, mild = _proj(x, positions[:, None, :], inv[:, None], ln_g, ln_b, sc_a, sh_a,
                                       w_proj, w_gate, gla_b_gate, tm)

    d_lo, d_hi = _diffattn(qk, dv, lambda_q1, lambda_k1, lambda_q2, lambda_k2, diff_norm_w, tm, 2)
    g_out = _gla(mild[:, :, 0, 0], gq, gk, gv, gl, gr, gla_norm_w, 2 * tm)

    wd = w_down[0].astype(BF16).reshape(nf, tf, D)
    return _mlp(x, d_lo, d_hi, g_out, ln_g, ln_b, gt_a, w_out[0].astype(BF16), ln_attn_g, ln_attn_b,
                sc_f, sh_f, gt_f, w_up[0].astype(BF16), conv_w[0], conv_b, wd, ln_ffn_g, ln_ffn_b, tm)
```

```python
import functools
import math

import numpy as np
import jax
import jax.numpy as jnp
from jax import lax
from jax.experimental import pallas as pl
from jax.experimental.pallas import tpu as pltpu

F32 = jnp.float32
BF16 = jnp.bfloat16

D_MODEL = 1024
DIFF_DH = 64
DIFF_HEADS = 4
HEAD_W = 2 * DIFF_DH
GLA_HEADS = 4
GLA_DK = 64
GLA_DV = 128
GLA_KW = GLA_HEADS * GLA_DK
GLA_VW = GLA_HEADS * GLA_DV
GLA_RANK = 16
GLA_TAU = 16.0
GLA_CHUNK = 64
GLA_LEVELS = 6
GLA_MILD_DECAY = 30.0
D_FF = 2816
CONV_W = 3
ROPE_THETA = 10000.0
LN_EPS = 1e-5
DEPTH = 1
DN_ALPHA = (2.0 * DEPTH) ** 0.25
LAMBDA_INIT = 0.8 - 0.6 * math.exp(-0.3 * 0)

N_MAIN = 3072
N_PROJ = N_MAIN + 128
LANES = 128
SUBLANES = 8
NEG_BIG = -1e30
LOG2_E = math.log2(math.e)

VMEM_LIMIT = 56 * 1024 * 1024


def _layer_norm(x, g, b):
    mu = jnp.mean(x, axis=-1, keepdims=True)
    xc = x - mu
    var = jnp.mean(xc * xc, axis=-1, keepdims=True)
    return xc * lax.rsqrt(var + LN_EPS) * g + b


def _silu(x):
    return x * jax.nn.sigmoid(x)


def _ada_kernel(c_ref, w_ref, b_ref, o_ref):
    ca = _silu(c_ref[...])
    o_ref[...] = jnp.dot(ca.astype(BF16), w_ref[...].astype(BF16),
                         preferred_element_type=F32) + b_ref[...]


def _ada(c_pad, w_ada, b_ada):
    rows, d = c_pad.shape
    n = w_ada.shape[1]
    tn = 1024
    return pl.pallas_call(
        _ada_kernel,
        grid=(n // tn,),
        in_specs=[pl.BlockSpec((rows, d), lambda j: (0, 0)),
                  pl.BlockSpec((d, tn), lambda j: (0, j)),
                  pl.BlockSpec((1, tn), lambda j: (0, j))],
        out_specs=pl.BlockSpec((rows, tn), lambda j: (0, j)),
        out_shape=jax.ShapeDtypeStruct((rows, n), F32),
        name="ada",
    )(c_pad, w_ada, b_ada)


def _proj_kernel(x_ref, pos_ref, inv_ref, lg_ref, lb_ref, sc_ref, sh_ref, w_ref, wg_ref, bg_ref,
                 qk_ref, dv_ref, gq_ref, gk_ref, gv_ref, gr_ref, gl_ref, mild_ref, *, tm):
    ang_t = inv_ref[...] * pos_ref[0].astype(F32)
    reps = LANES // ang_t.shape[0]
    cos = jnp.concatenate([jnp.cos(ang_t)] * reps, axis=0).T
    sin = jnp.concatenate([jnp.sin(ang_t)] * reps, axis=0).T
    lane = lax.broadcasted_iota(jnp.int32, (1, LANES), 1)
    first_half = (lane & 32) == 0
    sin_signed = jnp.where(first_half, -sin, sin)
    scale = LOG2_E / math.sqrt(DIFF_DH)

    hm = tm // 2
    worst = None
    for part in range(2):
        rows = slice(part * hm, (part + 1) * hm)
        h = _layer_norm(x_ref[0, rows, :], lg_ref[...], lb_ref[...])
        u = (h * (1.0 + sc_ref[0]) + sh_ref[0]).astype(BF16)
        proj = jnp.dot(u, w_ref[...], preferred_element_type=F32)
        for j in range(8):
            xg = proj[:, j * LANES:(j + 1) * LANES]
            partner = jnp.where(first_half, pltpu.roll(xg, LANES - 32, axis=1), pltpu.roll(xg, 32, axis=1))
            r = xg * cos[rows] + partner * sin_signed[rows]
            if j < 4:
                r = r * scale
            qk_ref[0, rows, j * LANES:(j + 1) * LANES] = r.astype(BF16)
        dv_ref[0, rows, :] = proj[:, 1024:1536].astype(BF16)
        gq_ref[0, rows, :] = (proj[:, 1536:1792] * (GLA_DK ** -0.5)).astype(BF16)
        gk_ref[0, rows, :] = proj[:, 1792:2048].astype(BF16)
        gv_ref[0, rows, :] = proj[:, 2048:2560].astype(BF16)
        gr_ref[0, rows, :] = proj[:, 2560:3072].astype(BF16)
        gg = proj[:, N_MAIN:N_PROJ].astype(BF16)
        z = jnp.dot(gg, wg_ref[...], preferred_element_type=F32) + bg_ref[...]
        log_sig = jnp.minimum(z, 0.0) - jnp.log1p(jnp.exp(-jnp.abs(z)))
        log_gate = log_sig * (1.0 / GLA_TAU)
        gl_ref[0, rows, :] = log_gate
        for c in range(hm // GLA_CHUNK):
            total = jnp.sum(log_gate[c * GLA_CHUNK:(c + 1) * GLA_CHUNK], axis=0, keepdims=True)
            worst = total if worst is None else jnp.minimum(worst, total)
    mild = jnp.min(worst, axis=1, keepdims=True) >= -GLA_MILD_DECAY
    mild_ref[0, 0] = jnp.broadcast_to(mild.astype(jnp.int32), mild_ref.shape[2:])


def _proj(x, pos_row, inv_col, ln_g, ln_b, sc, sh, w, wg, bg, tm):
    B, S, D = x.shape
    grid = (B, S // tm)
    row = lambda b, s: (b, s, 0)
    const2 = lambda b, s: (0, 0)
    per_b = lambda b, s: (b, 0, 0)
    widths = (1024, 512, GLA_KW, GLA_KW, GLA_VW, GLA_VW)
    out_shape = [jax.ShapeDtypeStruct((B, S, n), BF16) for n in widths]
    out_shape.append(jax.ShapeDtypeStruct((B, S, GLA_KW), F32))
    out_specs = [pl.BlockSpec((1, tm, n), row) for n in widths]
    out_specs.append(pl.BlockSpec((1, tm, GLA_KW), row))
    out_shape.append(jax.ShapeDtypeStruct((B, S // tm, SUBLANES, LANES), jnp.int32))
    out_specs.append(pl.BlockSpec((1, 1, SUBLANES, LANES), lambda b, s: (b, s, 0, 0)))
    return pl.pallas_call(
        functools.partial(_proj_kernel, tm=tm),
        grid=grid,
        in_specs=[pl.BlockSpec((1, tm, D), row),
                  pl.BlockSpec((1, 1, tm), lambda b, s: (b, 0, s)),
                  pl.BlockSpec(inv_col.shape, const2),
                  pl.BlockSpec((1, D), const2),
                  pl.BlockSpec((1, D), const2),
                  pl.BlockSpec((1, 1, D), per_b),
                  pl.BlockSpec((1, 1, D), per_b),
                  pl.BlockSpec((D, N_PROJ), const2),
                  pl.BlockSpec((LANES, GLA_KW), const2),
                  pl.BlockSpec((1, GLA_KW), const2)],
        out_specs=out_specs,
        out_shape=out_shape,
        compiler_params=pltpu.CompilerParams(
            dimension_semantics=("arbitrary", "arbitrary"), vmem_limit_bytes=VMEM_LIMIT),
        name="proj",
    )(x, pos_row, inv_col, ln_g, ln_b, sc, sh, w, wg, bg)


def _diffattn_kernel(qa_ref, qb_ref, k_ref, v_ref, lq1_ref, lk1_ref, lq2_ref, lk2_ref, nw_ref,
                     oa_ref, ob_ref, qz_ref, m_ref, acc_ref, *, tq, nt, hps):
    i = pl.program_id(2)
    lane = lax.broadcasted_iota(jnp.int32, (1, HEAD_W), 1)
    for hh in range(hps):
        cols = slice(hh * HEAD_W, (hh + 1) * HEAD_W)
        for t, q_ref in enumerate((qa_ref, qb_ref)):
            q = q_ref[0, :, cols]
            zero = jnp.zeros_like(q)
            qz_ref[hh, t] = jnp.concatenate([jnp.where(lane < DIFF_DH, q, zero),
                                             jnp.where(lane >= DIFF_DH, q, zero)], axis=0)
    m_ref[...] = jnp.full(m_ref.shape, NEG_BIG, F32)
    acc_ref[...] = jnp.zeros(acc_ref.shape, F32)
    ones = jnp.ones((tq, LANES), BF16)

    def block_step(hh, own, blk, q_lo=0, k_lo=0, k_len=None, masked=False):
        k_len = tq if k_len is None else k_len
        nq = tq - q_lo
        cols = slice(hh * HEAD_W, (hh + 1) * HEAD_W)
        rows = pl.ds(pl.multiple_of(blk * tq + k_lo, k_len), k_len)
        spans = (slice(q_lo, tq), slice(tq + q_lo, 2 * tq))

        def load(ref):
            return jnp.concatenate([ref[hh, own, sp, :] for sp in spans], axis=0)

        kb = k_ref[0, rows, cols]
        vb = jnp.concatenate([v_ref[0, rows, cols], ones[0:k_len]], axis=1)
        s = lax.dot_general(load(qz_ref), kb, (((1,), (1,)), ((), ())), preferred_element_type=F32)
        if masked:
            r = lax.broadcasted_iota(jnp.int32, (2 * nq, k_len), 0)
            c = lax.broadcasted_iota(jnp.int32, (2 * nq, k_len), 1)
            r = jnp.where(r >= nq, r - nq, r)
            s = jnp.where(c + k_lo <= r + q_lo, s, NEG_BIG)
        s_fold = s[:, 0:LANES]
        for t in range(1, k_len // LANES):
            s_fold = jnp.maximum(s_fold, s[:, t * LANES:(t + 1) * LANES])
        m_prev = load(m_ref)
        m_new = jnp.maximum(m_prev, jnp.max(s_fold, axis=1, keepdims=True))
        alpha = jnp.exp2(m_prev - m_new)
        p = jnp.exp2(s - jnp.concatenate([m_new] * (k_len // LANES), axis=1))
        acc = (jnp.concatenate([alpha, alpha], axis=1) * load(acc_ref)
               + jnp.dot(p.astype(BF16), vb, preferred_element_type=F32))
        for n, sp in enumerate(spans):
            acc_ref[hh, own, sp, :] = acc[n * nq:(n + 1) * nq]
            m_ref[hh, own, sp, :] = m_new[n * nq:(n + 1) * nq]

    def diagonal(hh, own, blk):
        hk = tq // 2
        block_step(hh, own, blk, k_len=hk, masked=True)
        block_step(hh, own, blk, q_lo=hk, k_lo=hk, k_len=hk, masked=True)

    for hh in range(hps):
        diagonal(hh, 0, i)
    for t in range(1, nt):
        is_a = t <= i
        for hh in range(hps):
            block_step(hh, jnp.where(is_a, 0, 1), jnp.where(is_a, i - t, t - i - 1))
    for hh in range(hps):
        diagonal(hh, 1, nt - 1 - i)

    lam = (jnp.exp(jnp.sum(lq1_ref[...] * lk1_ref[...], axis=1, keepdims=True))
           - jnp.exp(jnp.sum(lq2_ref[...] * lk2_ref[...], axis=1, keepdims=True)) + LAMBDA_INIT)
    for hh in range(hps):
        cols = slice(hh * HEAD_W, (hh + 1) * HEAD_W)
        for t, o_ref in enumerate((oa_ref, ob_ref)):
            acc = acc_ref[hh, t]
            o = acc[:, 0:HEAD_W] / acc[:, HEAD_W:2 * HEAD_W]
            d = o[0:tq] - lam * o[tq:2 * tq]
            ms = jnp.mean(d * d, axis=1, keepdims=True)
            o_ref[0, :, cols] = (d * lax.rsqrt(ms + LN_EPS) * nw_ref[...] * (1.0 - LAMBDA_INIT)).astype(BF16)


def _diffattn(qk, dv, lq1, lk1, lq2, lk2, nw, tq, hps):
    B, S, _ = qk.shape
    nt = S // tq
    assert nt % 2 == 0, "query tiles are processed in pairs (i, nt-1-i)"
    assert DIFF_HEADS % hps == 0
    half = nt // 2
    groups = DIFF_HEADS // hps
    gw = hps * HEAD_W
    const2 = lambda b, h, i: (0, 0)
    out = jax.ShapeDtypeStruct((B, S // 2, DIFF_HEADS * HEAD_W), BF16)
    lo, hi = pl.pallas_call(
        functools.partial(_diffattn_kernel, tq=tq, nt=nt, hps=hps),
        grid=(B, groups, half),
        in_specs=[pl.BlockSpec((1, tq, gw), lambda b, h, i: (b, i, h)),
                  pl.BlockSpec((1, tq, gw), lambda b, h, i: (b, nt - 1 - i, h)),
                  pl.BlockSpec((1, S, gw), lambda b, h, i: (b, 0, groups + h)),
                  pl.BlockSpec((1, S, gw), lambda b, h, i: (b, 0, h)),
                  pl.BlockSpec((1, DIFF_DH), const2),
                  pl.BlockSpec((1, DIFF_DH), const2),
                  pl.BlockSpec((1, DIFF_DH), const2),
                  pl.BlockSpec((1, DIFF_DH), const2),
                  pl.BlockSpec((1, HEAD_W), const2)],
        out_specs=[pl.BlockSpec((1, tq, gw), lambda b, h, i: (b, i, h)),
                   pl.BlockSpec((1, tq, gw), lambda b, h, i: (b, half - 1 - i, h))],
        out_shape=[out, out],
        scratch_shapes=[pltpu.VMEM((hps, 2, 2 * tq, HEAD_W), BF16),
                        pltpu.VMEM((hps, 2, 2 * tq, LANES), F32),
                        pltpu.VMEM((hps, 2, 2 * tq, 2 * HEAD_W), F32)],
        compiler_params=pltpu.CompilerParams(
            dimension_semantics=("arbitrary", "arbitrary", "arbitrary"), vmem_limit_bytes=VMEM_LIMIT),
        name="diffattn",
    )(qk, qk, qk, dv, lq1, lk1, lq2, lk2, nw)
    return lo, hi


def _gla_tables():
    C = GLA_CHUNK
    t = np.arange(C)
    rng = np.zeros((2 + GLA_LEVELS, C, C), np.float32)
    rng[0] = (t[None, :] <= t[:, None])
    rng[1] = (t[None, :] > t[:, None])
    lvl_mask = np.zeros((GLA_LEVELS + 1, C, C), np.float32)
    for l in range(GLA_LEVELS):
        s = C >> (l + 1)
        blk = t // (2 * s)
        mid = blk * 2 * s + s
        upper = (t % (2 * s)) >= s
        for i in range(C):
            if upper[i]:
                rng[2 + l, i, mid[i]:i + 1] = 1.0
            else:
                rng[2 + l, i, i + 1:mid[i]] = 1.0
        lvl_mask[l] = (blk[:, None] == blk[None, :]) & upper[:, None] & (~upper[None, :])
    lvl_mask[GLA_LEVELS] = np.eye(C)
    rng = rng.reshape((2 + GLA_LEVELS) * C, C)
    rng3 = np.concatenate([rng, rng, rng], axis=1)
    lvl_mask = np.tile(lvl_mask, (1, 1, GLA_HEADS))
    hk = np.kron(np.eye(GLA_HEADS), np.ones((C, GLA_DK)))
    hv = np.kron(np.eye(GLA_HEADS), np.ones((C, GLA_DV)))
    hs = np.kron(np.eye(GLA_HEADS), np.ones((GLA_DV, GLA_DK)))
    return (jnp.asarray(rng3, BF16), jnp.asarray(lvl_mask, F32), jnp.asarray(hk, BF16),
            jnp.asarray(hv, BF16), jnp.asarray(hs, F32))


def _gla_kernel(mild_ref, q_ref, k_ref, v_ref, g_ref, r_ref, rng_ref, lm_ref, tril_ref, hk_ref, hv_ref, hs_ref,
                nw_ref, o_ref, st_ref, *, ts, nb):
    C = GLA_CHUNK

    def where(j):
        return j % nb, pl.ds((j // nb) * C, C)

    @pl.when(pl.program_id(1) == 0)
    def _():
        st_ref[...] = jnp.zeros(st_ref.shape, F32)

    def gate_sums(c, mild):
        bb, rows = where(c)
        g = g_ref[bb, rows, :]
        g_hi = g.astype(BF16)
        rem = g - g_hi.astype(F32)
        g_mid = rem.astype(BF16)
        g_lo = (rem - g_mid.astype(F32)).astype(BF16)
        g3 = jnp.concatenate([g_hi, g_mid, g_lo], axis=0)
        rng = rng_ref[0:C, :] if mild else rng_ref[...]
        return jnp.dot(rng, g3, preferred_element_type=F32)

    def block_diag(kl):
        return jnp.concatenate([kl] * GLA_HEADS, axis=0) * hk_ref[...]

    def decays_mild(c, b):
        bb, rows = where(c)
        q = q_ref[bb, rows, :].astype(F32)
        k = k_ref[bb, rows, :].astype(F32)
        b_last = b[C - 1:C]
        ref = b[C // 2 - 1:C // 2]
        levels = [((q * jnp.exp(b - ref)).astype(BF16), block_diag((k * jnp.exp(ref - b)).astype(BF16)))]
        return dict(q_in=(q * jnp.exp(b)).astype(BF16),
                    k_out=(k * jnp.exp(b_last - b)).astype(BF16),
                    decay=jnp.exp(b_last),
                    levels=levels)

    def decays(c, e):
        bb, rows = where(c)
        q = q_ref[bb, rows, :].astype(F32)
        k = k_ref[bb, rows, :].astype(F32)
        f = jnp.exp(e)
        levels = []
        for l in range(GLA_LEVELS + 1):
            if l < GLA_LEVELS:
                fl = f[(2 + l) * C:(3 + l) * C]
                ql = (q * fl).astype(BF16)
                kl = (k * fl).astype(BF16)
            else:
                ql = q.astype(BF16)
                kl = k.astype(BF16)
            levels.append((ql, block_diag(kl)))
        return dict(q_in=(q * f[0:C]).astype(BF16),
                    k_out=(k * f[C:2 * C]).astype(BF16),
                    decay=f[C - 1:C],
                    levels=levels)

    def intra(d, mild):
        if mild:
            ql, k_bd = d["levels"][0]
            a = lax.dot_general(ql, k_bd, (((1,), (1,)), ((), ())), preferred_element_type=F32)
            return jnp.where(tril_ref[...] > 0.5, a, 0.0).astype(BF16)
        attn = jnp.zeros((C, GLA_HEADS * C), F32)
        for l, (ql, k_bd) in enumerate(d["levels"]):
            a = lax.dot_general(ql, k_bd, (((1,), (1,)), ((), ())), preferred_element_type=F32)
            attn = attn + a * lm_ref[l]
        return attn.astype(BF16)

    def output(c, d, attn):
        bb, rows = where(c)
        v = v_ref[bb, rows, :]
        v_bd = jnp.concatenate([v] * GLA_HEADS, axis=0) * hv_ref[...]
        st = st_ref[bb]
        o = jnp.dot(attn, v_bd, preferred_element_type=F32)
        o = o + lax.dot_general(d["q_in"], st.astype(BF16), (((1,), (1,)), ((), ())), preferred_element_type=F32)
        upd = lax.dot_general(v, d["k_out"], (((0,), (0,)), ((), ())), preferred_element_type=F32)
        st_ref[bb] = st * d["decay"] + upd * hs_ref[...]
        parts = []
        for h in range(GLA_HEADS):
            oh = o[:, h * GLA_DV:(h + 1) * GLA_DV]
            ms = jnp.mean(oh * oh, axis=1, keepdims=True)
            parts.append(oh * lax.rsqrt(ms + LN_EPS) * nw_ref[...])
        on = jnp.concatenate(parts, axis=1)
        o_ref[bb, rows, :] = (on * _silu(r_ref[bb, rows, :].astype(F32))).astype(BF16)

    n = nb * (ts // C)

    def run(mild):
        prep = decays_mild if mild else decays
        esum, dec, att = {0: gate_sums(0, mild)}, {}, {}
        for t in range(n + 2):
            if 0 <= t - 1 < n:
                att[t - 1] = intra(dec[t - 1], mild)
            if 0 <= t - 2 < n:
                output(t - 2, dec.pop(t - 2), att.pop(t - 2))
            if t + 1 < n:
                esum[t + 1] = gate_sums(t + 1, mild)
            if t < n:
                dec[t] = prep(t, esum.pop(t))

    mild = mild_ref[pl.program_id(0) * nb, pl.program_id(1)] != 0
    for t in range(1, nb):
        mild = jnp.logical_and(mild, mild_ref[pl.program_id(0) * nb + t, pl.program_id(1)] != 0)

    @pl.when(mild)
    def _():
        run(True)

    @pl.when(jnp.logical_not(mild))
    def _():
        run(False)


def _gla(mild, gq, gk, gv, gl, gr, nw, ts, nb):
    B, S, _ = gq.shape
    assert mild.shape == (B, S // ts) and B % nb == 0
    rng3, lvl_mask, hk, hv, hs = _gla_tables()
    tril = jnp.sum(lvl_mask, axis=0)
    row = lambda b, s, mild_ref: (b, s, 0)
    const2 = lambda b, s, mild_ref: (0, 0)
    const3 = lambda b, s, mild_ref: (0, 0, 0)
    grid_spec = pltpu.PrefetchScalarGridSpec(
        num_scalar_prefetch=1,
        grid=(B // nb, S // ts),
        in_specs=[pl.BlockSpec((nb, ts, GLA_KW), row),
                  pl.BlockSpec((nb, ts, GLA_KW), row),
                  pl.BlockSpec((nb, ts, GLA_VW), row),
                  pl.BlockSpec((nb, ts, GLA_KW), row),
                  pl.BlockSpec((nb, ts, GLA_VW), row),
                  pl.BlockSpec(rng3.shape, const2),
                  pl.BlockSpec(lvl_mask.shape, const3),
                  pl.BlockSpec(tril.shape, const2),
                  pl.BlockSpec(hk.shape, const2),
                  pl.BlockSpec(hv.shape, const2),
                  pl.BlockSpec(hs.shape, const2),
                  pl.BlockSpec((1, GLA_DV), const2)],
        out_specs=pl.BlockSpec((nb, ts, GLA_VW), row),
        scratch_shapes=[pltpu.VMEM((nb, GLA_VW, GLA_KW), F32)])
    return pl.pallas_call(
        functools.partial(_gla_kernel, ts=ts, nb=nb),
        grid_spec=grid_spec,
        out_shape=jax.ShapeDtypeStruct((B, S, GLA_VW), BF16),
        compiler_params=pltpu.CompilerParams(
            dimension_semantics=("arbitrary", "arbitrary"), vmem_limit_bytes=VMEM_LIMIT),
        name="gla",
    )(mild, gq, gk, gv, gl, gr, rng3, lvl_mask, tril, hk, hv, hs, nw)


def _mlp_kernel(x0_ref, d0_ref, g0_ref, gta0_ref, xn_ref, dlon_ref, dhin_ref, gn_ref, gtan_ref,
                lg_ref, lb_ref, wo_ref, ag_ref, ab_ref,
                sc_ref, sh_ref, gtf_ref, wu_ref, cw_ref, cb_ref, wd_ref, fg_ref, fb_ref, o_ref,
                carry_ref, act_ref, u_ref, h_ref, y_ref, hp_ref, *, tm, tf, nf, nt, half_tiles):
    SUB = SUBLANES
    blk = tm // SUB
    F = nf * tf
    lin = pl.program_id(0)
    s = lin % nt
    nl = h_ref.shape[0]
    hm = tm // 2

    def gather_rows(ref, start, size, stride):
        return jnp.concatenate([ref[c, pl.ds(start, size, stride=stride), :] for c in range(nl)], axis=1)

    def put_rows(ref, rows, val):
        for c in range(nl):
            ref[c, rows, :] = val[:, c * LANES:(c + 1) * LANES]

    def token_mix(x_ref, d_out, g_ref, gta_ref):
        for r in range(2):
            rows = slice(r * hm, (r + 1) * hm)
            h_in = _layer_norm(x_ref[0, rows, :], lg_ref[...], lb_ref[...])
            dw = d_out.shape[1]
            mix = (jnp.dot(d_out[rows], wo_ref[0:dw, :], preferred_element_type=F32)
                   + jnp.dot(g_ref[0, rows, :], wo_ref[dw:, :], preferred_element_type=F32))
            put_rows(h_ref, rows, _layer_norm(DN_ALPHA * h_in + (1.0 + gta_ref[0]) * mix, ag_ref[...], ab_ref[...]))

    @pl.when(lin == 0)
    def _():
        token_mix(x0_ref, d0_ref[0], g0_ref, gta0_ref)

    @pl.when(s == 0)
    def _():
        carry_ref[...] = jnp.zeros(carry_ref.shape, F32)

    for b in range(SUB):
        hb = gather_rows(h_ref, b, blk, SUB)
        hp_ref[b * blk:(b + 1) * blk, :] = hb
        u_ref[b * blk:(b + 1) * blk, :] = (hb * (1.0 + sc_ref[0]) + sh_ref[0]).astype(BF16)

    def shift_rows(block, before):
        return pltpu.roll(jnp.concatenate([before, block], axis=0), 1, axis=0)[SUB:SUB + blk]

    def conv_half(half, f):
        cols = slice(half * F + f * tf, half * F + (f + 1) * tf)
        up = jnp.dot(u_ref[...], wu_ref[:, cols], preferred_element_type=F32)
        lo, hi = (SUB - 2) * blk, (SUB - 1) * blk
        b6 = shift_rows(up[lo:hi], carry_ref[half, f, 0:SUB])
        b7 = shift_rows(up[hi:tm], carry_ref[half, f, SUB:2 * SUB])
        carry_ref[half, f, 0:SUB] = up[hi - SUB:hi]
        carry_ref[half, f, SUB:2 * SUB] = up[tm - SUB:tm]
        back1 = jnp.concatenate([b7, up[0:hi]], axis=0)
        back2 = jnp.concatenate([b6, b7, up[0:lo]], axis=0)
        cw = cw_ref[:, cols]
        return cb_ref[:, cols] + cw[0:1] * back2 + cw[1:2] * back1 + cw[2:3] * up

    for f in range(nf):
        act_ref[f] = (_silu(conv_half(0, f)) * conv_half(1, f)).astype(BF16)

    for r in range(2):
        rows = slice(r * hm, (r + 1) * hm)
        ff = jnp.dot(act_ref[0, rows, :], wd_ref[0], preferred_element_type=F32)
        for f in range(1, nf):
            ff = ff + jnp.dot(act_ref[f, rows, :], wd_ref[f], preferred_element_type=F32)
        for j in range(hm // blk):
            b = r * (hm // blk) + j
            y = DN_ALPHA * hp_ref[b * blk:(b + 1) * blk, :] + (1.0 + gtf_ref[0]) * ff[j * blk:(j + 1) * blk]
            put_rows(y_ref, pl.ds(b, blk, stride=SUB), _layer_norm(y, fg_ref[...], fb_ref[...]))

    o_ref[0] = jnp.concatenate([y_ref[c] for c in range(nl)], axis=1)

    s_next = (lin + 1) % nt
    token_mix(xn_ref, jnp.where(s_next < half_tiles, dlon_ref[0], dhin_ref[0]), gn_ref, gtan_ref)


def _mlp(x, d_lo, d_hi, g_out, ln_g, ln_b, gt_a, w_o, ag, ab, sc, sh, gt_f, wu, cw, cb, wd, fg, fb, tm):
    B, S, D = x.shape
    nf, tf, _ = wd.shape
    nt = S // tm
    half_tiles = d_lo.shape[1] // tm
    last = B * nt - 1

    def tile(lin):
        return lin // nt, lin % nt

    def nxt(lin):
        return tile(jnp.minimum(lin + 1, last))

    first = lambda lin: (0, 0, 0)

    def resident(shape):
        return pl.BlockSpec(shape, lambda lin: (0,) * len(shape), pipeline_mode=pl.Buffered(1))

    return pl.pallas_call(
        functools.partial(_mlp_kernel, tm=tm, tf=tf, nf=nf, nt=nt, half_tiles=half_tiles),
        grid=(B * nt,),
        in_specs=[pl.BlockSpec((1, tm, D), first),
                  pl.BlockSpec((1, tm, d_lo.shape[2]), first),
                  pl.BlockSpec((1, tm, g_out.shape[2]), first),
                  pl.BlockSpec((1, 1, D), first),
                  pl.BlockSpec((1, tm, D), lambda lin: (nxt(lin)[0], nxt(lin)[1], 0)),
                  pl.BlockSpec((1, tm, d_lo.shape[2]),
                               lambda lin: (nxt(lin)[0], jnp.minimum(nxt(lin)[1], half_tiles - 1), 0)),
                  pl.BlockSpec((1, tm, d_hi.shape[2]),
                               lambda lin: (nxt(lin)[0], jnp.maximum(nxt(lin)[1] - half_tiles, 0), 0)),
                  pl.BlockSpec((1, tm, g_out.shape[2]), lambda lin: (nxt(lin)[0], nxt(lin)[1], 0)),
                  pl.BlockSpec((1, 1, D), lambda lin: (nxt(lin)[0], 0, 0)),
                  resident((1, D)), resident((1, D)),
                  resident(w_o.shape),
                  resident((1, D)), resident((1, D)),
                  pl.BlockSpec((1, 1, D), lambda lin: (tile(lin)[0], 0, 0)),
                  pl.BlockSpec((1, 1, D), lambda lin: (tile(lin)[0], 0, 0)),
                  pl.BlockSpec((1, 1, D), lambda lin: (tile(lin)[0], 0, 0)),
                  resident(wu.shape), resident(cw.shape), resident(cb.shape), resident(wd.shape),
                  resident((1, D)), resident((1, D))],
        out_specs=pl.BlockSpec((1, tm, D), lambda lin: (tile(lin)[0], tile(lin)[1], 0)),
        out_shape=jax.ShapeDtypeStruct((B, S, D), F32),
        scratch_shapes=[pltpu.VMEM((2, nf, 2 * SUBLANES, tf), F32),
                        pltpu.VMEM((nf, tm, tf), BF16),
                        pltpu.VMEM((tm, D), BF16),
                        pltpu.VMEM((D // LANES, tm, LANES), F32),
                        pltpu.VMEM((D // LANES, tm, LANES), F32),
                        pltpu.VMEM((tm, D), F32)],
        compiler_params=pltpu.CompilerParams(
            dimension_semantics=("arbitrary",), vmem_limit_bytes=VMEM_LIMIT),
        name="mlp",
    )(x, d_lo, g_out, gt_a, x, d_lo, d_hi, g_out, gt_a, ln_g, ln_b, w_o, ag, ab, sc, sh, gt_f,
      wu, cw, cb, wd, fg, fb)


def kernel(x, c, positions, ln_in_g, ln_in_b, w_ada, b_ada, w_in, lambda_q1, lambda_k1, lambda_q2, lambda_k2, diff_norm_w, gla_w_gate_up, gla_b_gate, gla_norm_w, w_out, ln_attn_g, ln_attn_b, w_up, conv_w, conv_b, w_down, ln_ffn_g, ln_ffn_b):
    B, S, D = x.shape
    assert D == D_MODEL and w_ada.shape[0] == 1
    assert conv_w.shape[1] == CONV_W and w_up.shape[2] == 2 * D_FF
    tm = min(512, S)
    tf = 256
    nf = D_FF // tf

    c_pad = jnp.pad(c, ((0, -B % SUBLANES), (0, 0)))
    ada = _ada(c_pad, w_ada[0], b_ada)[:B]
    sh_a, sc_a, gt_a, sh_f, sc_f, gt_f = [t[:, None, :] for t in jnp.split(ada, 6, axis=-1)]

    ln_g = ln_in_g[None, :]
    ln_b = ln_in_b[None, :]

    w_proj = jnp.pad(w_in[0].astype(BF16), ((0, 0), (0, N_PROJ - w_in.shape[2])))
    w_gate = jnp.pad(gla_w_gate_up[0], ((0, LANES - GLA_RANK), (0, 0))).astype(BF16)
    inv = ROPE_THETA ** (-jnp.arange(0, DIFF_DH, 2, dtype=F32) / DIFF_DH)
    qk, dv, gq, gk, gv, gr, gl, mild = _proj(x, positions[:, None, :], inv[:, None], ln_g, ln_b, sc_a, sh_a,
                                       w_proj, w_gate, gla_b_gate, tm)

    d_lo, d_hi = _diffattn(qk, dv, lambda_q1, lambda_k1, lambda_q2, lambda_k2, diff_norm_w, tm, 2)
    g_out = _gla(mild[:, :, 0, 0], gq, gk, gv, gl, gr, gla_norm_w, tm, 2)

    wd = w_down[0].astype(BF16).reshape(nf, tf, D)
    return _mlp(x, d_lo, d_hi, g_out, ln_g, ln_b, gt_a, w_out[0].astype(BF16), ln_attn_g, ln_attn_b,
                sc_f, sh_f, gt_f, w_up[0].astype(BF16), conv_w[0], conv_b, wd, ln_ffn_g, ln_ffn_b, tm)
```

```python
import functools
import math

import numpy as np
import jax
import jax.numpy as jnp
from jax import lax
from jax.experimental import pallas as pl
from jax.experimental.pallas import tpu as pltpu

F32 = jnp.float32
BF16 = jnp.bfloat16

D_MODEL = 1024
DIFF_DH = 64
DIFF_HEADS = 4
HEAD_W = 2 * DIFF_DH
GLA_HEADS = 4
GLA_DK = 64
GLA_DV = 128
GLA_KW = GLA_HEADS * GLA_DK
GLA_VW = GLA_HEADS * GLA_DV
GLA_RANK = 16
GLA_TAU = 16.0
GLA_CHUNK = 64
GLA_LEVELS = 6
GLA_MILD_DECAY = 30.0
D_FF = 2816
CONV_W = 3
ROPE_THETA = 10000.0
LN_EPS = 1e-5
DEPTH = 1
DN_ALPHA = (2.0 * DEPTH) ** 0.25
LAMBDA_INIT = 0.8 - 0.6 * math.exp(-0.3 * 0)

N_MAIN = 3072
N_PROJ = N_MAIN + 128
LANES = 128
SUBLANES = 8
NEG_BIG = -1e30
LOG2_E = math.log2(math.e)

VMEM_LIMIT = 56 * 1024 * 1024


def _layer_norm(x, g, b):
    mu = jnp.mean(x, axis=-1, keepdims=True)
    xc = x - mu
    var = jnp.mean(xc * xc, axis=-1, keepdims=True)
    return xc * lax.rsqrt(var + LN_EPS) * g + b


def _silu(x):
    return x * jax.nn.sigmoid(x)


def _ada_kernel(c_ref, w_ref, b_ref, o_ref):
    ca = _silu(c_ref[...])
    o_ref[...] = jnp.dot(ca.astype(BF16), w_ref[...].astype(BF16),
                         preferred_element_type=F32) + b_ref[...]


def _ada(c_pad, w_ada, b_ada):
    rows, d = c_pad.shape
    n = w_ada.shape[1]
    tn = 1024
    return pl.pallas_call(
        _ada_kernel,
        grid=(n // tn,),
        in_specs=[pl.BlockSpec((rows, d), lambda j: (0, 0)),
                  pl.BlockSpec((d, tn), lambda j: (0, j)),
                  pl.BlockSpec((1, tn), lambda j: (0, j))],
        out_specs=pl.BlockSpec((rows, tn), lambda j: (0, j)),
        out_shape=jax.ShapeDtypeStruct((rows, n), F32),
        name="ada",
    )(c_pad, w_ada, b_ada)


def _proj_kernel(x_ref, pos_ref, inv_ref, lg_ref, lb_ref, sc_ref, sh_ref, w_ref, wg_ref, bg_ref,
                 qk_ref, dv_ref, gq_ref, gk_ref, gv_ref, gr_ref, gl_ref, mild_ref, *, tm):
    ang_t = inv_ref[...] * pos_ref[0].astype(F32)
    reps = LANES // ang_t.shape[0]
    cos = jnp.concatenate([jnp.cos(ang_t)] * reps, axis=0).T
    sin = jnp.concatenate([jnp.sin(ang_t)] * reps, axis=0).T
    lane = lax.broadcasted_iota(jnp.int32, (1, LANES), 1)
    first_half = (lane & 32) == 0
    sin_signed = jnp.where(first_half, -sin, sin)
    scale = LOG2_E / math.sqrt(DIFF_DH)

    hm = tm // 2
    worst = None
    for part in range(2):
        rows = slice(part * hm, (part + 1) * hm)
        h = _layer_norm(x_ref[0, rows, :], lg_ref[...], lb_ref[...])
        u = (h * (1.0 + sc_ref[0]) + sh_ref[0]).astype(BF16)
        proj = jnp.dot(u, w_ref[...], preferred_element_type=F32)
        for j in range(8):
            xg = proj[:, j * LANES:(j + 1) * LANES]
            partner = jnp.where(first_half, pltpu.roll(xg, LANES - 32, axis=1), pltpu.roll(xg, 32, axis=1))
            r = xg * cos[rows] + partner * sin_signed[rows]
            if j < 4:
                r = r * scale
            qk_ref[0, rows, j * LANES:(j + 1) * LANES] = r.astype(BF16)
        dv_ref[0, rows, :] = proj[:, 1024:1536].astype(BF16)
        gq_ref[0, rows, :] = (proj[:, 1536:1792] * (GLA_DK ** -0.5)).astype(BF16)
        gk_ref[0, rows, :] = proj[:, 1792:2048].astype(BF16)
        gv_ref[0, rows, :] = proj[:, 2048:2560].astype(BF16)
        gr_ref[0, rows, :] = proj[:, 2560:3072].astype(BF16)
        gg = proj[:, N_MAIN:N_PROJ].astype(BF16)
        z = jnp.dot(gg, wg_ref[...], preferred_element_type=F32) + bg_ref[...]
        log_sig = jnp.minimum(z, 0.0) - jnp.log1p(jnp.exp(-jnp.abs(z)))
        log_gate = log_sig * (1.0 / GLA_TAU)
        gl_ref[0, rows, :] = log_gate
        for c in range(hm // GLA_CHUNK):
            total = jnp.sum(log_gate[c * GLA_CHUNK:(c + 1) * GLA_CHUNK], axis=0, keepdims=True)
            worst = total if worst is None else jnp.minimum(worst, total)
    mild = jnp.min(worst, axis=1, keepdims=True) >= -GLA_MILD_DECAY
    mild_ref[0, 0] = jnp.broadcast_to(mild.astype(jnp.int32), mild_ref.shape[2:])


def _proj(x, pos_row, inv_col, ln_g, ln_b, sc, sh, w, wg, bg, tm):
    B, S, D = x.shape
    grid = (B, S // tm)
    row = lambda b, s: (b, s, 0)
    const2 = lambda b, s: (0, 0)
    per_b = lambda b, s: (b, 0, 0)
    widths = (1024, 512, GLA_KW, GLA_KW, GLA_VW, GLA_VW)
    out_shape = [jax.ShapeDtypeStruct((B, S, n), BF16) for n in widths]
    out_shape.append(jax.ShapeDtypeStruct((B, S, GLA_KW), F32))
    out_specs = [pl.BlockSpec((1, tm, n), row) for n in widths]
    out_specs.append(pl.BlockSpec((1, tm, GLA_KW), row))
    out_shape.append(jax.ShapeDtypeStruct((B, S // tm, SUBLANES, LANES), jnp.int32))
    out_specs.append(pl.BlockSpec((1, 1, SUBLANES, LANES), lambda b, s: (b, s, 0, 0)))
    return pl.pallas_call(
        functools.partial(_proj_kernel, tm=tm),
        grid=grid,
        in_specs=[pl.BlockSpec((1, tm, D), row),
                  pl.BlockSpec((1, 1, tm), lambda b, s: (b, 0, s)),
                  pl.BlockSpec(inv_col.shape, const2),
                  pl.BlockSpec((1, D), const2),
                  pl.BlockSpec((1, D), const2),
                  pl.BlockSpec((1, 1, D), per_b),
                  pl.BlockSpec((1, 1, D), per_b),
                  pl.BlockSpec((D, N_PROJ), const2),
                  pl.BlockSpec((LANES, GLA_KW), const2),
                  pl.BlockSpec((1, GLA_KW), const2)],
        out_specs=out_specs,
        out_shape=out_shape,
        compiler_params=pltpu.CompilerParams(
            dimension_semantics=("arbitrary", "arbitrary"), vmem_limit_bytes=VMEM_LIMIT),
        name="proj",
    )(x, pos_row, inv_col, ln_g, ln_b, sc, sh, w, wg, bg)


def _diffattn_kernel(qa_ref, qb_ref, k_ref, v_ref, lq1_ref, lk1_ref, lq2_ref, lk2_ref, nw_ref,
                     oa_ref, ob_ref, qz_ref, m_ref, acc_ref, *, tq, nt, hps):
    i = pl.program_id(2)
    lane = lax.broadcasted_iota(jnp.int32, (1, HEAD_W), 1)
    for hh in range(hps):
        cols = slice(hh * HEAD_W, (hh + 1) * HEAD_W)
        for t, q_ref in enumerate((qa_ref, qb_ref)):
            q = q_ref[0, :, cols]
            zero = jnp.zeros_like(q)
            qz_ref[hh, t] = jnp.concatenate([jnp.where(lane < DIFF_DH, q, zero),
                                             jnp.where(lane >= DIFF_DH, q, zero)], axis=0)
    m_ref[...] = jnp.full(m_ref.shape, NEG_BIG, F32)
    acc_ref[...] = jnp.zeros(acc_ref.shape, F32)
    ones = jnp.ones((tq, LANES), BF16)

    def block_step(hh, own, blk, q_lo=0, k_lo=0, k_len=None, masked=False):
        k_len = tq if k_len is None else k_len
        nq = tq - q_lo
        cols = slice(hh * HEAD_W, (hh + 1) * HEAD_W)
        rows = pl.ds(pl.multiple_of(blk * tq + k_lo, k_len), k_len)
        spans = (slice(q_lo, tq), slice(tq + q_lo, 2 * tq))

        def load(ref):
            return jnp.concatenate([ref[hh, own, sp, :] for sp in spans], axis=0)

        kb = k_ref[0, rows, cols]
        vb = jnp.concatenate([v_ref[0, rows, cols], ones[0:k_len]], axis=1)
        s = lax.dot_general(load(qz_ref), kb, (((1,), (1,)), ((), ())), preferred_element_type=F32)
        if masked:
            r = lax.broadcasted_iota(jnp.int32, (2 * nq, k_len), 0)
            c = lax.broadcasted_iota(jnp.int32, (2 * nq, k_len), 1)
            r = jnp.where(r >= nq, r - nq, r)
            s = jnp.where(c + k_lo <= r + q_lo, s, NEG_BIG)
        s_fold = s[:, 0:LANES]
        for t in range(1, k_len // LANES):
            s_fold = jnp.maximum(s_fold, s[:, t * LANES:(t + 1) * LANES])
        m_prev = load(m_ref)
        m_new = jnp.maximum(m_prev, jnp.max(s_fold, axis=1, keepdims=True))
        alpha = jnp.exp2(m_prev - m_new)
        p = jnp.exp2(s - jnp.concatenate([m_new] * (k_len // LANES), axis=1))
        acc = (jnp.concatenate([alpha, alpha], axis=1) * load(acc_ref)
               + jnp.dot(p.astype(BF16), vb, preferred_element_type=F32))
        for n, sp in enumerate(spans):
            acc_ref[hh, own, sp, :] = acc[n * nq:(n + 1) * nq]
            m_ref[hh, own, sp, :] = m_new[n * nq:(n + 1) * nq]

    def diagonal(hh, own, blk):
        hk = tq // 2
        block_step(hh, own, blk, k_len=hk, masked=True)
        block_step(hh, own, blk, q_lo=hk, k_lo=hk, k_len=hk, masked=True)

    for hh in range(hps):
        diagonal(hh, 0, i)
    for t in range(1, nt):
        is_a = t <= i
        for hh in range(hps):
            block_step(hh, jnp.where(is_a, 0, 1), jnp.where(is_a, i - t, t - i - 1))
    for hh in range(hps):
        diagonal(hh, 1, nt - 1 - i)

    lam = (jnp.exp(jnp.sum(lq1_ref[...] * lk1_ref[...], axis=1, keepdims=True))
           - jnp.exp(jnp.sum(lq2_ref[...] * lk2_ref[...], axis=1, keepdims=True)) + LAMBDA_INIT)
    for hh in range(hps):
        cols = slice(hh * HEAD_W, (hh + 1) * HEAD_W)
        for t, o_ref in enumerate((oa_ref, ob_ref)):
            acc = acc_ref[hh, t]
            o = acc[:, 0:HEAD_W] / acc[:, HEAD_W:2 * HEAD_W]
            d = o[0:tq] - lam * o[tq:2 * tq]
            ms = jnp.mean(d * d, axis=1, keepdims=True)
            o_ref[0, :, cols] = (d * lax.rsqrt(ms + LN_EPS) * nw_ref[...] * (1.0 - LAMBDA_INIT)).astype(BF16)


def _diffattn(qk, dv, lq1, lk1, lq2, lk2, nw, tq, hps):
    B, S, _ = qk.shape
    nt = S // tq
    assert nt % 2 == 0, "query tiles are processed in pairs (i, nt-1-i)"
    assert DIFF_HEADS % hps == 0
    half = nt // 2
    groups = DIFF_HEADS // hps
    gw = hps * HEAD_W
    const2 = lambda b, h, i: (0, 0)
    out = jax.ShapeDtypeStruct((B, S // 2, DIFF_HEADS * HEAD_W), BF16)
    lo, hi = pl.pallas_call(
        functools.partial(_diffattn_kernel, tq=tq, nt=nt, hps=hps),
        grid=(B, groups, half),
        in_specs=[pl.BlockSpec((1, tq, gw), lambda b, h, i: (b, i, h)),
                  pl.BlockSpec((1, tq, gw), lambda b, h, i: (b, nt - 1 - i, h)),
                  pl.BlockSpec((1, S, gw), lambda b, h, i: (b, 0, groups + h)),
                  pl.BlockSpec((1, S, gw), lambda b, h, i: (b, 0, h)),
                  pl.BlockSpec((1, DIFF_DH), const2),
                  pl.BlockSpec((1, DIFF_DH), const2),
                  pl.BlockSpec((1, DIFF_DH), const2),
                  pl.BlockSpec((1, DIFF_DH), const2),
                  pl.BlockSpec((1, HEAD_W), const2)],
        out_specs=[pl.BlockSpec((1, tq, gw), lambda b, h, i: (b, i, h)),
                   pl.BlockSpec((1, tq, gw), lambda b, h, i: (b, half - 1 - i, h))],
        out_shape=[out, out],
        scratch_shapes=[pltpu.VMEM((hps, 2, 2 * tq, HEAD_W), BF16),
                        pltpu.VMEM((hps, 2, 2 * tq, LANES), F32),
                        pltpu.VMEM((hps, 2, 2 * tq, 2 * HEAD_W), F32)],
        compiler_params=pltpu.CompilerParams(
            dimension_semantics=("arbitrary", "arbitrary", "arbitrary"), vmem_limit_bytes=VMEM_LIMIT),
        name="diffattn",
    )(qk, qk, qk, dv, lq1, lk1, lq2, lk2, nw)
    return lo, hi


def _gla_tables():
    C = GLA_CHUNK
    t = np.arange(C)
    rng = np.zeros((2 + GLA_LEVELS, C, C), np.float32)
    rng[0] = (t[None, :] <= t[:, None])
    rng[1] = (t[None, :] > t[:, None])
    lvl_mask = np.zeros((GLA_LEVELS + 1, C, C), np.float32)
    for l in range(GLA_LEVELS):
        s = C >> (l + 1)
        blk = t // (2 * s)
        mid = blk * 2 * s + s
        upper = (t % (2 * s)) >= s
        for i in range(C):
            if upper[i]:
                rng[2 + l, i, mid[i]:i + 1] = 1.0
            else:
                rng[2 + l, i, i + 1:mid[i]] = 1.0
        lvl_mask[l] = (blk[:, None] == blk[None, :]) & upper[:, None] & (~upper[None, :])
    lvl_mask[GLA_LEVELS] = np.eye(C)
    rng = rng.reshape((2 + GLA_LEVELS) * C, C)
    rng3 = np.concatenate([rng, rng, rng], axis=1)
    lvl_mask = np.tile(lvl_mask, (1, 1, GLA_HEADS))
    hk = np.kron(np.eye(GLA_HEADS), np.ones((C, GLA_DK)))
    hv = np.kron(np.eye(GLA_HEADS), np.ones((C, GLA_DV)))
    hs = np.kron(np.eye(GLA_HEADS), np.ones((GLA_DV, GLA_DK)))
    return (jnp.asarray(rng3, BF16), jnp.asarray(lvl_mask, F32), jnp.asarray(hk, BF16),
            jnp.asarray(hv, BF16), jnp.asarray(hs, F32))


def _gla_kernel(mild_ref, q_ref, k_ref, v_ref, g_ref, r_ref, rng_ref, lm_ref, tril_ref, hk_ref, hv_ref, hs_ref,
                nw_ref, o_ref, st_ref, *, ts, nb, flag_rows):
    C = GLA_CHUNK

    def where(j):
        return j % nb, pl.ds((j // nb) * C, C)

    @pl.when(pl.program_id(1) == 0)
    def _():
        st_ref[...] = jnp.zeros(st_ref.shape, F32)

    def gate_sums(c, mild):
        bb, rows = where(c)
        g = g_ref[bb, rows, :]
        g_hi = g.astype(BF16)
        rem = g - g_hi.astype(F32)
        g_mid = rem.astype(BF16)
        g_lo = (rem - g_mid.astype(F32)).astype(BF16)
        g3 = jnp.concatenate([g_hi, g_mid, g_lo], axis=0)
        rng = rng_ref[0:C, :] if mild else rng_ref[...]
        return jnp.dot(rng, g3, preferred_element_type=F32)

    def block_diag(kl):
        return jnp.concatenate([kl] * GLA_HEADS, axis=0) * hk_ref[...]

    def decays_mild(c, b):
        bb, rows = where(c)
        q = q_ref[bb, rows, :].astype(F32)
        k = k_ref[bb, rows, :].astype(F32)
        b_last = b[C - 1:C]
        ref = b[C // 2 - 1:C // 2]
        levels = [((q * jnp.exp(b - ref)).astype(BF16), block_diag((k * jnp.exp(ref - b)).astype(BF16)))]
        return dict(q_in=(q * jnp.exp(b)).astype(BF16),
                    k_out=(k * jnp.exp(b_last - b)).astype(BF16),
                    decay=jnp.exp(b_last),
                    levels=levels)

    def decays(c, e):
        bb, rows = where(c)
        q = q_ref[bb, rows, :].astype(F32)
        k = k_ref[bb, rows, :].astype(F32)
        f = jnp.exp(e)
        levels = []
        for l in range(GLA_LEVELS + 1):
            if l < GLA_LEVELS:
                fl = f[(2 + l) * C:(3 + l) * C]
                ql = (q * fl).astype(BF16)
                kl = (k * fl).astype(BF16)
            else:
                ql = q.astype(BF16)
                kl = k.astype(BF16)
            levels.append((ql, block_diag(kl)))
        return dict(q_in=(q * f[0:C]).astype(BF16),
                    k_out=(k * f[C:2 * C]).astype(BF16),
                    decay=f[C - 1:C],
                    levels=levels)

    def intra(d, mild):
        if mild:
            ql, k_bd = d["levels"][0]
            a = lax.dot_general(ql, k_bd, (((1,), (1,)), ((), ())), preferred_element_type=F32)
            return jnp.where(tril_ref[...] > 0.5, a, 0.0).astype(BF16)
        attn = jnp.zeros((C, GLA_HEADS * C), F32)
        for l, (ql, k_bd) in enumerate(d["levels"]):
            a = lax.dot_general(ql, k_bd, (((1,), (1,)), ((), ())), preferred_element_type=F32)
            attn = attn + a * lm_ref[l]
        return attn.astype(BF16)

    def output(c, d, attn):
        bb, rows = where(c)
        v = v_ref[bb, rows, :]
        v_bd = jnp.concatenate([v] * GLA_HEADS, axis=0) * hv_ref[...]
        st = st_ref[bb]
        o = jnp.dot(attn, v_bd, preferred_element_type=F32)
        o = o + lax.dot_general(d["q_in"], st.astype(BF16), (((1,), (1,)), ((), ())), preferred_element_type=F32)
        upd = lax.dot_general(v, d["k_out"], (((0,), (0,)), ((), ())), preferred_element_type=F32)
        st_ref[bb] = st * d["decay"] + upd * hs_ref[...]
        parts = []
        for h in range(GLA_HEADS):
            oh = o[:, h * GLA_DV:(h + 1) * GLA_DV]
            ms = jnp.mean(oh * oh, axis=1, keepdims=True)
            parts.append(oh * lax.rsqrt(ms + LN_EPS) * nw_ref[...])
        on = jnp.concatenate(parts, axis=1)
        o_ref[bb, rows, :] = (on * _silu(r_ref[bb, rows, :].astype(F32))).astype(BF16)

    n = nb * (ts // C)

    def run(mild):
        prep = decays_mild if mild else decays
        esum, dec, att = {0: gate_sums(0, mild)}, {}, {}
        for t in range(n + 2):
            if 0 <= t - 1 < n:
                att[t - 1] = intra(dec[t - 1], mild)
            if 0 <= t - 2 < n:
                output(t - 2, dec.pop(t - 2), att.pop(t - 2))
            if t + 1 < n:
                esum[t + 1] = gate_sums(t + 1, mild)
            if t < n:
                dec[t] = prep(t, esum.pop(t))

    col = (pl.program_id(1) * ts) // flag_rows
    mild = mild_ref[pl.program_id(0) * nb, col] != 0
    for t in range(1, nb):
        mild = jnp.logical_and(mild, mild_ref[pl.program_id(0) * nb + t, col] != 0)

    @pl.when(mild)
    def _():
        run(True)

    @pl.when(jnp.logical_not(mild))
    def _():
        run(False)


def _gla(mild, gq, gk, gv, gl, gr, nw, ts, nb):
    B, S, _ = gq.shape
    flag_rows = S // mild.shape[1]
    assert flag_rows % ts == 0 and B % nb == 0
    rng3, lvl_mask, hk, hv, hs = _gla_tables()
    tril = jnp.sum(lvl_mask, axis=0)
    row = lambda b, s, mild_ref: (b, s, 0)
    const2 = lambda b, s, mild_ref: (0, 0)
    const3 = lambda b, s, mild_ref: (0, 0, 0)
    grid_spec = pltpu.PrefetchScalarGridSpec(
        num_scalar_prefetch=1,
        grid=(B // nb, S // ts),
        in_specs=[pl.BlockSpec((nb, ts, GLA_KW), row),
                  pl.BlockSpec((nb, ts, GLA_KW), row),
                  pl.BlockSpec((nb, ts, GLA_VW), row),
                  pl.BlockSpec((nb, ts, GLA_KW), row),
                  pl.BlockSpec((nb, ts, GLA_VW), row),
                  pl.BlockSpec(rng3.shape, const2),
                  pl.BlockSpec(lvl_mask.shape, const3),
                  pl.BlockSpec(tril.shape, const2),
                  pl.BlockSpec(hk.shape, const2),
                  pl.BlockSpec(hv.shape, const2),
                  pl.BlockSpec(hs.shape, const2),
                  pl.BlockSpec((1, GLA_DV), const2)],
        out_specs=pl.BlockSpec((nb, ts, GLA_VW), row),
        scratch_shapes=[pltpu.VMEM((nb, GLA_VW, GLA_KW), F32)])
    return pl.pallas_call(
        functools.partial(_gla_kernel, ts=ts, nb=nb, flag_rows=flag_rows),
        grid_spec=grid_spec,
        out_shape=jax.ShapeDtypeStruct((B, S, GLA_VW), BF16),
        compiler_params=pltpu.CompilerParams(
            dimension_semantics=("arbitrary", "arbitrary"), vmem_limit_bytes=VMEM_LIMIT),
        name="gla",
    )(mild, gq, gk, gv, gl, gr, rng3, lvl_mask, tril, hk, hv, hs, nw)


def _mlp_kernel(x0_ref, d0_ref, g0_ref, gta0_ref, xn_ref, dlon_ref, dhin_ref, gn_ref, gtan_ref,
                lg_ref, lb_ref, wo_ref, ag_ref, ab_ref,
                sc_ref, sh_ref, gtf_ref, wu_ref, cw_ref, cb_ref, wd_ref, fg_ref, fb_ref, o_ref,
                carry_ref, act_ref, u_ref, h_ref, y_ref, hp_ref, *, tm, tf, nf, nt, half_tiles):
    SUB = SUBLANES
    blk = tm // SUB
    F = nf * tf
    lin = pl.program_id(0)
    s = lin % nt
    nl = h_ref.shape[0]
    hm = tm // 2

    def gather_rows(ref, start, size, stride):
        return jnp.concatenate([ref[c, pl.ds(start, size, stride=stride), :] for c in range(nl)], axis=1)

    def put_rows(ref, rows, val):
        for c in range(nl):
            ref[c, rows, :] = val[:, c * LANES:(c + 1) * LANES]

    def token_mix(x_ref, d_out, g_ref, gta_ref):
        for r in range(2):
            rows = slice(r * hm, (r + 1) * hm)
            h_in = _layer_norm(x_ref[0, rows, :], lg_ref[...], lb_ref[...])
            dw = d_out.shape[1]
            mix = (jnp.dot(d_out[rows], wo_ref[0:dw, :], preferred_element_type=F32)
                   + jnp.dot(g_ref[0, rows, :], wo_ref[dw:, :], preferred_element_type=F32))
            put_rows(h_ref, rows, _layer_norm(DN_ALPHA * h_in + (1.0 + gta_ref[0]) * mix, ag_ref[...], ab_ref[...]))

    @pl.when(lin == 0)
    def _():
        token_mix(x0_ref, d0_ref[0], g0_ref, gta0_ref)

    @pl.when(s == 0)
    def _():
        carry_ref[...] = jnp.zeros(carry_ref.shape, F32)

    for b in range(SUB):
        hb = gather_rows(h_ref, b, blk, SUB)
        hp_ref[b * blk:(b + 1) * blk, :] = hb
        u_ref[b * blk:(b + 1) * blk, :] = (hb * (1.0 + sc_ref[0]) + sh_ref[0]).astype(BF16)

    def shift_rows(block, before):
        return pltpu.roll(jnp.concatenate([before, block], axis=0), 1, axis=0)[SUB:SUB + blk]

    def conv_half(half, f):
        cols = slice(half * F + f * tf, half * F + (f + 1) * tf)
        up = jnp.dot(u_ref[...], wu_ref[:, cols], preferred_element_type=F32)
        lo, hi = (SUB - 2) * blk, (SUB - 1) * blk
        b6 = shift_rows(up[lo:hi], carry_ref[half, f, 0:SUB])
        b7 = shift_rows(up[hi:tm], carry_ref[half, f, SUB:2 * SUB])
        carry_ref[half, f, 0:SUB] = up[hi - SUB:hi]
        carry_ref[half, f, SUB:2 * SUB] = up[tm - SUB:tm]
        back1 = jnp.concatenate([b7, up[0:hi]], axis=0)
        back2 = jnp.concatenate([b6, b7, up[0:lo]], axis=0)
        cw = cw_ref[:, cols]
        return cb_ref[:, cols] + cw[0:1] * back2 + cw[1:2] * back1 + cw[2:3] * up

    for f in range(nf):
        act_ref[f] = (_silu(conv_half(0, f)) * conv_half(1, f)).astype(BF16)

    for r in range(2):
        rows = slice(r * hm, (r + 1) * hm)
        ff = jnp.dot(act_ref[0, rows, :], wd_ref[0], preferred_element_type=F32)
        for f in range(1, nf):
            ff = ff + jnp.dot(act_ref[f, rows, :], wd_ref[f], preferred_element_type=F32)
        for j in range(hm // blk):
            b = r * (hm // blk) + j
            y = DN_ALPHA * hp_ref[b * blk:(b + 1) * blk, :] + (1.0 + gtf_ref[0]) * ff[j * blk:(j + 1) * blk]
            put_rows(y_ref, pl.ds(b, blk, stride=SUB), _layer_norm(y, fg_ref[...], fb_ref[...]))

    o_ref[0] = jnp.concatenate([y_ref[c] for c in range(nl)], axis=1)

    s_next = (lin + 1) % nt
    token_mix(xn_ref, jnp.where(s_next < half_tiles, dlon_ref[0], dhin_ref[0]), gn_ref, gtan_ref)


def _mlp(x, d_lo, d_hi, g_out, ln_g, ln_b, gt_a, w_o, ag, ab, sc, sh, gt_f, wu, cw, cb, wd, fg, fb, tm):
    B, S, D = x.shape
    nf, tf, _ = wd.shape
    nt = S // tm
    half_tiles = d_lo.shape[1] // tm
    last = B * nt - 1

    def tile(lin):
        return lin // nt, lin % nt

    def nxt(lin):
        return tile(jnp.minimum(lin + 1, last))

    first = lambda lin: (0, 0, 0)

    def resident(shape):
        return pl.BlockSpec(shape, lambda lin: (0,) * len(shape), pipeline_mode=pl.Buffered(1))

    return pl.pallas_call(
        functools.partial(_mlp_kernel, tm=tm, tf=tf, nf=nf, nt=nt, half_tiles=half_tiles),
        grid=(B * nt,),
        in_specs=[pl.BlockSpec((1, tm, D), first),
                  pl.BlockSpec((1, tm, d_lo.shape[2]), first),
                  pl.BlockSpec((1, tm, g_out.shape[2]), first),
                  pl.BlockSpec((1, 1, D), first),
                  pl.BlockSpec((1, tm, D), lambda lin: (nxt(lin)[0], nxt(lin)[1], 0)),
                  pl.BlockSpec((1, tm, d_lo.shape[2]),
                               lambda lin: (nxt(lin)[0], jnp.minimum(nxt(lin)[1], half_tiles - 1), 0)),
                  pl.BlockSpec((1, tm, d_hi.shape[2]),
                               lambda lin: (nxt(lin)[0], jnp.maximum(nxt(lin)[1] - half_tiles, 0), 0)),
                  pl.BlockSpec((1, tm, g_out.shape[2]), lambda lin: (nxt(lin)[0], nxt(lin)[1], 0)),
                  pl.BlockSpec((1, 1, D), lambda lin: (nxt(lin)[0], 0, 0)),
                  resident((1, D)), resident((1, D)),
                  resident(w_o.shape),
                  resident((1, D)), resident((1, D)),
                  pl.BlockSpec((1, 1, D), lambda lin: (tile(lin)[0], 0, 0)),
                  pl.BlockSpec((1, 1, D), lambda lin: (tile(lin)[0], 0, 0)),
                  pl.BlockSpec((1, 1, D), lambda lin: (tile(lin)[0], 0, 0)),
                  resident(wu.shape), resident(cw.shape), resident(cb.shape), resident(wd.shape),
                  resident((1, D)), resident((1, D))],
        out_specs=pl.BlockSpec((1, tm, D), lambda lin: (tile(lin)[0], tile(lin)[1], 0)),
        out_shape=jax.ShapeDtypeStruct((B, S, D), F32),
        scratch_shapes=[pltpu.VMEM((2, nf, 2 * SUBLANES, tf), F32),
                        pltpu.VMEM((nf, tm, tf), BF16),
                        pltpu.VMEM((tm, D), BF16),
                        pltpu.VMEM((D // LANES, tm, LANES), F32),
                        pltpu.VMEM((D // LANES, tm, LANES), F32),
                        pltpu.VMEM((tm, D), F32)],
        compiler_params=pltpu.CompilerParams(
            dimension_semantics=("arbitrary",), vmem_limit_bytes=VMEM_LIMIT),
        name="mlp",
    )(x, d_lo, g_out, gt_a, x, d_lo, d_hi, g_out, gt_a, ln_g, ln_b, w_o, ag, ab, sc, sh, gt_f,
      wu, cw, cb, wd, fg, fb)


def kernel(x, c, positions, ln_in_g, ln_in_b, w_ada, b_ada, w_in, lambda_q1, lambda_k1, lambda_q2, lambda_k2, diff_norm_w, gla_w_gate_up, gla_b_gate, gla_norm_w, w_out, ln_attn_g, ln_attn_b, w_up, conv_w, conv_b, w_down, ln_ffn_g, ln_ffn_b):
    B, S, D = x.shape
    assert D == D_MODEL and w_ada.shape[0] == 1
    assert conv_w.shape[1] == CONV_W and w_up.shape[2] == 2 * D_FF
    tm = min(512, S)
    tf = 256
    nf = D_FF // tf

    c_pad = jnp.pad(c, ((0, -B % SUBLANES), (0, 0)))
    ada = _ada(c_pad, w_ada[0], b_ada)[:B]
    sh_a, sc_a, gt_a, sh_f, sc_f, gt_f = [t[:, None, :] for t in jnp.split(ada, 6, axis=-1)]

    ln_g = ln_in_g[None, :]
    ln_b = ln_in_b[None, :]

    w_proj = jnp.pad(w_in[0].astype(BF16), ((0, 0), (0, N_PROJ - w_in.shape[2])))
    w_gate = jnp.pad(gla_w_gate_up[0], ((0, LANES - GLA_RANK), (0, 0))).astype(BF16)
    inv = ROPE_THETA ** (-jnp.arange(0, DIFF_DH, 2, dtype=F32) / DIFF_DH)
    qk, dv, gq, gk, gv, gr, gl, mild = _proj(x, positions[:, None, :], inv[:, None], ln_g, ln_b, sc_a, sh_a,
                                       w_proj, w_gate, gla_b_gate, tm)

    d_lo, d_hi = _diffattn(qk, dv, lambda_q1, lambda_k1, lambda_q2, lambda_k2, diff_norm_w, tm, 2)
    g_out = _gla(mild[:, :, 0, 0], gq, gk, gv, gl, gr, gla_norm_w, tm // 2, 4)

    wd = w_down[0].astype(BF16).reshape(nf, tf, D)
    return _mlp(x, d_lo, d_hi, g_out, ln_g, ln_b, gt_a, w_out[0].astype(BF16), ln_attn_g, ln_attn_b,
                sc_f, sh_f, gt_f, w_up[0].astype(BF16), conv_w[0], conv_b, wd, ln_ffn_g, ln_ffn_b, tm)
```
